```python
import jax
import jax.numpy as jnp
from jax import lax
import numpy as np

D_MODEL = 1024
BATCH = 8
SEQ = 4096
DEPTH = 2

N_EVEN = (DEPTH + 1) // 2
N_ODD = DEPTH // 2
D_FF = 2816
GROUP_W = D_MODEL // 2
MIX_W = 2 * GROUP_W
EPS = 1e-6

RET_HEADS = 4
RET_DK = GROUP_W // RET_HEADS
RET_DV = GROUP_W // RET_HEADS
RET_CHUNK = 128
ROPE_BASE = 10000.0

GLA_HEADS = 4
GLA_DK = GROUP_W // (2 * GLA_HEADS)
GLA_DV = GROUP_W // GLA_HEADS
GLA_RANK = 16
GLA_TAU = 16.0
GLA_CHUNK = 64

LRU_W = GROUP_W
LRU_BLOCKS = 8
LRU_BS = LRU_W // LRU_BLOCKS
LRU_C = 8.0
CONV_W = 4

ML_HEADS = 4
ML_DH = GROUP_W // ML_HEADS
ML_CHUNK = 128

EVEN_SPLITS = (RET_HEADS * RET_DK, RET_HEADS * RET_DK, RET_HEADS * RET_DV, RET_HEADS * RET_DV,
               GLA_HEADS * GLA_DK, GLA_HEADS * GLA_DK, GLA_HEADS * GLA_DV, GLA_HEADS * GLA_DV, GLA_RANK)
EVEN_IN = sum(EVEN_SPLITS)
ODD_SPLITS = (LRU_W, LRU_W, GROUP_W, GROUP_W, GROUP_W, ML_HEADS, ML_HEADS)
ODD_IN = sum(ODD_SPLITS)

kernel_name = 'hybrid_retnet_gla_rglru_mlstm_macaron'


def rmsnorm(x, w):
    xf = x.astype(jnp.float32)
    y = xf * lax.rsqrt(jnp.mean(xf * xf, axis=-1, keepdims=True) + EPS)
    return (y * w.astype(jnp.float32)).astype(x.dtype)


def head_rms(x):
    xf = x.astype(jnp.float32)
    return xf * lax.rsqrt(jnp.mean(xf * xf, axis=-1, keepdims=True) + EPS)


def swiglu(x, w_gu, w_down):
    g, u = jnp.split(x @ w_gu, 2, axis=-1)
    return (jax.nn.silu(g) * u) @ w_down


def split_cols(y, sizes):
    return jnp.split(y, np.cumsum(sizes)[:-1].tolist(), axis=-1)


def to_chunks(x, chunk):
    B, S, H, d = x.shape
    return x.reshape(B, S // chunk, chunk, H, d).transpose(0, 3, 1, 2, 4)


def from_chunks(x):
    B, H, N, L, d = x.shape
    return x.transpose(0, 2, 3, 1, 4).reshape(B, N * L, H, d)


def scan_states(decay, upd):
    def step(s, du):
        d, u = du
        return d * s + u, s
    _, states = lax.scan(step, jnp.zeros_like(upd[0]), (decay, upd))
    return states


def rotary(x, pos):
    half = x.shape[-1] // 2
    inv = ROPE_BASE ** (-jnp.arange(half, dtype=jnp.float32) / half)
    ang = pos.astype(jnp.float32)[:, None] * inv[None, :]
    cos = jnp.cos(ang)[None, :, None, :]
    sin = jnp.sin(ang)[None, :, None, :]
    x1, x2 = x[..., :half], x[..., half:]
    return jnp.concatenate([x1 * cos - x2 * sin, x1 * sin + x2 * cos], axis=-1)


def causal_dwconv(x, w, b):
    out = lax.conv_general_dilated(x, w[:, None, :].astype(x.dtype), window_strides=(1,),
                                   padding=[(CONV_W - 1, 0)], dimension_numbers=('NWC', 'WIO', 'NWC'),
                                   feature_group_count=x.shape[-1])
    return out + b.astype(x.dtype)


def retention(q, k, v):
    B, S, H, _ = q.shape
    L = RET_CHUNK
    N = S // L
    log_g = jnp.log1p(-jnp.exp2(-5.0 - jnp.arange(H, dtype=jnp.float32)))
    qc, kc, vc = to_chunks(q, L), to_chunks(k, L), to_chunks(v, L)
    idx = jnp.arange(L, dtype=jnp.float32)
    rel = idx[:, None] - idx[None, :]
    causal = rel >= 0
    dmat = jnp.where(causal, jnp.exp(log_g[:, None, None] * jnp.where(causal, rel, 0.0)), 0.0)
    s = jnp.einsum('bhnld,bhnsd->bhnls', qc, kc) * dmat[None, :, None]
    intra = jnp.einsum('bhnls,bhnsv->bhnlv', s, vc)
    w_end = jnp.exp(log_g[:, None] * (L - 1.0 - idx)[None, :])
    upd = jnp.einsum('bhnsd,hs,bhnsv->nbhdv', kc, w_end, vc)
    decay = jnp.broadcast_to(jnp.exp(log_g * L)[None, None, :, None, None], (N, 1, H, 1, 1))
    states = scan_states(decay, upd)
    w_in = jnp.exp(log_g[:, None] * (idx + 1.0)[None, :])
    inter = jnp.einsum('bhnld,nbhdv->bhnlv', qc, states) * w_in[None, :, None, :, None]
    return from_chunks(intra + inter)


def gla(q, k, v, log_a):
    B, S, H, _ = q.shape
    L = GLA_CHUNK
    qc, kc, vc, gc = to_chunks(q, L), to_chunks(k, L), to_chunks(v, L), to_chunks(log_a, L)
    b = jnp.cumsum(gc, axis=3)
    b_mid = b[:, :, :, L // 2 - 1:L // 2]
    causal = jnp.tril(jnp.ones((L, L), dtype=bool))
    s = jnp.einsum('bhnld,bhnsd->bhnls', qc * jnp.exp(b - b_mid), kc * jnp.exp(b_mid - b))
    s = jnp.where(causal, s, 0.0)
    intra = jnp.einsum('bhnls,bhnsv->bhnlv', s, vc)
    b_end = b[:, :, :, -1:]
    upd = jnp.einsum('bhnsd,bhnsv->nbhdv', kc * jnp.exp(b_end - b), vc)
    decay = jnp.exp(b_end[:, :, :, 0]).transpose(2, 0, 1, 3)[..., None]
    states = scan_states(decay, upd)
    inter = jnp.einsum('bhnld,nbhdv->bhnlv', qc * jnp.exp(b), states)
    return from_chunks(intra + inter)


def rg_lru(xc, w_a, b_a, w_x, b_x, lam):
    B, S, C = xc.shape
    xb = xc.reshape(B, S, LRU_BLOCKS, LRU_BS)
    r = jax.nn.sigmoid(jnp.einsum('bsgi,gij->bsgj', xb, w_a).reshape(B, S, C) + b_a)
    i = jax.nn.sigmoid(jnp.einsum('bsgi,gij->bsgj', xb, w_x).reshape(B, S, C) + b_x)
    log_a = -LRU_C * r * jax.nn.softplus(-lam)
    a = jnp.exp(log_a)
    u = jnp.sqrt(-jnp.expm1(2.0 * log_a)) * (i * xc)

    def combine(left, right):
        a1, b1 = left
        a2, b2 = right
        return a1 * a2, a2 * b1 + b2

    _, h = lax.associative_scan(combine, (a, u), axis=1)
    return h


def mlstm(q, k, v, i_pre, f_pre):
    B, S, H, _ = q.shape
    L = ML_CHUNK
    N = S // L
    qc, kc, vc = to_chunks(q, L), to_chunks(k, L), to_chunks(v, L)
    ic = i_pre.reshape(B, N, L, H).transpose(0, 3, 1, 2)
    bc = jnp.cumsum(jax.nn.log_sigmoid(f_pre).reshape(B, N, L, H).transpose(0, 3, 1, 2), axis=-1)
    causal = jnp.tril(jnp.ones((L, L), dtype=bool))
    log_w = jnp.where(causal, bc[..., :, None] - bc[..., None, :] + ic[..., None, :], -jnp.inf)
    m_intra = jnp.max(log_w, axis=-1)
    b_end = bc[..., -1]
    a = b_end[..., None] - bc + ic
    a_max = jnp.max(a, axis=-1)
    w_end = jnp.exp(a - a_max[..., None])
    upd_c = jnp.einsum('bhns,bhnsd,bhnsv->nbhdv', w_end, kc, vc)
    upd_n = jnp.einsum('bhns,bhnsd->nbhd', w_end, kc)

    def step(carry, xs):
        c, n, m = carry
        uc, un, be, am = xs
        m_new = jnp.maximum(be + m, am)
        g_old = jnp.exp(be + m - m_new)
        g_new = jnp.exp(am - m_new)
        c_new = g_old[..., None, None] * c + g_new[..., None, None] * uc
        n_new = g_old[..., None] * n + g_new[..., None] * un
        return (c_new, n_new, m_new), (c, n, m)

    init = (jnp.zeros_like(upd_c[0]), jnp.zeros_like(upd_n[0]), jnp.zeros_like(b_end[:, :, 0]))
    _, (cs, ns, ms) = lax.scan(step, init, (upd_c, upd_n, b_end.transpose(2, 0, 1), a_max.transpose(2, 0, 1)))
    m_inter = bc + ms.transpose(1, 2, 0)[..., None]
    m_t = jnp.maximum(m_inter, m_intra)
    s = jnp.einsum('bhnld,bhnsd->bhnls', qc, kc) * jnp.exp(log_w - m_t[..., None])
    w_inter = jnp.exp(m_inter - m_t)
    num = jnp.einsum('bhnls,bhnsv->bhnlv', s, vc) + w_inter[..., None] * jnp.einsum('bhnld,nbhdv->bhnlv', qc, cs)
    den = jnp.sum(s, axis=-1) + w_inter * jnp.einsum('bhnld,nbhd->bhnl', qc, ns)
    h = num / jnp.maximum(jnp.abs(den), jnp.exp(-m_t))[..., None]
    return from_chunks(h)


def even_mixer(hn, w_in, w_lr_up, b_lr, head_norm_w, w_out):
    B, S, _ = hn.shape
    f32 = lambda t: t.astype(jnp.float32)
    rq, rk, rv, rg, gq, gk, gv, gg, glr = split_cols(hn @ w_in, EVEN_SPLITS)
    pos = jnp.arange(S)
    rq = rotary(f32(rq).reshape(B, S, RET_HEADS, RET_DK), pos)
    rk = rotary(f32(rk).reshape(B, S, RET_HEADS, RET_DK), pos) * (RET_DK ** -0.5)
    ret = retention(rq, rk, f32(rv).reshape(B, S, RET_HEADS, RET_DV))
    log_a = jax.nn.log_sigmoid(f32(glr) @ f32(w_lr_up) + f32(b_lr)) / GLA_TAU
    go = gla(f32(gq).reshape(B, S, GLA_HEADS, GLA_DK),
             f32(gk).reshape(B, S, GLA_HEADS, GLA_DK) * (GLA_DK ** -0.5),
             f32(gv).reshape(B, S, GLA_HEADS, GLA_DV),
             log_a.reshape(B, S, GLA_HEADS, GLA_DK))
    o = jnp.concatenate([head_rms(ret).reshape(B, S, -1) * jax.nn.silu(f32(rg)),
                         head_rms(go).reshape(B, S, -1) * jax.nn.silu(f32(gg))], axis=-1)
    o = o * f32(head_norm_w)
    return o.astype(hn.dtype) @ w_out


def odd_mixer(hn, w_in, lru_conv_w, lru_conv_b, lru_wa, lru_ba, lru_wx, lru_bx, lru_lambda,
              ml_conv_w, ml_conv_b, ml_wq, ml_wk, ml_bi, ml_bf, ml_norm_w, w_out):
    B, S, _ = hn.shape
    f32 = lambda t: t.astype(jnp.float32)
    ly, lx, mu, mv, mo, mi, mf = split_cols(hn @ w_in, ODD_SPLITS)
    lxc = causal_dwconv(f32(lx), f32(lru_conv_w), lru_conv_b)
    lru = rg_lru(lxc, f32(lru_wa), f32(lru_ba), f32(lru_wx), f32(lru_bx), f32(lru_lambda))
    lru = lru * jax.nn.gelu(f32(ly))
    mc = jax.nn.silu(causal_dwconv(f32(mu), f32(ml_conv_w), ml_conv_b)).reshape(B, S, ML_HEADS, ML_DH)
    q = jnp.einsum('bshi,hij->bshj', mc, f32(ml_wq))
    k = jnp.einsum('bshi,hij->bshj', mc, f32(ml_wk)) * (ML_DH ** -0.5)
    hm = mlstm(q, k, f32(mv).reshape(B, S, ML_HEADS, ML_DH), f32(mi) + f32(ml_bi), f32(mf) + f32(ml_bf))
    hm = jax.nn.sigmoid(f32(mo)).reshape(B, S, ML_HEADS, ML_DH) * hm
    hm = head_rms(hm).reshape(B, S, -1) * f32(ml_norm_w)
    o = jnp.concatenate([lru, hm], axis=-1)
    return o.astype(hn.dtype) @ w_out


def setup_inputs(seed: int = 0) -> dict:
    key = jax.random.key(seed)
    ks = jax.random.split(key, 40)
    nrm = lambda k, shape, scale: jax.random.normal(k, shape, dtype=jnp.float32) * scale
    gain = lambda k, shape: 1.0 + 0.02 * jax.random.normal(k, shape, dtype=jnp.float32)
    a8 = jax.random.uniform(ks[30], (N_ODD, LRU_W), minval=0.9, maxval=0.999, dtype=jnp.float32)
    a1 = a8 ** (1.0 / LRU_C)
    lam = jnp.log(a1) - jnp.log1p(-a1)
    bf = jnp.broadcast_to(jnp.linspace(3.0, 6.0, ML_HEADS, dtype=jnp.float32), (N_ODD, ML_HEADS))
    return {
        'x': nrm(ks[0], (BATCH, SEQ, D_MODEL), 1.0),
        'ffn1_norm': gain(ks[1], (DEPTH, D_MODEL)),
        'ffn1_wgu': nrm(ks[2], (DEPTH, D_MODEL, 2 * D_FF), D_MODEL ** -0.5),
        'ffn1_wd': nrm(ks[3], (DEPTH, D_FF, D_MODEL), D_FF ** -0.5),
        'mix_norm': gain(ks[4], (DEPTH, D_MODEL)),
        'ffn2_norm': gain(ks[5], (DEPTH, D_MODEL)),
        'ffn2_wgu': nrm(ks[6], (DEPTH, D_MODEL, 2 * D_FF), D_MODEL ** -0.5),
        'ffn2_wd': nrm(ks[7], (DEPTH, D_FF, D_MODEL), D_FF ** -0.5),
        'e_w_in': nrm(ks[8], (N_EVEN, D_MODEL, EVEN_IN), D_MODEL ** -0.5),
        'e_w_lr_up': nrm(ks[9], (N_EVEN, GLA_RANK, GLA_HEADS * GLA_DK), GLA_RANK ** -0.5),
        'e_b_lr': nrm(ks[10], (N_EVEN, GLA_HEADS * GLA_DK), 0.1),
        'e_head_norm': gain(ks[11], (N_EVEN, MIX_W)),
        'e_w_out': nrm(ks[12], (N_EVEN, MIX_W, D_MODEL), MIX_W ** -0.5),
        'o_w_in': nrm(ks[13], (N_ODD, D_MODEL, ODD_IN), D_MODEL ** -0.5),
        'o_lru_conv_w': nrm(ks[14], (N_ODD, CONV_W, LRU_W), CONV_W ** -0.5),
        'o_lru_conv_b': nrm(ks[15], (N_ODD, LRU_W), 0.01),
        'o_lru_wa': nrm(ks[16], (N_ODD, LRU_BLOCKS, LRU_BS, LRU_BS), LRU_BS ** -0.5),
        'o_lru_ba': nrm(ks[17], (N_ODD, LRU_W), 0.01),
        'o_lru_wx': nrm(ks[18], (N_ODD, LRU_BLOCKS, LRU_BS, LRU_BS), LRU_BS ** -0.5),
        'o_lru_bx': nrm(ks[19], (N_ODD, LRU_W), 0.01),
        'o_lru_lambda': lam,
        'o_ml_conv_w': nrm(ks[20], (N_ODD, CONV_W, GROUP_W), CONV_W ** -0.5),
        'o_ml_conv_b': nrm(ks[21], (N_ODD, GROUP_W), 0.01),
        'o_ml_wq': nrm(ks[22], (N_ODD, ML_HEADS, ML_DH, ML_DH), ML_DH ** -0.5),
        'o_ml_wk': nrm(ks[23], (N_ODD, ML_HEADS, ML_DH, ML_DH), ML_DH ** -0.5),
        'o_ml_bi': nrm(ks[24], (N_ODD, ML_HEADS), 0.1),
        'o_ml_bf': bf + nrm(ks[25], (N_ODD, ML_HEADS), 0.01),
        'o_ml_norm': gain(ks[26], (N_ODD, GROUP_W)),
        'o_w_out': nrm(ks[27], (N_ODD, MIX_W, D_MODEL), MIX_W ** -0.5),
        'final_norm': gain(ks[28], (D_MODEL,)),
    }


def reference(x, ffn1_norm, ffn1_wgu, ffn1_wd, mix_norm, ffn2_norm, ffn2_wgu, ffn2_wd,
              e_w_in, e_w_lr_up, e_b_lr, e_head_norm, e_w_out,
              o_w_in, o_lru_conv_w, o_lru_conv_b, o_lru_wa, o_lru_ba, o_lru_wx, o_lru_bx, o_lru_lambda,
              o_ml_conv_w, o_ml_conv_b, o_ml_wq, o_ml_wk, o_ml_bi, o_ml_bf, o_ml_norm, o_w_out,
              final_norm):
    h = x
    for layer in range(DEPTH):
        h = h + 0.5 * swiglu(rmsnorm(h, ffn1_norm[layer]), ffn1_wgu[layer], ffn1_wd[layer])
        hn = rmsnorm(h, mix_norm[layer])
        j = layer // 2
        if layer % 2 == 0:
            h = h + even_mixer(hn, e_w_in[j], e_w_lr_up[j], e_b_lr[j], e_head_norm[j], e_w_out[j])
        else:
            h = h + odd_mixer(hn, o_w_in[j], o_lru_conv_w[j], o_lru_conv_b[j], o_lru_wa[j], o_lru_ba[j],
                              o_lru_wx[j], o_lru_bx[j], o_lru_lambda[j], o_ml_conv_w[j], o_ml_conv_b[j],
                              o_ml_wq[j], o_ml_wk[j], o_ml_bi[j], o_ml_bf[j], o_ml_norm[j], o_w_out[j])
        h = h + 0.5 * swiglu(rmsnorm(h, ffn2_norm[layer]), ffn2_wgu[layer], ffn2_wd[layer])
    return rmsnorm(h, final_norm)
```

```python
import functools

import numpy as np
import jax
import jax.numpy as jnp
from jax import lax
from jax.experimental import pallas as pl
from jax.experimental.pallas import tpu as pltpu

F32 = jnp.float32
BF16 = jnp.bfloat16

D_MODEL = 1024
D_FF = 2816
GROUP_W = D_MODEL // 2
EPS = 1e-6

RET_HEADS = 4
RET_DK = 128
RET_CHUNK = 128
ROPE_BASE = 10000.0

GLA_HEADS = 4
GLA_DK = 64
GLA_DV = 128
GLA_RANK = 16
GLA_TAU = 16.0
GLA_CHUNK = 64
GLA_QK = GLA_HEADS * GLA_DK

LRU_W = GROUP_W
LRU_BLOCKS = 8
LRU_BS = LRU_W // LRU_BLOCKS
LRU_C = 8.0
CONV_W = 4

ML_HEADS = 4
ML_DH = 128
ML_CHUNK = 128

LANE = 128
SUBLANE = 8
VMEM_LIMIT = 56 * 1024 * 1024

E_RQ, E_RK, E_RV, E_RG = 0, 512, 1024, 1536
E_GQ, E_GK, E_GV, E_GG, E_LR = 2048, 2304, 2560, 3072, 3584
E_NP = 3712
O_LY, O_LX, O_MU, O_MV, O_MO, O_IF = 0, 512, 1024, 1536, 2048, 2560
O_NP = 2688

FFN_TM = 512
FFN_FC = 256
MIX_TS = 512


def _dot(a, b):
    return jnp.dot(a, b, preferred_element_type=F32)


def _dot_nt(a, b):
    return lax.dot_general(a, b, (((1,), (1,)), ((), ())), preferred_element_type=F32)


def _dot_tn(a, b):
    return lax.dot_general(a, b, (((0,), (0,)), ((), ())), preferred_element_type=F32)


def _rms(x):
    return x * lax.rsqrt(jnp.mean(x * x, axis=-1, keepdims=True) + EPS)


def _sigmoid(x):
    return 1.0 / (1.0 + jnp.exp(-x))


def _silu(x):
    return x * _sigmoid(x)


def _log_sigmoid(x):
    return jnp.minimum(x, 0.0) - jnp.log1p(jnp.exp(-jnp.abs(x)))


def _split_bf16(x):
    hi = x.astype(BF16)
    lo = (x - hi.astype(F32)).astype(BF16)
    return hi, lo


def _ffn_body(final, x_ref, nw_ref, wgu_ref, wd_ref, fw_ref, o_ref, act_ref):
    x = x_ref[...]
    xn = (_rms(x) * nw_ref[...]).astype(BF16)
    for j in range(D_FF // FFN_FC):
        c0 = j * FFN_FC
        g = _dot(xn, wgu_ref[:, c0:c0 + FFN_FC])
        u = _dot(xn, wgu_ref[:, D_FF + c0:D_FF + c0 + FFN_FC])
        act_ref[:, c0:c0 + FFN_FC] = (_silu(g) * u).astype(BF16)
    h = x + 0.5 * _dot(act_ref[...], wd_ref[...])
    if final:
        h = _rms(h) * fw_ref[...]
    o_ref[...] = h


def _ffn(h, norm_w, wgu, wd, final_w, final):
    t = h.shape[0]
    const = lambda shape: pl.BlockSpec(shape, lambda i: (0, 0), pipeline_mode=pl.Buffered(1))
    return pl.pallas_call(
        functools.partial(_ffn_body, final),
        grid=(t // FFN_TM,),
        in_specs=[
            pl.BlockSpec((FFN_TM, D_MODEL), lambda i: (i, 0)),
            const((1, D_MODEL)),
            const((D_MODEL, 2 * D_FF)),
            const((D_FF, D_MODEL)),
            const((1, D_MODEL)),
        ],
        out_specs=pl.BlockSpec((FFN_TM, D_MODEL), lambda i: (i, 0)),
        out_shape=jax.ShapeDtypeStruct((t, D_MODEL), F32),
        scratch_shapes=[pltpu.VMEM((FFN_TM, D_FF), BF16)],
        compiler_params=pltpu.CompilerParams(
            dimension_semantics=("parallel",), vmem_limit_bytes=VMEM_LIMIT),
        name="ffn_final" if final else "ffn",
    )(h, norm_w.reshape(1, -1), wgu, wd, final_w.reshape(1, -1))


def _even_body(x_ref, nw_ref, win_ref, wlr_ref, blr_ref, cos_ref, sin_ref,
               dmat_ref, winb_ref, wendb_ref, dec_ref, tri_ref, cbd_ref, ones_ref,
               hnw_ref, wout_ref, o_ref,
               proj, labuf, obuf, ret_s, gla_s):
    ts = x_ref.shape[0]

    @pl.when(pl.program_id(1) == 0)
    def _():
        ret_s[...] = jnp.zeros_like(ret_s)
        gla_s[...] = jnp.zeros_like(gla_s)

    x = x_ref[...]
    hn = (_rms(x) * nw_ref[...]).astype(BF16)
    for c0 in range(0, E_NP, 512):
        c1 = min(c0 + 512, E_NP)
        proj[:, c0:c1] = _dot(hn, win_ref[:, c0:c1])

    glr = proj[:, E_LR:E_LR + LANE].astype(BF16)
    labuf[...] = _log_sigmoid(_dot(glr, wlr_ref[...]) + blr_ref[...]) * (1.0 / GLA_TAU)

    cos = cos_ref[...]
    sin = sin_ref[...]
    for c0 in range(0, 2 * RET_HEADS * RET_DK, RET_DK):
        xx = proj[:, c0:c0 + RET_DK]
        r = xx * cos + pltpu.roll(xx, RET_DK // 2, 1) * sin
        if c0 >= E_RK:
            r = r * (RET_DK ** -0.5)
        proj[:, c0:c0 + RET_DK] = r

    def ret_chunk(c, carry):
        r0 = pl.multiple_of(c * RET_CHUNK, RET_CHUNK)
        rows = pl.ds(r0, RET_CHUNK)
        for h in range(RET_HEADS):
            q = proj[rows, E_RQ + h * RET_DK:E_RQ + (h + 1) * RET_DK].astype(BF16)
            k = proj[rows, E_RK + h * RET_DK:E_RK + (h + 1) * RET_DK].astype(BF16)
            v = proj[rows, E_RV + h * RET_DK:E_RV + (h + 1) * RET_DK]
            g = proj[rows, E_RG + h * RET_DK:E_RG + (h + 1) * RET_DK]
            s = _dot_nt(q, k) * dmat_ref[h]
            st = ret_s[h]
            o = _dot(s.astype(BF16), v.astype(BF16)) + _dot(q, st.astype(BF16)) * winb_ref[h]
            ret_s[h] = dec_ref[h] * st + _dot_tn(k, (wendb_ref[h] * v).astype(BF16))
            obuf[rows, h * RET_DK:(h + 1) * RET_DK] = _rms(o) * _silu(g)
        return carry

    lax.fori_loop(0, ts // RET_CHUNK, ret_chunk, 0)

    lane_head = lax.broadcasted_iota(jnp.int32, (1, GLA_QK), 1) // GLA_DK

    def stack_heads(z):
        return jnp.concatenate(
            [jnp.where(lane_head == h, z, 0.0) for h in range(GLA_HEADS)], axis=0).astype(BF16)

    def gla_chunk(c, carry):
        r0 = pl.multiple_of(c * GLA_CHUNK, GLA_CHUNK)
        rows = pl.ds(r0, GLA_CHUNK)
        la_hi, la_lo = _split_bf16(labuf[rows, :])
        tri = tri_ref[...]
        b = _dot(tri, la_hi) + _dot(tri, la_lo)
        b_mid = b[GLA_CHUNK // 2 - 1:GLA_CHUNK // 2, :]
        b_end = b[GLA_CHUNK - 1:GLA_CHUNK, :]
        q = proj[rows, E_GQ:E_GQ + GLA_QK]
        k = proj[rows, E_GK:E_GK + GLA_QK] * (GLA_DK ** -0.5)
        v = proj[rows, E_GV:E_GV + GLA_HEADS * GLA_DV]
        s = _dot_nt(stack_heads(q * jnp.exp(b - b_mid)), stack_heads(k * jnp.exp(b_mid - b)))
        s = s * cbd_ref[...]
        vst = jnp.concatenate(
            [v[:, h * GLA_DV:(h + 1) * GLA_DV] for h in range(GLA_HEADS)], axis=0).astype(BF16)
        st = gla_s[...]
        out = _dot(s.astype(BF16), vst) + _dot(stack_heads(q * jnp.exp(b)), st.astype(BF16))
        upd_full = _dot_tn((k * jnp.exp(b_end - b)).astype(BF16), v.astype(BF16))
        upd = jnp.concatenate(
            [upd_full[h * GLA_DK:(h + 1) * GLA_DK, h * GLA_DV:(h + 1) * GLA_DV]
             for h in range(GLA_HEADS)], axis=0)
        ones = ones_ref[...]
        decay = jnp.exp(_dot_tn(la_hi, ones) + _dot_tn(la_lo, ones))
        gla_s[...] = decay * st + upd
        for h in range(GLA_HEADS):
            g = proj[rows, E_GG + h * GLA_DV:E_GG + (h + 1) * GLA_DV]
            o = out[h * GLA_CHUNK:(h + 1) * GLA_CHUNK, :]
            obuf[rows, GROUP_W + h * GLA_DV:GROUP_W + (h + 1) * GLA_DV] = _rms(o) * _silu(g)
        return carry

    lax.fori_loop(0, ts // GLA_CHUNK, gla_chunk, 0)

    o_ref[...] = x + _dot((obuf[...] * hnw_ref[...]).astype(BF16), wout_ref[...])


def _even_mixer(h, norm_w, w_in, w_lr, b_lr, head_norm, w_out, ts):
    bsz, seq, _ = h.shape
    hh = jnp.arange(RET_HEADS, dtype=F32)
    log_g = jnp.log1p(-jnp.exp2(-5.0 - hh))
    idx = jnp.arange(RET_CHUNK, dtype=F32)
    rel = idx[:, None] - idx[None, :]
    causal = rel >= 0
    dmat = jnp.where(causal, jnp.exp(log_g[:, None, None] * jnp.where(causal, rel, 0.0)), 0.0)
    full = (RET_HEADS, RET_CHUNK, RET_DK)
    w_end = jnp.broadcast_to(jnp.exp(log_g[:, None] * (RET_CHUNK - 1.0 - idx)[None, :])[:, :, None], full)
    w_inb = jnp.broadcast_to(jnp.exp(log_g[:, None] * (idx + 1.0)[None, :])[:, :, None], full)
    dec = jnp.broadcast_to(jnp.exp(log_g * RET_CHUNK)[:, None, None], full)

    half = RET_DK // 2
    inv = ROPE_BASE ** (-jnp.arange(half, dtype=F32) / half)
    ang = jnp.arange(seq).astype(F32)[:, None] * inv[None, :]
    cos2 = jnp.concatenate([jnp.cos(ang), jnp.cos(ang)], axis=-1)
    sin2 = jnp.concatenate([-jnp.sin(ang), jnp.sin(ang)], axis=-1)

    tri = jnp.tril(jnp.ones((GLA_CHUNK, GLA_CHUNK), F32)).astype(BF16)
    blk = jnp.arange(GLA_QK) // GLA_CHUNK
    pos = jnp.arange(GLA_QK) % GLA_CHUNK
    cbd = ((blk[:, None] == blk[None, :]) & (pos[:, None] >= pos[None, :])).astype(F32)
    ones = jnp.ones((GLA_CHUNK, LANE), BF16)

    w_in_p = jnp.pad(w_in, ((0, 0), (0, E_NP - w_in.shape[1]))).astype(BF16)
    w_lr_p = jnp.pad(w_lr, ((0, LANE - GLA_RANK), (0, 0))).astype(BF16)

    const2 = lambda shape: pl.BlockSpec(shape, lambda b, s: (0, 0), pipeline_mode=pl.Buffered(1))
    const3 = lambda shape: pl.BlockSpec(shape, lambda b, s: (0, 0, 0), pipeline_mode=pl.Buffered(1))
    return pl.pallas_call(
        _even_body,
        grid=(bsz, seq // ts),
        in_specs=[
            pl.BlockSpec((None, ts, D_MODEL), lambda b, s: (b, s, 0)),
            const2((1, D_MODEL)),
            const2((D_MODEL, E_NP)),
            const2((LANE, GLA_QK)),
            const2((1, GLA_QK)),
            pl.BlockSpec((ts, RET_DK), lambda b, s: (s, 0)),
            pl.BlockSpec((ts, RET_DK), lambda b, s: (s, 0)),
            const3(full), const3(full), const3(full), const3(full),
            const2((GLA_CHUNK, GLA_CHUNK)),
            const2((GLA_QK, GLA_QK)),
            const2((GLA_CHUNK, LANE)),
            const2((1, D_MODEL)),
            const2((D_MODEL, D_MODEL)),
        ],
        out_specs=pl.BlockSpec((None, ts, D_MODEL), lambda b, s: (b, s, 0)),
        out_shape=jax.ShapeDtypeStruct(h.shape, F32),
        scratch_shapes=[
            pltpu.VMEM((ts, E_NP), F32),
            pltpu.VMEM((ts, GLA_QK), F32),
            pltpu.VMEM((ts, D_MODEL), F32),
            pltpu.VMEM(full, F32),
            pltpu.VMEM((GLA_QK, GLA_DV), F32),
        ],
        compiler_params=pltpu.CompilerParams(
            dimension_semantics=("parallel", "arbitrary"), vmem_limit_bytes=VMEM_LIMIT),
        name="even_mixer",
    )(h, norm_w.reshape(1, -1), w_in_p, w_lr_p, b_lr.reshape(1, -1), cos2, sin2,
      dmat, w_inb, w_end, dec, tri, cbd, ones, head_norm.reshape(1, -1), w_out.astype(BF16))


def _causal_conv(buf, x, w_ref, b_ref, ts):
    buf[SUBLANE:SUBLANE + ts, :] = x
    acc = b_ref[...] + w_ref[CONV_W - 1:CONV_W, :] * x
    for j in range(CONV_W - 1):
        acc = acc + w_ref[j:j + 1, :] * buf[pl.ds(SUBLANE - (CONV_W - 1) + j, ts), :]
    buf[0:SUBLANE, :] = buf[ts:ts + SUBLANE, :]
    return acc


def _odd_body(x_ref, nw_ref, win_ref, lcw_ref, lcb_ref, mcw_ref, mcb_ref,
              wa_ref, ba_ref, wx_ref, bx_ref, lam_ref, wqk_ref, bif_ref, tri_ref,
              mnw_ref, wout_ref, o_ref,
              proj, lbuf, mbuf, abuf, ubuf, qk, ibuf, fbuf, obuf, lru_h, ml_c, ml_m):
    ts = x_ref.shape[0]

    @pl.when(pl.program_id(1) == 0)
    def _():
        lbuf[0:SUBLANE, :] = jnp.zeros((SUBLANE, LRU_W), F32)
        mbuf[0:SUBLANE, :] = jnp.zeros((SUBLANE, GROUP_W), F32)
        lru_h[...] = jnp.zeros_like(lru_h)
        ml_c[...] = jnp.zeros_like(ml_c)
        ml_m[...] = jnp.zeros_like(ml_m)

    x = x_ref[...]
    hn = (_rms(x) * nw_ref[...]).astype(BF16)
    for c0 in range(0, O_NP, 512):
        c1 = min(c0 + 512, O_NP)
        proj[:, c0:c1] = _dot(hn, win_ref[:, c0:c1])

    lxc = _causal_conv(lbuf, proj[:, O_LX:O_LX + LRU_W], lcw_ref, lcb_ref, ts)
    xb = lxc.astype(BF16)
    r = _sigmoid(_dot(xb, wa_ref[...]) + ba_ref[...])
    i = _sigmoid(_dot(xb, wx_ref[...]) + bx_ref[...])
    lam = lam_ref[...]
    softplus_neg_lam = jnp.maximum(-lam, 0.0) + jnp.log1p(jnp.exp(-jnp.abs(lam)))
    log_a = (-LRU_C) * r * softplus_neg_lam
    a = jnp.exp(log_a)
    u = jnp.sqrt(-jnp.tanh(log_a) * (a * a + 1.0)) * (i * lxc)
    rmod = lax.broadcasted_iota(jnp.int32, (ts, 1), 0) % SUBLANE
    d = 1
    while d < SUBLANE:
        keep = rmod >= d
        a_s = jnp.where(keep, pltpu.roll(a, d, 0), 1.0)
        u_s = jnp.where(keep, pltpu.roll(u, d, 0), 0.0)
        u = a * u_s + u
        a = a * a_s
        d *= 2
    abuf[...] = a
    ubuf[...] = u

    def lru_group(g, hprev):
        rows = pl.ds(pl.multiple_of(g * SUBLANE, SUBLANE), SUBLANE)
        hg = ubuf[rows, :] + abuf[rows, :] * hprev
        ubuf[rows, :] = hg
        return hg[SUBLANE - 1:SUBLANE, :]

    lru_h[0:1, :] = lax.fori_loop(0, ts // SUBLANE, lru_group, lru_h[0:1, :], unroll=8)
    ly = proj[:, O_LY:O_LY + LRU_W]
    gelu = ly * (0.5 * (1.0 + jnp.tanh(np.float32(np.sqrt(2.0 / np.pi)) * (ly + 0.044715 * (ly * ly * ly)))))
    obuf[:, 0:LRU_W] = ubuf[...] * gelu

    mc = _silu(_causal_conv(mbuf, proj[:, O_MU:O_MU + GROUP_W], mcw_ref, mcb_ref, ts))
    for h in range(ML_HEADS):
        qkh = _dot(mc[:, h * ML_DH:(h + 1) * ML_DH].astype(BF16), wqk_ref[h])
        qk[:, h * ML_DH:(h + 1) * ML_DH] = qkh[:, :ML_DH]
        qk[:, GROUP_W + h * ML_DH:GROUP_W + (h + 1) * ML_DH] = qkh[:, ML_DH:] * (ML_DH ** -0.5)
    gates = proj[:, O_IF:O_IF + LANE] + bif_ref[...]
    ibuf[...] = gates
    fbuf[...] = pltpu.roll(_log_sigmoid(gates), LANE - ML_HEADS, 1)

    row_i = lax.broadcasted_iota(jnp.int32, (ML_CHUNK, ML_CHUNK), 0)
    col_i = lax.broadcasted_iota(jnp.int32, (ML_CHUNK, ML_CHUNK), 1)
    causal = row_i >= col_i
    ones_col = jnp.where(col_i == 0, 1.0, 0.0).astype(BF16)

    def ml_chunk(c, carry):
        r0 = pl.multiple_of(c * ML_CHUNK, ML_CHUNK)
        rows = pl.ds(r0, ML_CHUNK)
        f_hi, f_lo = _split_bf16(fbuf[rows, :])
        tri = tri_ref[...]
        fc = _dot(tri, f_hi) + _dot(tri, f_lo)
        dm = ibuf[rows, :] - fc
        dm_t = dm.T
        b_end = fc[ML_CHUNK - 1:ML_CHUNK, :]
        a_col = b_end + dm
        a_max = jnp.max(a_col, axis=0, keepdims=True)
        w_end = jnp.exp(a_col - a_max)
        m_old = ml_m[0:1, :]
        m_new = jnp.maximum(b_end + m_old, a_max)
        g_old = jnp.exp(b_end + m_old - m_new)
        g_new = jnp.exp(a_max - m_new)
        for h in range(ML_HEADS):
            hs = slice(h * ML_DH, (h + 1) * ML_DH)
            qf = qk[rows, hs]
            kf = qk[rows, GROUP_W + h * ML_DH:GROUP_W + (h + 1) * ML_DH]
            q = qf.astype(BF16)
            v = proj[rows, O_MV + h * ML_DH:O_MV + (h + 1) * ML_DH].astype(BF16)
            d_row = dm_t[h:h + 1, :]
            m_h = m_old[:, h:h + 1]
            cm = jnp.max(jnp.where(causal, d_row, -jnp.inf), axis=1, keepdims=True)
            mx = jnp.maximum(cm, m_h)
            p = jnp.where(causal, jnp.exp(d_row - mx), 0.0)
            s = _dot_nt(q, kf.astype(BF16)) * p
            cx = ml_c[h]
            qc = _dot(q, cx.astype(BF16))
            w_inter = jnp.exp(m_h - mx)
            num = _dot(s.astype(BF16), v) + w_inter * qc[:, :ML_DH]
            den = jnp.sum(s, axis=1, keepdims=True) + w_inter * qc[:, ML_DH:ML_DH + 1]
            m_t = fc[:, h:h + 1] + mx
            hh = num / jnp.maximum(jnp.abs(den), jnp.exp(-m_t))
            og = _sigmoid(proj[rows, O_MO + h * ML_DH:O_MO + (h + 1) * ML_DH])
            obuf[rows, GROUP_W + h * ML_DH:GROUP_W + (h + 1) * ML_DH] = _rms(og * hh)
            wk = (w_end[:, h:h + 1] * kf).astype(BF16)
            updx = _dot_tn(wk, jnp.concatenate([v, ones_col], axis=1))
            ml_c[h] = g_old[:, h:h + 1] * cx + g_new[:, h:h + 1] * updx
        ml_m[0:1, :] = m_new
        return carry

    lax.fori_loop(0, ts // ML_CHUNK, ml_chunk, 0)

    obuf[:, GROUP_W:] = obuf[:, GROUP_W:] * mnw_ref[...]
    o_ref[...] = x + _dot(obuf[...].astype(BF16), wout_ref[...])


def _block_diag(w):
    g, n, _ = w.shape
    eye = jnp.eye(g, dtype=w.dtype)
    return (eye[:, None, :, None] * w[:, :, None, :]).reshape(g * n, g * n)


def _odd_mixer(h, norm_w, w_in, lcw, lcb, wa, ba, wx, bx, lam, mcw, mcb, wq, wk, bi, bf, ml_norm, w_out, ts):
    bsz, seq, _ = h.shape
    w_in_p = jnp.pad(w_in, ((0, 0), (0, O_NP - w_in.shape[1]))).astype(BF16)
    wqk = jnp.concatenate([wq, wk], axis=-1).astype(BF16)
    bif = jnp.pad(jnp.concatenate([bi, bf]), (0, LANE - 2 * ML_HEADS)).reshape(1, LANE)
    tri = jnp.tril(jnp.ones((ML_CHUNK, ML_CHUNK), F32)).astype(BF16)
    row = lambda t: t.reshape(1, -1)

    const2 = lambda shape: pl.BlockSpec(shape, lambda b, s: (0, 0), pipeline_mode=pl.Buffered(1))
    const3 = lambda shape: pl.BlockSpec(shape, lambda b, s: (0, 0, 0), pipeline_mode=pl.Buffered(1))
    return pl.pallas_call(
        _odd_body,
        grid=(bsz, seq // ts),
        in_specs=[
            pl.BlockSpec((None, ts, D_MODEL), lambda b, s: (b, s, 0)),
            const2((1, D_MODEL)),
            const2((D_MODEL, O_NP)),
            const2((CONV_W, LRU_W)), const2((1, LRU_W)),
            const2((CONV_W, GROUP_W)), const2((1, GROUP_W)),
            const2((LRU_W, LRU_W)), const2((1, LRU_W)),
            const2((LRU_W, LRU_W)), const2((1, LRU_W)),
            const2((1, LRU_W)),
            const3((ML_HEADS, ML_DH, 2 * ML_DH)),
            const2((1, LANE)),
            const2((ML_CHUNK, ML_CHUNK)),
            const2((1, GROUP_W)),
            const2((D_MODEL, D_MODEL)),
        ],
        out_specs=pl.BlockSpec((None, ts, D_MODEL), lambda b, s: (b, s, 0)),
        out_shape=jax.ShapeDtypeStruct(h.shape, F32),
        scratch_shapes=[
            pltpu.VMEM((ts, O_NP), F32),
            pltpu.VMEM((ts + SUBLANE, LRU_W), F32),
            pltpu.VMEM((ts + SUBLANE, GROUP_W), F32),
            pltpu.VMEM((ts, LRU_W), F32),
            pltpu.VMEM((ts, LRU_W), F32),
            pltpu.VMEM((ts, 2 * GROUP_W), F32),
            pltpu.VMEM((ts, LANE), F32),
            pltpu.VMEM((ts, LANE), F32),
            pltpu.VMEM((ts, D_MODEL), F32),
            pltpu.VMEM((SUBLANE, LRU_W), F32),
            pltpu.VMEM((ML_HEADS, ML_DH, 2 * ML_DH), F32),
            pltpu.VMEM((SUBLANE, LANE), F32),
        ],
        compiler_params=pltpu.CompilerParams(
            dimension_semantics=("parallel", "arbitrary"), vmem_limit_bytes=VMEM_LIMIT),
        name="odd_mixer",
    )(h, row(norm_w), w_in_p, lcw, row(lcb), mcw, row(mcb),
      _block_diag(wa).astype(BF16), row(ba), _block_diag(wx).astype(BF16), row(bx), row(lam),
      wqk, bif, tri, row(ml_norm), w_out.astype(BF16))


def kernel(x, ffn1_norm, ffn1_wgu, ffn1_wd, mix_norm, ffn2_norm, ffn2_wgu, ffn2_wd, e_w_in, e_w_lr_up, e_b_lr, e_head_norm, e_w_out, o_w_in, o_lru_conv_w, o_lru_conv_b, o_lru_wa, o_lru_ba, o_lru_wx, o_lru_bx, o_lru_lambda, o_ml_conv_w, o_ml_conv_b, o_ml_wq, o_ml_wk, o_ml_bi, o_ml_bf, o_ml_norm, o_w_out, final_norm):
    bsz, seq, d = x.shape
    depth = ffn1_norm.shape[0]
    h = x
    for layer in range(depth):
        j = layer // 2
        h = _ffn(h.reshape(bsz * seq, d), ffn1_norm[layer], ffn1_wgu[layer].astype(BF16),
                 ffn1_wd[layer].astype(BF16), final_norm, False).reshape(bsz, seq, d)
        if layer % 2 == 0:
            h = _even_mixer(h, mix_norm[layer], e_w_in[j], e_w_lr_up[j], e_b_lr[j], e_head_norm[j],
                            e_w_out[j], MIX_TS)
        else:
            h = _odd_mixer(h, mix_norm[layer], o_w_in[j], o_lru_conv_w[j], o_lru_conv_b[j],
                           o_lru_wa[j], o_lru_ba[j], o_lru_wx[j], o_lru_bx[j], o_lru_lambda[j],
                           o_ml_conv_w[j], o_ml_conv_b[j], o_ml_wq[j], o_ml_wk[j], o_ml_bi[j], o_ml_bf[j],
                           o_ml_norm[j], o_w_out[j], MIX_TS)
        h = _ffn(h.reshape(bsz * seq, d), ffn2_norm[layer], ffn2_wgu[layer].astype(BF16),
                 ffn2_wd[layer].astype(BF16), final_norm, layer == depth - 1).reshape(bsz, seq, d)
    return h
```

```python
import functools

import numpy as np
import jax
import jax.numpy as jnp
from jax import lax
from jax.experimental import pallas as pl
from jax.experimental.pallas import tpu as pltpu

F32 = jnp.float32
BF16 = jnp.bfloat16

D_MODEL = 1024
D_FF = 2816
GROUP_W = D_MODEL // 2
EPS = 1e-6

RET_HEADS = 4
RET_DK = 128
RET_CHUNK = 128
ROPE_BASE = 10000.0

GLA_HEADS = 4
GLA_DK = 64
GLA_DV = 128
GLA_RANK = 16
GLA_TAU = 16.0
GLA_CHUNK = 64
GLA_QK = GLA_HEADS * GLA_DK

LRU_W = GROUP_W
LRU_BLOCKS = 8
LRU_BS = LRU_W // LRU_BLOCKS
LRU_C = 8.0
CONV_W = 4

ML_HEADS = 4
ML_DH = 128
ML_CHUNK = 128

LANE = 128
SUBLANE = 8
VMEM_LIMIT = 56 * 1024 * 1024

E_RQ, E_RK, E_RV, E_RG = 0, 512, 1024, 1536
E_GQ, E_GK, E_GV, E_GG, E_LR = 2048, 2304, 2560, 3072, 3584
E_NP = 3712
O_LY, O_LX, O_MU, O_MV, O_MO, O_IF = 0, 512, 1024, 1536, 2048, 2560
O_NP = 2688

FFN_TM = 512
FFN_FC = 256
MIX_TS = 256
MIX_G = 2


def _dot(a, b):
    return jnp.dot(a, b, preferred_element_type=F32)


def _dot_nt(a, b):
    return lax.dot_general(a, b, (((1,), (1,)), ((), ())), preferred_element_type=F32)


def _dot_tn(a, b):
    return lax.dot_general(a, b, (((0,), (0,)), ((), ())), preferred_element_type=F32)


def _rms(x):
    return x * lax.rsqrt(jnp.mean(x * x, axis=-1, keepdims=True) + EPS)


def _sigmoid(x):
    return 1.0 / (1.0 + jnp.exp(-x))


def _silu(x):
    return x * _sigmoid(x)


def _log_sigmoid(x):
    return jnp.minimum(x, 0.0) - jnp.log1p(jnp.exp(-jnp.abs(x)))


def _split_bf16(x):
    hi = x.astype(BF16)
    lo = (x - hi.astype(F32)).astype(BF16)
    return hi, lo


def _ffn_body(final, x_ref, nw_ref, wgu_ref, wd_ref, fw_ref, o_ref, act_ref):
    x = x_ref[...]
    xn = (_rms(x) * nw_ref[...]).astype(BF16)
    for j in range(D_FF // FFN_FC):
        c0 = j * FFN_FC
        g = _dot(xn, wgu_ref[:, c0:c0 + FFN_FC])
        u = _dot(xn, wgu_ref[:, D_FF + c0:D_FF + c0 + FFN_FC])
        act_ref[:, c0:c0 + FFN_FC] = (_silu(g) * u).astype(BF16)
    h = x + 0.5 * _dot(act_ref[...], wd_ref[...])
    if final:
        h = _rms(h) * fw_ref[...]
    o_ref[...] = h


def _ffn(h, norm_w, wgu, wd, final_w, final):
    t = h.shape[0]
    const = lambda shape: pl.BlockSpec(shape, lambda i: (0, 0), pipeline_mode=pl.Buffered(1))
    return pl.pallas_call(
        functools.partial(_ffn_body, final),
        grid=(t // FFN_TM,),
        in_specs=[
            pl.BlockSpec((FFN_TM, D_MODEL), lambda i: (i, 0)),
            const((1, D_MODEL)),
            const((D_MODEL, 2 * D_FF)),
            const((D_FF, D_MODEL)),
            const((1, D_MODEL)),
        ],
        out_specs=pl.BlockSpec((FFN_TM, D_MODEL), lambda i: (i, 0)),
        out_shape=jax.ShapeDtypeStruct((t, D_MODEL), F32),
        scratch_shapes=[pltpu.VMEM((FFN_TM, D_FF), BF16)],
        compiler_params=pltpu.CompilerParams(
            dimension_semantics=("parallel",), vmem_limit_bytes=VMEM_LIMIT),
        name="ffn_final" if final else "ffn",
    )(h, norm_w.reshape(1, -1), wgu, wd, final_w.reshape(1, -1))


def _even_body(x_ref, *rest):
    consts, o_ref, scratch = rest[:15], rest[15], rest[16:]
    ret_s, gla_s = scratch[3], scratch[4]

    @pl.when(pl.program_id(1) == 0)
    def _():
        ret_s[...] = jnp.zeros_like(ret_s)
        gla_s[...] = jnp.zeros_like(gla_s)

    for g in range(x_ref.shape[0]):
        _even_one(x_ref.at[g], *consts, o_ref.at[g], *[r.at[g] for r in scratch])


def _even_one(x_ref, nw_ref, win_ref, wlr_ref, blr_ref, cos_ref, sin_ref,
              dmat_ref, winb_ref, wendb_ref, dec_ref, tri_ref, cbd_ref, ones_ref,
              hnw_ref, wout_ref, o_ref,
              proj, labuf, obuf, ret_s, gla_s):
    ts = x_ref.shape[0]
    x = x_ref[...]
    hn = (_rms(x) * nw_ref[...]).astype(BF16)
    for c0 in range(0, E_NP, 512):
        c1 = min(c0 + 512, E_NP)
        proj[:, c0:c1] = _dot(hn, win_ref[:, c0:c1])

    glr = proj[:, E_LR:E_LR + LANE].astype(BF16)
    labuf[...] = _log_sigmoid(_dot(glr, wlr_ref[...]) + blr_ref[...]) * (1.0 / GLA_TAU)

    cos = cos_ref[...]
    sin = sin_ref[...]
    for c0 in range(0, 2 * RET_HEADS * RET_DK, RET_DK):
        xx = proj[:, c0:c0 + RET_DK]
        r = xx * cos + pltpu.roll(xx, RET_DK // 2, 1) * sin
        if c0 >= E_RK:
            r = r * (RET_DK ** -0.5)
        proj[:, c0:c0 + RET_DK] = r

    def ret_chunk(c, carry):
        rows = pl.ds(c * RET_CHUNK, RET_CHUNK)
        for h in range(RET_HEADS):
            q = proj[rows, E_RQ + h * RET_DK:E_RQ + (h + 1) * RET_DK].astype(BF16)
            k = proj[rows, E_RK + h * RET_DK:E_RK + (h + 1) * RET_DK].astype(BF16)
            v = proj[rows, E_RV + h * RET_DK:E_RV + (h + 1) * RET_DK]
            g = proj[rows, E_RG + h * RET_DK:E_RG + (h + 1) * RET_DK]
            s = _dot_nt(q, k) * dmat_ref[h]
            st = ret_s[h]
            o = _dot(s.astype(BF16), v.astype(BF16)) + _dot(q, st.astype(BF16)) * winb_ref[h]
            ret_s[h] = dec_ref[h] * st + _dot_tn(k, (wendb_ref[h] * v).astype(BF16))
            obuf[rows, h * RET_DK:(h + 1) * RET_DK] = _rms(o) * _silu(g)
        return carry

    for c in range(ts // RET_CHUNK):
        ret_chunk(c, 0)

    lane_head = lax.broadcasted_iota(jnp.int32, (1, GLA_QK), 1) // GLA_DK

    def stack_heads(z):
        return jnp.concatenate(
            [jnp.where(lane_head == h, z, 0.0) for h in range(GLA_HEADS)], axis=0).astype(BF16)

    def gla_chunk(c, carry):
        rows = pl.ds(c * GLA_CHUNK, GLA_CHUNK)
        la_hi, la_lo = _split_bf16(labuf[rows, :])
        tri = tri_ref[...]
        b = _dot(tri, la_hi) + _dot(tri, la_lo)
        b_mid = b[GLA_CHUNK // 2 - 1:GLA_CHUNK // 2, :]
        b_end = b[GLA_CHUNK - 1:GLA_CHUNK, :]
        q = proj[rows, E_GQ:E_GQ + GLA_QK]
        k = proj[rows, E_GK:E_GK + GLA_QK] * (GLA_DK ** -0.5)
        v = proj[rows, E_GV:E_GV + GLA_HEADS * GLA_DV]
        s = _dot_nt(stack_heads(q * jnp.exp(b - b_mid)), stack_heads(k * jnp.exp(b_mid - b)))
        s = s * cbd_ref[...]
        vst = jnp.concatenate(
            [v[:, h * GLA_DV:(h + 1) * GLA_DV] for h in range(GLA_HEADS)], axis=0).astype(BF16)
        st = gla_s[...]
        out = _dot(s.astype(BF16), vst) + _dot(stack_heads(q * jnp.exp(b)), st.astype(BF16))
        upd_full = _dot_tn((k * jnp.exp(b_end - b)).astype(BF16), v.astype(BF16))
        upd = jnp.concatenate(
            [upd_full[h * GLA_DK:(h + 1) * GLA_DK, h * GLA_DV:(h + 1) * GLA_DV]
             for h in range(GLA_HEADS)], axis=0)
        ones = ones_ref[...]
        decay = jnp.exp(_dot_tn(la_hi, ones) + _dot_tn(la_lo, ones))
        gla_s[...] = decay * st + upd
        for h in range(GLA_HEADS):
            g = proj[rows, E_GG + h * GLA_DV:E_GG + (h + 1) * GLA_DV]
            o = out[h * GLA_CHUNK:(h + 1) * GLA_CHUNK, :]
            obuf[rows, GROUP_W + h * GLA_DV:GROUP_W + (h + 1) * GLA_DV] = _rms(o) * _silu(g)
        return carry

    for c in range(ts // GLA_CHUNK):
        gla_chunk(c, 0)

    o_ref[...] = x + _dot((obuf[...] * hnw_ref[...]).astype(BF16), wout_ref[...])


def _even_mixer(h, norm_w, w_in, w_lr, b_lr, head_norm, w_out, ts):
    bsz, seq, _ = h.shape
    hh = jnp.arange(RET_HEADS, dtype=F32)
    log_g = jnp.log1p(-jnp.exp2(-5.0 - hh))
    idx = jnp.arange(RET_CHUNK, dtype=F32)
    rel = idx[:, None] - idx[None, :]
    causal = rel >= 0
    dmat = jnp.where(causal, jnp.exp(log_g[:, None, None] * jnp.where(causal, rel, 0.0)), 0.0)
    full = (RET_HEADS, RET_CHUNK, RET_DK)
    w_end = jnp.broadcast_to(jnp.exp(log_g[:, None] * (RET_CHUNK - 1.0 - idx)[None, :])[:, :, None], full)
    w_inb = jnp.broadcast_to(jnp.exp(log_g[:, None] * (idx + 1.0)[None, :])[:, :, None], full)
    dec = jnp.broadcast_to(jnp.exp(log_g * RET_CHUNK)[:, None, None], full)

    half = RET_DK // 2
    inv = ROPE_BASE ** (-jnp.arange(half, dtype=F32) / half)
    ang = jnp.arange(seq).astype(F32)[:, None] * inv[None, :]
    cos2 = jnp.concatenate([jnp.cos(ang), jnp.cos(ang)], axis=-1)
    sin2 = jnp.concatenate([-jnp.sin(ang), jnp.sin(ang)], axis=-1)

    tri = jnp.tril(jnp.ones((GLA_CHUNK, GLA_CHUNK), F32)).astype(BF16)
    blk = jnp.arange(GLA_QK) // GLA_CHUNK
    pos = jnp.arange(GLA_QK) % GLA_CHUNK
    cbd = ((blk[:, None] == blk[None, :]) & (pos[:, None] >= pos[None, :])).astype(F32)
    ones = jnp.ones((GLA_CHUNK, LANE), BF16)

    w_in_p = jnp.pad(w_in, ((0, 0), (0, E_NP - w_in.shape[1]))).astype(BF16)
    w_lr_p = jnp.pad(w_lr, ((0, LANE - GLA_RANK), (0, 0))).astype(BF16)

    const2 = lambda shape: pl.BlockSpec(shape, lambda b, s: (0, 0), pipeline_mode=pl.Buffered(1))
    const3 = lambda shape: pl.BlockSpec(shape, lambda b, s: (0, 0, 0), pipeline_mode=pl.Buffered(1))
    return pl.pallas_call(
        _even_body,
        grid=(bsz // MIX_G, seq // ts),
        in_specs=[
            pl.BlockSpec((MIX_G, ts, D_MODEL), lambda b, s: (b, s, 0)),
            const2((1, D_MODEL)),
            const2((D_MODEL, E_NP)),
            const2((LANE, GLA_QK)),
            const2((1, GLA_QK)),
            pl.BlockSpec((ts, RET_DK), lambda b, s: (s, 0)),
            pl.BlockSpec((ts, RET_DK), lambda b, s: (s, 0)),
            const3(full), const3(full), const3(full), const3(full),
            const2((GLA_CHUNK, GLA_CHUNK)),
            const2((GLA_QK, GLA_QK)),
            const2((GLA_CHUNK, LANE)),
            const2((1, D_MODEL)),
            const2((D_MODEL, D_MODEL)),
        ],
        out_specs=pl.BlockSpec((MIX_G, ts, D_MODEL), lambda b, s: (b, s, 0)),
        out_shape=jax.ShapeDtypeStruct(h.shape, F32),
        scratch_shapes=[
            pltpu.VMEM((MIX_G, ts, E_NP), F32),
            pltpu.VMEM((MIX_G, ts, GLA_QK), F32),
            pltpu.VMEM((MIX_G, ts, D_MODEL), F32),
            pltpu.VMEM((MIX_G,) + full, F32),
            pltpu.VMEM((MIX_G, GLA_QK, GLA_DV), F32),
        ],
        compiler_params=pltpu.CompilerParams(
            dimension_semantics=("parallel", "arbitrary"), vmem_limit_bytes=VMEM_LIMIT),
        name="even_mixer",
    )(h, norm_w.reshape(1, -1), w_in_p, w_lr_p, b_lr.reshape(1, -1), cos2, sin2,
      dmat, w_inb, w_end, dec, tri, cbd, ones, head_norm.reshape(1, -1), w_out.astype(BF16))


def _causal_conv(buf, x, w_ref, b_ref, ts):
    buf[SUBLANE:SUBLANE + ts, :] = x
    acc = b_ref[...] + w_ref[CONV_W - 1:CONV_W, :] * x
    for j in range(CONV_W - 1):
        acc = acc + w_ref[j:j + 1, :] * buf[pl.ds(SUBLANE - (CONV_W - 1) + j, ts), :]
    buf[0:SUBLANE, :] = buf[ts:ts + SUBLANE, :]
    return acc


def _odd_body(x_ref, *rest):
    consts, o_ref, scratch = rest[:16], rest[16], rest[17:]
    lbuf, mbuf, lru_h, ml_c, ml_m = scratch[1], scratch[2], scratch[9], scratch[10], scratch[11]

    @pl.when(pl.program_id(1) == 0)
    def _():
        lbuf[:, 0:SUBLANE, :] = jnp.zeros((lbuf.shape[0], SUBLANE, LRU_W), F32)
        mbuf[:, 0:SUBLANE, :] = jnp.zeros((mbuf.shape[0], SUBLANE, GROUP_W), F32)
        lru_h[...] = jnp.zeros_like(lru_h)
        ml_c[...] = jnp.zeros_like(ml_c)
        ml_m[...] = jnp.zeros_like(ml_m)

    for g in range(x_ref.shape[0]):
        _odd_one(x_ref.at[g], *consts, o_ref.at[g], *[r.at[g] for r in scratch])


def _odd_one(x_ref, nw_ref, win_ref, lcw_ref, lcb_ref, mcw_ref, mcb_ref,
             wa_ref, ba_ref, wx_ref, bx_ref, lam_ref, wqk_ref, bif_ref, tri_ref,
             mnw_ref, wout_ref, o_ref,
             proj, lbuf, mbuf, abuf, ubuf, qk, ibuf, fbuf, obuf, lru_h, ml_c, ml_m):
    ts = x_ref.shape[0]
    x = x_ref[...]
    hn = (_rms(x) * nw_ref[...]).astype(BF16)
    for c0 in range(0, O_NP, 512):
        c1 = min(c0 + 512, O_NP)
        proj[:, c0:c1] = _dot(hn, win_ref[:, c0:c1])

    lxc = _causal_conv(lbuf, proj[:, O_LX:O_LX + LRU_W], lcw_ref, lcb_ref, ts)
    xb = lxc.astype(BF16)
    r = _sigmoid(_dot(xb, wa_ref[...]) + ba_ref[...])
    i = _sigmoid(_dot(xb, wx_ref[...]) + bx_ref[...])
    lam = lam_ref[...]
    softplus_neg_lam = jnp.maximum(-lam, 0.0) + jnp.log1p(jnp.exp(-jnp.abs(lam)))
    log_a = (-LRU_C) * r * softplus_neg_lam
    a = jnp.exp(log_a)
    u = jnp.sqrt(-jnp.tanh(log_a) * (a * a + 1.0)) * (i * lxc)
    rmod = lax.broadcasted_iota(jnp.int32, (ts, 1), 0) % SUBLANE
    d = 1
    while d < SUBLANE:
        keep = rmod >= d
        a_s = jnp.where(keep, pltpu.roll(a, d, 0), 1.0)
        u_s = jnp.where(keep, pltpu.roll(u, d, 0), 0.0)
        u = a * u_s + u
        a = a * a_s
        d *= 2
    abuf[...] = a
    ubuf[...] = u

    hprev = lru_h[0:1, :]
    for r0 in range(0, ts, SUBLANE):
        hg = ubuf[r0:r0 + SUBLANE, :] + abuf[r0:r0 + SUBLANE, :] * hprev
        ubuf[r0:r0 + SUBLANE, :] = hg
        hprev = hg[SUBLANE - 1:SUBLANE, :]
    lru_h[0:1, :] = hprev
    ly = proj[:, O_LY:O_LY + LRU_W]
    gelu = ly * (0.5 * (1.0 + jnp.tanh(np.float32(np.sqrt(2.0 / np.pi)) * (ly + 0.044715 * (ly * ly * ly)))))
    obuf[:, 0:LRU_W] = ubuf[...] * gelu

    mc = _silu(_causal_conv(mbuf, proj[:, O_MU:O_MU + GROUP_W], mcw_ref, mcb_ref, ts))
    for h in range(ML_HEADS):
        qkh = _dot(mc[:, h * ML_DH:(h + 1) * ML_DH].astype(BF16), wqk_ref[h])
        qk[:, h * ML_DH:(h + 1) * ML_DH] = qkh[:, :ML_DH]
        qk[:, GROUP_W + h * ML_DH:GROUP_W + (h + 1) * ML_DH] = qkh[:, ML_DH:] * (ML_DH ** -0.5)
    gates = proj[:, O_IF:O_IF + LANE] + bif_ref[...]
    ibuf[...] = gates
    fbuf[...] = pltpu.roll(_log_sigmoid(gates), LANE - ML_HEADS, 1)

    row_i = lax.broadcasted_iota(jnp.int32, (ML_CHUNK, ML_CHUNK), 0)
    col_i = lax.broadcasted_iota(jnp.int32, (ML_CHUNK, ML_CHUNK), 1)
    causal = row_i >= col_i
    ones_col = jnp.where(col_i == 0, 1.0, 0.0).astype(BF16)

    def ml_chunk(c, carry):
        rows = pl.ds(c * ML_CHUNK, ML_CHUNK)
        f_hi, f_lo = _split_bf16(fbuf[rows, :])
        tri = tri_ref[...]
        fc = _dot(tri, f_hi) + _dot(tri, f_lo)
        dm = ibuf[rows, :] - fc
        dm_t = dm.T
        b_end = fc[ML_CHUNK - 1:ML_CHUNK, :]
        a_col = b_end + dm
        a_max = jnp.max(a_col, axis=0, keepdims=True)
        w_end = jnp.exp(a_col - a_max)
        m_old = ml_m[0:1, :]
        m_new = jnp.maximum(b_end + m_old, a_max)
        g_old = jnp.exp(b_end + m_old - m_new)
        g_new = jnp.exp(a_max - m_new)
        for h in range(ML_HEADS):
            hs = slice(h * ML_DH, (h + 1) * ML_DH)
            qf = qk[rows, hs]
            kf = qk[rows, GROUP_W + h * ML_DH:GROUP_W + (h + 1) * ML_DH]
            q = qf.astype(BF16)
            v = proj[rows, O_MV + h * ML_DH:O_MV + (h + 1) * ML_DH].astype(BF16)
            d_row = dm_t[h:h + 1, :]
            m_h = m_old[:, h:h + 1]
            cm = jnp.max(jnp.where(causal, d_row, -jnp.inf), axis=1, keepdims=True)
            mx = jnp.maximum(cm, m_h)
            p = jnp.where(causal, jnp.exp(d_row - mx), 0.0)
            s = _dot_nt(q, kf.astype(BF16)) * p
            cx = ml_c[h]
            qc = _dot(q, cx.astype(BF16))
            w_inter = jnp.exp(m_h - mx)
            num = _dot(s.astype(BF16), v) + w_inter * qc[:, :ML_DH]
            den = jnp.sum(s, axis=1, keepdims=True) + w_inter * qc[:, ML_DH:ML_DH + 1]
            m_t = fc[:, h:h + 1] + mx
            hh = num / jnp.maximum(jnp.abs(den), jnp.exp(-m_t))
            og = _sigmoid(proj[rows, O_MO + h * ML_DH:O_MO + (h + 1) * ML_DH])
            obuf[rows, GROUP_W + h * ML_DH:GROUP_W + (h + 1) * ML_DH] = _rms(og * hh)
            wk = (w_end[:, h:h + 1] * kf).astype(BF16)
            updx = _dot_tn(wk, jnp.concatenate([v, ones_col], axis=1))
            ml_c[h] = g_old[:, h:h + 1] * cx + g_new[:, h:h + 1] * updx
        ml_m[0:1, :] = m_new
        return carry

    for c in range(ts // ML_CHUNK):
        ml_chunk(c, 0)

    obuf[:, GROUP_W:] = obuf[:, GROUP_W:] * mnw_ref[...]
    o_ref[...] = x + _dot(obuf[...].astype(BF16), wout_ref[...])


def _block_diag(w):
    g, n, _ = w.shape
    eye = jnp.eye(g, dtype=w.dtype)
    return (eye[:, None, :, None] * w[:, :, None, :]).reshape(g * n, g * n)


def _odd_mixer(h, norm_w, w_in, lcw, lcb, wa, ba, wx, bx, lam, mcw, mcb, wq, wk, bi, bf, ml_norm, w_out, ts):
    bsz, seq, _ = h.shape
    w_in_p = jnp.pad(w_in, ((0, 0), (0, O_NP - w_in.shape[1]))).astype(BF16)
    wqk = jnp.concatenate([wq, wk], axis=-1).astype(BF16)
    bif = jnp.pad(jnp.concatenate([bi, bf]), (0, LANE - 2 * ML_HEADS)).reshape(1, LANE)
    tri = jnp.tril(jnp.ones((ML_CHUNK, ML_CHUNK), F32)).astype(BF16)
    row = lambda t: t.reshape(1, -1)

    const2 = lambda shape: pl.BlockSpec(shape, lambda b, s: (0, 0), pipeline_mode=pl.Buffered(1))
    const3 = lambda shape: pl.BlockSpec(shape, lambda b, s: (0, 0, 0), pipeline_mode=pl.Buffered(1))
    return pl.pallas_call(
        _odd_body,
        grid=(bsz // MIX_G, seq // ts),
        in_specs=[
            pl.BlockSpec((MIX_G, ts, D_MODEL), lambda b, s: (b, s, 0)),
            const2((1, D_MODEL)),
            const2((D_MODEL, O_NP)),
            const2((CONV_W, LRU_W)), const2((1, LRU_W)),
            const2((CONV_W, GROUP_W)), const2((1, GROUP_W)),
            const2((LRU_W, LRU_W)), const2((1, LRU_W)),
            const2((LRU_W, LRU_W)), const2((1, LRU_W)),
            const2((1, LRU_W)),
            const3((ML_HEADS, ML_DH, 2 * ML_DH)),
            const2((1, LANE)),
            const2((ML_CHUNK, ML_CHUNK)),
            const2((1, GROUP_W)),
            const2((D_MODEL, D_MODEL)),
        ],
        out_specs=pl.BlockSpec((MIX_G, ts, D_MODEL), lambda b, s: (b, s, 0)),
        out_shape=jax.ShapeDtypeStruct(h.shape, F32),
        scratch_shapes=[
            pltpu.VMEM((MIX_G, ts, O_NP), F32),
            pltpu.VMEM((MIX_G, ts + SUBLANE, LRU_W), F32),
            pltpu.VMEM((MIX_G, ts + SUBLANE, GROUP_W), F32),
            pltpu.VMEM((MIX_G, ts, LRU_W), F32),
            pltpu.VMEM((MIX_G, ts, LRU_W), F32),
            pltpu.VMEM((MIX_G, ts, 2 * GROUP_W), F32),
            pltpu.VMEM((MIX_G, ts, LANE), F32),
            pltpu.VMEM((MIX_G, ts, LANE), F32),
            pltpu.VMEM((MIX_G, ts, D_MODEL), F32),
            pltpu.VMEM((MIX_G, SUBLANE, LRU_W), F32),
            pltpu.VMEM((MIX_G, ML_HEADS, ML_DH, 2 * ML_DH), F32),
            pltpu.VMEM((MIX_G, SUBLANE, LANE), F32),
        ],
        compiler_params=pltpu.CompilerParams(
            dimension_semantics=("parallel", "arbitrary"), vmem_limit_bytes=VMEM_LIMIT),
        name="odd_mixer",
    )(h, row(norm_w), w_in_p, lcw, row(lcb), mcw, row(mcb),
      _block_diag(wa).astype(BF16), row(ba), _block_diag(wx).astype(BF16), row(bx), row(lam),
      wqk, bif, tri, row(ml_norm), w_out.astype(BF16))


def kernel(x, ffn1_norm, ffn1_wgu, ffn1_wd, mix_norm, ffn2_norm, ffn2_wgu, ffn2_wd, e_w_in, e_w_lr_up, e_b_lr, e_head_norm, e_w_out, o_w_in, o_lru_conv_w, o_lru_conv_b, o_lru_wa, o_lru_ba, o_lru_wx, o_lru_bx, o_lru_lambda, o_ml_conv_w, o_ml_conv_b, o_ml_wq, o_ml_wk, o_ml_bi, o_ml_bf, o_ml_norm, o_w_out, final_norm):
    bsz, seq, d = x.shape
    depth = ffn1_norm.shape[0]
    h = x
    for layer in range(depth):
        j = layer // 2
        h = _ffn(h.reshape(bsz * seq, d), ffn1_norm[layer], ffn1_wgu[layer].astype(BF16),
                 ffn1_wd[layer].astype(BF16), final_norm, False).reshape(bsz, seq, d)
        if layer % 2 == 0:
            h = _even_mixer(h, mix_norm[layer], e_w_in[j], e_w_lr_up[j], e_b_lr[j], e_head_norm[j],
                            e_w_out[j], MIX_TS)
        else:
            h = _odd_mixer(h, mix_norm[layer], o_w_in[j], o_lru_conv_w[j], o_lru_conv_b[j],
                           o_lru_wa[j], o_lru_ba[j], o_lru_wx[j], o_lru_bx[j], o_lru_lambda[j],
                           o_ml_conv_w[j], o_ml_conv_b[j], o_ml_wq[j], o_ml_wk[j], o_ml_bi[j], o_ml_bf[j],
                           o_ml_norm[j], o_w_out[j], MIX_TS)
        h = _ffn(h.reshape(bsz * seq, d), ffn2_norm[layer], ffn2_wgu[layer].astype(BF16),
                 ffn2_wd[layer].astype(BF16), final_norm, layer == depth - 1).reshape(bsz, seq, d)
    return h
```

```python
import functools

import numpy as np
import jax
import jax.numpy as jnp
from jax import lax
from jax.experimental import pallas as pl
from jax.experimental.pallas import tpu as pltpu

F32 = jnp.float32
BF16 = jnp.bfloat16

D_MODEL = 1024
D_FF = 2816
GROUP_W = D_MODEL // 2
EPS = 1e-6

RET_HEADS = 4
RET_DK = 128
RET_CHUNK = 128
ROPE_BASE = 10000.0

GLA_HEADS = 4
GLA_DK = 64
GLA_DV = 128
GLA_RANK = 16
GLA_TAU = 16.0
GLA_CHUNK = 64
GLA_QK = GLA_HEADS * GLA_DK

LRU_W = GROUP_W
LRU_BLOCKS = 8
LRU_BS = LRU_W // LRU_BLOCKS
LRU_C = 8.0
CONV_W = 4

ML_HEADS = 4
ML_DH = 128
ML_CHUNK = 128

LANE = 128
SUBLANE = 8
VMEM_LIMIT = 56 * 1024 * 1024

E_RQ, E_RK, E_RV, E_RG = 0, 512, 1024, 1536
E_GQ, E_GK, E_GV, E_GG, E_LR = 2048, 2304, 2560, 3072, 3584
E_NP = 3712
O_LY, O_LX, O_MU, O_MV, O_MO, O_IF = 0, 512, 1024, 1536, 2048, 2560
O_NP = 2688

FFN_TM = 512
FFN_FC = 256
MIX_TS = 256
MIX_G = 2
EVEN_LAG = 0
ODD_LAG = 6
PAIR = 2 * LANE


def _dot(a, b):
    return jnp.dot(a, b, preferred_element_type=F32)


def _dot_nt(a, b):
    return lax.dot_general(a, b, (((1,), (1,)), ((), ())), preferred_element_type=F32)


def _dot_tn(a, b):
    return lax.dot_general(a, b, (((0,), (0,)), ((), ())), preferred_element_type=F32)


def _rms(x):
    return x * lax.rsqrt(jnp.mean(x * x, axis=-1, keepdims=True) + EPS)


def _sigmoid(x):
    return 1.0 / (1.0 + jnp.exp(-x))


def _silu(x):
    return x * _sigmoid(x)


def _log_sigmoid(x):
    return jnp.minimum(x, 0.0) - jnp.log1p(jnp.exp(-jnp.abs(x)))


def _split_bf16(x):
    hi = x.astype(BF16)
    lo = (x - hi.astype(F32)).astype(BF16)
    return hi, lo


def _bdiag2(a, b):
    z = jnp.zeros_like(a)
    return jnp.concatenate([jnp.concatenate([a, z], axis=1), jnp.concatenate([z, b], axis=1)], axis=0)


def _round_robin(gens, lag=0):
    gens = list(gens)
    done = [False] * len(gens)
    rnd = 0
    while not all(done):
        for i, gen in enumerate(gens):
            if done[i] or rnd < lag * i:
                continue
            try:
                next(gen)
            except StopIteration:
                done[i] = True
        rnd += 1
        yield


def _run(gen):
    for _ in gen:
        pass


def _ffn_body(final, x_ref, nw_ref, wgu_ref, wd_ref, fw_ref, o_ref, act_ref):
    x = x_ref[...]
    xn = (_rms(x) * nw_ref[...]).astype(BF16)
    for j in range(D_FF // FFN_FC):
        c0 = j * FFN_FC
        g = _dot(xn, wgu_ref[:, c0:c0 + FFN_FC])
        u = _dot(xn, wgu_ref[:, D_FF + c0:D_FF + c0 + FFN_FC])
        act_ref[:, c0:c0 + FFN_FC] = (_silu(g) * u).astype(BF16)
    h = x + 0.5 * _dot(act_ref[...], wd_ref[...])
    if final:
        h = _rms(h) * fw_ref[...]
    o_ref[...] = h


def _ffn(h, norm_w, wgu, wd, final_w, final):
    t = h.shape[0]
    const = lambda shape: pl.BlockSpec(shape, lambda i: (0, 0), pipeline_mode=pl.Buffered(1))
    return pl.pallas_call(
        functools.partial(_ffn_body, final),
        grid=(t // FFN_TM,),
        in_specs=[
            pl.BlockSpec((FFN_TM, D_MODEL), lambda i: (i, 0)),
            const((1, D_MODEL)),
            const((D_MODEL, 2 * D_FF)),
            const((D_FF, D_MODEL)),
            const((1, D_MODEL)),
        ],
        out_specs=pl.BlockSpec((FFN_TM, D_MODEL), lambda i: (i, 0)),
        out_shape=jax.ShapeDtypeStruct((t, D_MODEL), F32),
        scratch_shapes=[pltpu.VMEM((FFN_TM, D_FF), BF16)],
        compiler_params=pltpu.CompilerParams(
            dimension_semantics=("parallel",), vmem_limit_bytes=VMEM_LIMIT),
        name="ffn_final" if final else "ffn",
    )(h, norm_w.reshape(1, -1), wgu, wd, final_w.reshape(1, -1))


def _even_body(x_ref, nw_ref, win_ref, wlr_ref, blr_ref, cos_ref, sin_ref,
               dmat_ref, winb_ref, wendb_ref, dec_ref, tri_ref, ctile_ref,
               hnw_ref, wout_ref, o_ref,
               proj, labuf, obuf, ret_s, gla_st):
    groups = x_ref.shape[0]
    ts = x_ref.shape[1]

    @pl.when(pl.program_id(1) == 0)
    def _():
        ret_s[...] = jnp.zeros_like(ret_s)
        gla_st[...] = jnp.zeros_like(gla_st)

    def sequence(g):
        x = x_ref[g]
        pg = proj.at[g]
        hn = (_rms(x) * nw_ref[...]).astype(BF16)
        for c0 in range(0, E_NP, 512):
            c1 = min(c0 + 512, E_NP)
            pg[:, c0:c1] = _dot(hn, win_ref[:, c0:c1])
            yield
        glr = pg[:, E_LR:E_LR + LANE].astype(BF16)
        labuf[g] = _log_sigmoid(_dot(glr, wlr_ref[...]) + blr_ref[...]) * (1.0 / GLA_TAU)
        cos = cos_ref[...]
        sin = sin_ref[...]
        for c0 in range(0, 2 * RET_HEADS * RET_DK, RET_DK):
            xx = pg[:, c0:c0 + RET_DK]
            r = xx * cos + pltpu.roll(xx, RET_DK // 2, 1) * sin
            if c0 >= E_RK:
                r = r * (RET_DK ** -0.5)
            pg[:, c0:c0 + RET_DK] = r
        yield
        yield from _round_robin([
            _ret_chain(pg, obuf.at[g], ret_s.at[g], dmat_ref, winb_ref, wendb_ref, dec_ref, ts),
            _gla_chain(pg, labuf.at[g], obuf.at[g], gla_st.at[g], tri_ref, ctile_ref, ts)])
        o_ref[g] = x + _dot((obuf[g] * hnw_ref[...]).astype(BF16), wout_ref[...])
        yield

    _run(_round_robin([sequence(g) for g in range(groups)], lag=EVEN_LAG))


def _ret_chain(proj, obuf, ret_s, dmat_ref, winb_ref, wendb_ref, dec_ref, ts):
    for c in range(ts // RET_CHUNK):
        rows = pl.ds(c * RET_CHUNK, RET_CHUNK)
        stage = []
        for pr in range(RET_HEADS // 2):
            lo = pr * PAIR
            q2 = proj[rows, E_RQ + lo:E_RQ + lo + PAIR].astype(BF16)
            k2 = proj[rows, E_RK + lo:E_RK + lo + PAIR].astype(BF16)
            v2 = proj[rows, E_RV + lo:E_RV + lo + PAIR]
            s0 = ret_s[2 * pr]
            s1 = ret_s[2 * pr + 1]
            s_cat = _dot_nt(q2, _bdiag2(k2[:, :LANE], k2[:, LANE:]))
            inter = _dot(q2, _bdiag2(s0.astype(BF16), s1.astype(BF16)))
            full = _dot_tn(k2, (wendb_ref[pr] * v2).astype(BF16))
            ret_s[2 * pr] = dec_ref[2 * pr] * s0 + full[:LANE, :LANE]
            ret_s[2 * pr + 1] = dec_ref[2 * pr + 1] * s1 + full[LANE:, LANE:]
            stage.append((s_cat, inter, v2.astype(BF16)))
        yield
        outs = []
        for pr in range(RET_HEADS // 2):
            s_cat, inter, vb = stage[pr]
            p = (s_cat * dmat_ref[pr]).astype(BF16)
            outs.append(_dot(p, _bdiag2(vb[:, :LANE], vb[:, LANE:])) + inter * winb_ref[pr])
        yield
        for h in range(RET_HEADS):
            o = outs[h // 2][:, (h % 2) * LANE:(h % 2 + 1) * LANE]
            g = proj[rows, E_RG + h * RET_DK:E_RG + (h + 1) * RET_DK]
            obuf[rows, h * RET_DK:(h + 1) * RET_DK] = _rms(o) * _silu(g)
        yield


def _gla_chain(proj, labuf, obuf, gla_st, tri_ref, ctile_ref, ts):
    lane_head = lax.broadcasted_iota(jnp.int32, (1, GLA_QK), 1) // GLA_DK

    def stack_heads(z):
        return jnp.concatenate(
            [jnp.where(lane_head == h, z, 0.0) for h in range(GLA_HEADS)], axis=0).astype(BF16)

    zb = jnp.zeros((GLA_CHUNK, GLA_DV), BF16)
    for c in range(ts // GLA_CHUNK):
        rows = pl.ds(c * GLA_CHUNK, GLA_CHUNK)
        la_hi, la_lo = _split_bf16(labuf[rows, :])
        tri = tri_ref[...]
        b = _dot(tri, la_hi) + _dot(tri, la_lo)
        yield
        b_mid = b[GLA_CHUNK // 2 - 1:GLA_CHUNK // 2, :]
        b_end = b[GLA_CHUNK - 1:GLA_CHUNK, :]
        q = proj[rows, E_GQ:E_GQ + GLA_QK]
        k = proj[rows, E_GK:E_GK + GLA_QK] * (GLA_DK ** -0.5)
        vb = proj[rows, E_GV:E_GV + GLA_HEADS * GLA_DV].astype(BF16)
        vh = [vb[:, h * GLA_DV:(h + 1) * GLA_DV] for h in range(GLA_HEADS)]
        s_cat = _dot_nt((q * jnp.exp(b - b_mid)).astype(BF16), stack_heads(k * jnp.exp(b_mid - b)))
        st = gla_st[...]
        inter = _dot_nt(stack_heads(q * jnp.exp(b)), st.astype(BF16))
        upd = _dot_tn(jnp.concatenate(vh, axis=0), stack_heads(k * jnp.exp(b_end - b)))
        gla_st[...] = jnp.exp(b_end) * st + upd
        yield
        p = (s_cat * ctile_ref[...]).astype(BF16)
        vbd = jnp.concatenate(
            [jnp.concatenate([vh[h] if j == h else zb for j in range(GLA_HEADS)], axis=1)
             for h in range(GLA_HEADS)], axis=0)
        out_cat = _dot(p, vbd)
        yield
        for h in range(GLA_HEADS):
            g = proj[rows, E_GG + h * GLA_DV:E_GG + (h + 1) * GLA_DV]
            o = out_cat[:, h * GLA_DV:(h + 1) * GLA_DV] + inter[h * GLA_CHUNK:(h + 1) * GLA_CHUNK, :]
            obuf[rows, GROUP_W + h * GLA_DV:GROUP_W + (h + 1) * GLA_DV] = _rms(o) * _silu(g)
        yield


def _pair_lanes(t):
    return jnp.concatenate([t[0::2], t[1::2]], axis=-1)


def _even_mixer(h, norm_w, w_in, w_lr, b_lr, head_norm, w_out, ts):
    bsz, seq, _ = h.shape
    hh = jnp.arange(RET_HEADS, dtype=F32)
    log_g = jnp.log1p(-jnp.exp2(-5.0 - hh))
    idx = jnp.arange(RET_CHUNK, dtype=F32)
    rel = idx[:, None] - idx[None, :]
    causal = rel >= 0
    dmat = jnp.where(causal, jnp.exp(log_g[:, None, None] * jnp.where(causal, rel, 0.0)), 0.0)
    full = (RET_HEADS, RET_CHUNK, RET_DK)
    pair = (RET_HEADS // 2, RET_CHUNK, PAIR)
    w_end = jnp.broadcast_to(jnp.exp(log_g[:, None] * (RET_CHUNK - 1.0 - idx)[None, :])[:, :, None], full)
    w_inb = jnp.broadcast_to(jnp.exp(log_g[:, None] * (idx + 1.0)[None, :])[:, :, None], full)
    dec = jnp.broadcast_to(jnp.exp(log_g * RET_CHUNK)[:, None, None], full)

    half = RET_DK // 2
    inv = ROPE_BASE ** (-jnp.arange(half, dtype=F32) / half)
    ang = jnp.arange(seq).astype(F32)[:, None] * inv[None, :]
    cos2 = jnp.concatenate([jnp.cos(ang), jnp.cos(ang)], axis=-1)
    sin2 = jnp.concatenate([-jnp.sin(ang), jnp.sin(ang)], axis=-1)

    tril = jnp.tril(jnp.ones((GLA_CHUNK, GLA_CHUNK), F32))
    ctile = jnp.tile(tril, (1, GLA_HEADS))

    w_in_p = jnp.pad(w_in, ((0, 0), (0, E_NP - w_in.shape[1]))).astype(BF16)
    w_lr_p = jnp.pad(w_lr, ((0, LANE - GLA_RANK), (0, 0))).astype(BF16)

    const2 = lambda shape: pl.BlockSpec(shape, lambda b, s: (0, 0), pipeline_mode=pl.Buffered(1))
    const3 = lambda shape: pl.BlockSpec(shape, lambda b, s: (0, 0, 0), pipeline_mode=pl.Buffered(1))
    return pl.pallas_call(
        _even_body,
        grid=(bsz // MIX_G, seq // ts),
        in_specs=[
            pl.BlockSpec((MIX_G, ts, D_MODEL), lambda b, s: (b, s, 0)),
            const2((1, D_MODEL)),
            const2((D_MODEL, E_NP)),
            const2((LANE, GLA_QK)),
            const2((1, GLA_QK)),
            pl.BlockSpec((ts, RET_DK), lambda b, s: (s, 0)),
            pl.BlockSpec((ts, RET_DK), lambda b, s: (s, 0)),
            const3(pair), const3(pair), const3(pair), const3(full),
            const2((GLA_CHUNK, GLA_CHUNK)),
            const2((GLA_CHUNK, GLA_QK)),
            const2((1, D_MODEL)),
            const2((D_MODEL, D_MODEL)),
        ],
        out_specs=pl.BlockSpec((MIX_G, ts, D_MODEL), lambda b, s: (b, s, 0)),
        out_shape=jax.ShapeDtypeStruct(h.shape, F32),
        scratch_shapes=[
            pltpu.VMEM((MIX_G, ts, E_NP), F32),
            pltpu.VMEM((MIX_G, ts, GLA_QK), F32),
            pltpu.VMEM((MIX_G, ts, D_MODEL), F32),
            pltpu.VMEM((MIX_G,) + full, F32),
            pltpu.VMEM((MIX_G, GLA_DV, GLA_QK), F32),
        ],
        compiler_params=pltpu.CompilerParams(
            dimension_semantics=("parallel", "arbitrary"), vmem_limit_bytes=VMEM_LIMIT),
        name="even_mixer",
    )(h, norm_w.reshape(1, -1), w_in_p, w_lr_p, b_lr.reshape(1, -1), cos2, sin2,
      _pair_lanes(dmat), _pair_lanes(w_inb), _pair_lanes(w_end), dec, tril.astype(BF16), ctile,
      head_norm.reshape(1, -1), w_out.astype(BF16))


def _causal_conv(buf, x, w_ref, b_ref, ts):
    buf[SUBLANE:SUBLANE + ts, :] = x
    acc = b_ref[...] + w_ref[CONV_W - 1:CONV_W, :] * x
    for j in range(CONV_W - 1):
        acc = acc + w_ref[j:j + 1, :] * buf[pl.ds(SUBLANE - (CONV_W - 1) + j, ts), :]
    buf[0:SUBLANE, :] = buf[ts:ts + SUBLANE, :]
    return acc


def _odd_body(x_ref, nw_ref, win_ref, lcw_ref, lcb_ref, mcw_ref, mcb_ref,
              wa_ref, ba_ref, wx_ref, bx_ref, lam_ref, wqk_ref, bif_ref, tri_ref,
              mnw_ref, wout_ref, o_ref,
              proj, lbuf, mbuf, abuf, ubuf, qk, ibuf, fbuf, obuf, lru_h, ml_c, ml_m):
    groups = x_ref.shape[0]
    ts = x_ref.shape[1]

    @pl.when(pl.program_id(1) == 0)
    def _():
        lbuf[:, 0:SUBLANE, :] = jnp.zeros((groups, SUBLANE, LRU_W), F32)
        mbuf[:, 0:SUBLANE, :] = jnp.zeros((groups, SUBLANE, GROUP_W), F32)
        lru_h[...] = jnp.zeros_like(lru_h)
        ml_c[...] = jnp.zeros_like(ml_c)
        ml_m[...] = jnp.zeros_like(ml_m)

    def sequence(g):
        x = x_ref[g]
        hn = (_rms(x) * nw_ref[...]).astype(BF16)
        for c0 in range(0, O_NP, 512):
            c1 = min(c0 + 512, O_NP)
            proj[g, :, c0:c1] = _dot(hn, win_ref[:, c0:c1])
            yield
        yield from _round_robin([
            _lru_chain(proj.at[g], lbuf.at[g], abuf.at[g], ubuf.at[g], obuf.at[g], lru_h.at[g],
                       lcw_ref, lcb_ref, wa_ref, ba_ref, wx_ref, bx_ref, lam_ref, ts),
            _mlstm_chain(proj.at[g], mbuf.at[g], qk.at[g], ibuf.at[g], fbuf.at[g], obuf.at[g],
                         ml_c.at[g], ml_m.at[g], mcw_ref, mcb_ref, wqk_ref, bif_ref, tri_ref, mnw_ref, ts)])
        o_ref[g] = x + _dot(obuf[g].astype(BF16), wout_ref[...])
        yield

    _run(_round_robin([sequence(g) for g in range(groups)], lag=ODD_LAG))


def _lru_chain(proj, lbuf, abuf, ubuf, obuf, lru_h, lcw_ref, lcb_ref, wa_ref, ba_ref, wx_ref, bx_ref, lam_ref, ts):
    lxc = _causal_conv(lbuf, proj[:, O_LX:O_LX + LRU_W], lcw_ref, lcb_ref, ts)
    xb = lxc.astype(BF16)
    r = _sigmoid(_dot(xb, wa_ref[...]) + ba_ref[...])
    i = _sigmoid(_dot(xb, wx_ref[...]) + bx_ref[...])
    yield
    lam = lam_ref[...]
    softplus_neg_lam = jnp.maximum(-lam, 0.0) + jnp.log1p(jnp.exp(-jnp.abs(lam)))
    log_a = (-LRU_C) * r * softplus_neg_lam
    a = jnp.exp(log_a)
    u = jnp.sqrt(-jnp.tanh(log_a) * (a * a + 1.0)) * (i * lxc)
    rmod = lax.broadcasted_iota(jnp.int32, (ts, 1), 0) % SUBLANE
    d = 1
    while d < SUBLANE:
        keep = rmod >= d
        a_s = jnp.where(keep, pltpu.roll(a, d, 0), 1.0)
        u_s = jnp.where(keep, pltpu.roll(u, d, 0), 0.0)
        u = a * u_s + u
        a = a * a_s
        d *= 2
    abuf[...] = a
    ubuf[...] = u
    hprev = lru_h[0:1, :]
    for r0 in range(0, ts, SUBLANE):
        hg = ubuf[r0:r0 + SUBLANE, :] + abuf[r0:r0 + SUBLANE, :] * hprev
        ubuf[r0:r0 + SUBLANE, :] = hg
        hprev = hg[SUBLANE - 1:SUBLANE, :]
    lru_h[0:1, :] = hprev
    ly = proj[:, O_LY:O_LY + LRU_W]
    gelu = ly * (0.5 * (1.0 + jnp.tanh(np.float32(np.sqrt(2.0 / np.pi)) * (ly + 0.044715 * (ly * ly * ly)))))
    obuf[:, 0:LRU_W] = ubuf[...] * gelu
    yield


def _mlstm_chain(proj, mbuf, qk, ibuf, fbuf, obuf, ml_c, ml_m, mcw_ref, mcb_ref, wqk_ref, bif_ref, tri_ref,
                 mnw_ref, ts):
    mc = _silu(_causal_conv(mbuf, proj[:, O_MU:O_MU + GROUP_W], mcw_ref, mcb_ref, ts))
    for h in range(ML_HEADS):
        qkh = _dot(mc[:, h * ML_DH:(h + 1) * ML_DH].astype(BF16), wqk_ref[h])
        qk[:, h * ML_DH:(h + 1) * ML_DH] = qkh[:, :ML_DH]
        qk[:, GROUP_W + h * ML_DH:GROUP_W + (h + 1) * ML_DH] = qkh[:, ML_DH:] * (ML_DH ** -0.5)
    gates = proj[:, O_IF:O_IF + LANE] + bif_ref[...]
    ibuf[...] = gates
    fbuf[...] = pltpu.roll(_log_sigmoid(gates), LANE - ML_HEADS, 1)
    yield

    row_i = lax.broadcasted_iota(jnp.int32, (ML_CHUNK, ML_CHUNK), 0)
    col_i = lax.broadcasted_iota(jnp.int32, (ML_CHUNK, ML_CHUNK), 1)
    causal = row_i >= col_i
    ones_col = jnp.where(col_i == 0, 1.0, 0.0).astype(BF16)

    for c in range(ts // ML_CHUNK):
        rows = pl.ds(c * ML_CHUNK, ML_CHUNK)
        f_hi, f_lo = _split_bf16(fbuf[rows, :])
        tri = tri_ref[...]
        fc = _dot(tri, f_hi) + _dot(tri, f_lo)
        yield
        dm = ibuf[rows, :] - fc
        dm_t = dm.T
        b_end = fc[ML_CHUNK - 1:ML_CHUNK, :]
        a_col = b_end + dm
        a_max = jnp.max(a_col, axis=0, keepdims=True)
        w_end = jnp.exp(a_col - a_max)
        m_old = ml_m[0:1, :]
        m_new = jnp.maximum(b_end + m_old, a_max)
        g_old = jnp.exp(b_end + m_old - m_new)
        g_new = jnp.exp(a_max - m_new)
        ml_m[0:1, :] = m_new
        s_cat, qc, vh = [], [], []
        for pr in range(ML_HEADS // 2):
            lo = pr * PAIR
            q2 = qk[rows, lo:lo + PAIR].astype(BF16)
            k2f = qk[rows, GROUP_W + lo:GROUP_W + lo + PAIR]
            k2 = k2f.astype(BF16)
            s_cat.append(_dot_nt(q2, _bdiag2(k2[:, :LANE], k2[:, LANE:])))
            for j in range(2):
                h = 2 * pr + j
                v = proj[rows, O_MV + h * ML_DH:O_MV + (h + 1) * ML_DH].astype(BF16)
                cx = ml_c[h]
                qc.append(_dot(q2[:, j * LANE:(j + 1) * LANE], cx.astype(BF16)))
                wk = (w_end[:, h:h + 1] * k2f[:, j * LANE:(j + 1) * LANE]).astype(BF16)
                updx = _dot_tn(wk, jnp.concatenate([v, ones_col], axis=1))
                ml_c[h] = g_old[:, h:h + 1] * cx + g_new[:, h:h + 1] * updx
                vh.append(v)
        yield
        mxs, sms, dens, nums = [], [], [], []
        for h in range(ML_HEADS):
            d_row = dm_t[h:h + 1, :]
            m_h = m_old[:, h:h + 1]
            cm = jnp.max(jnp.where(causal, d_row, -jnp.inf), axis=1, keepdims=True)
            mx = jnp.maximum(cm, m_h)
            p = jnp.where(causal, jnp.exp(d_row - mx), 0.0)
            s = s_cat[h // 2][:, (h % 2) * LANE:(h % 2 + 1) * LANE] * p
            mxs.append(mx)
            sms.append(s.astype(BF16))
            dens.append(jnp.sum(s, axis=1, keepdims=True))
        for pr in range(ML_HEADS // 2):
            nums.append(_dot(jnp.concatenate([sms[2 * pr], sms[2 * pr + 1]], axis=1),
                             _bdiag2(vh[2 * pr], vh[2 * pr + 1])))
        yield
        for h in range(ML_HEADS):
            m_h = m_old[:, h:h + 1]
            w_inter = jnp.exp(m_h - mxs[h])
            num = nums[h // 2][:, (h % 2) * LANE:(h % 2 + 1) * LANE] + w_inter * qc[h][:, :ML_DH]
            den = dens[h] + w_inter * qc[h][:, ML_DH:ML_DH + 1]
            m_t = fc[:, h:h + 1] + mxs[h]
            hh = num / jnp.maximum(jnp.abs(den), jnp.exp(-m_t))
            og = _sigmoid(proj[rows, O_MO + h * ML_DH:O_MO + (h + 1) * ML_DH])
            hs = slice(GROUP_W + h * ML_DH, GROUP_W + (h + 1) * ML_DH)
            obuf[rows, hs] = _rms(og * hh) * mnw_ref[:, h * ML_DH:(h + 1) * ML_DH]
        yield


def _block_diag(w):
    g, n, _ = w.shape
    eye = jnp.eye(g, dtype=w.dtype)
    return (eye[:, None, :, None] * w[:, :, None, :]).reshape(g * n, g * n)


def _odd_mixer(h, norm_w, w_in, lcw, lcb, wa, ba, wx, bx, lam, mcw, mcb, wq, wk, bi, bf, ml_norm, w_out, ts):
    bsz, seq, _ = h.shape
    w_in_p = jnp.pad(w_in, ((0, 0), (0, O_NP - w_in.shape[1]))).astype(BF16)
    wqk = jnp.concatenate([wq, wk], axis=-1).astype(BF16)
    bif = jnp.pad(jnp.concatenate([bi, bf]), (0, LANE - 2 * ML_HEADS)).reshape(1, LANE)
    tri = jnp.tril(jnp.ones((ML_CHUNK, ML_CHUNK), F32)).astype(BF16)
    row = lambda t: t.reshape(1, -1)

    const2 = lambda shape: pl.BlockSpec(shape, lambda b, s: (0, 0), pipeline_mode=pl.Buffered(1))
    const3 = lambda shape: pl.BlockSpec(shape, lambda b, s: (0, 0, 0), pipeline_mode=pl.Buffered(1))
    return pl.pallas_call(
        _odd_body,
        grid=(bsz // MIX_G, seq // ts),
        in_specs=[
            pl.BlockSpec((MIX_G, ts, D_MODEL), lambda b, s: (b, s, 0)),
            const2((1, D_MODEL)),
            const2((D_MODEL, O_NP)),
            const2((CONV_W, LRU_W)), const2((1, LRU_W)),
            const2((CONV_W, GROUP_W)), const2((1, GROUP_W)),
            const2((LRU_W, LRU_W)), const2((1, LRU_W)),
            const2((LRU_W, LRU_W)), const2((1, LRU_W)),
            const2((1, LRU_W)),
            const3((ML_HEADS, ML_DH, 2 * ML_DH)),
            const2((1, LANE)),
            const2((ML_CHUNK, ML_CHUNK)),
            const2((1, GROUP_W)),
            const2((D_MODEL, D_MODEL)),
        ],
        out_specs=pl.BlockSpec((MIX_G, ts, D_MODEL), lambda b, s: (b, s, 0)),
        out_shape=jax.ShapeDtypeStruct(h.shape, F32),
        scratch_shapes=[
            pltpu.VMEM((MIX_G, ts, O_NP), F32),
            pltpu.VMEM((MIX_G, ts + SUBLANE, LRU_W), F32),
            pltpu.VMEM((MIX_G, ts + SUBLANE, GROUP_W), F32),
            pltpu.VMEM((MIX_G, ts, LRU_W), F32),
            pltpu.VMEM((MIX_G, ts, LRU_W), F32),
            pltpu.VMEM((MIX_G, ts, 2 * GROUP_W), F32),
            pltpu.VMEM((MIX_G, ts, LANE), F32),
            pltpu.VMEM((MIX_G, ts, LANE), F32),
            pltpu.VMEM((MIX_G, ts, D_MODEL), F32),
            pltpu.VMEM((MIX_G, SUBLANE, LRU_W), F32),
            pltpu.VMEM((MIX_G, ML_HEADS, ML_DH, 2 * ML_DH), F32),
            pltpu.VMEM((MIX_G, SUBLANE, LANE), F32),
        ],
        compiler_params=pltpu.CompilerParams(
            dimension_semantics=("parallel", "arbitrary"), vmem_limit_bytes=VMEM_LIMIT),
        name="odd_mixer",
    )(h, row(norm_w), w_in_p, lcw, row(lcb), mcw, row(mcb),
      _block_diag(wa).astype(BF16), row(ba), _block_diag(wx).astype(BF16), row(bx), row(lam),
      wqk, bif, tri, row(ml_norm), w_out.astype(BF16))


def kernel(x, ffn1_norm, ffn1_wgu, ffn1_wd, mix_norm, ffn2_norm, ffn2_wgu, ffn2_wd, e_w_in, e_w_lr_up, e_b_lr, e_head_norm, e_w_out, o_w_in, o_lru_conv_w, o_lru_conv_b, o_lru_wa, o_lru_ba, o_lru_wx, o_lru_bx, o_lru_lambda, o_ml_conv_w, o_ml_conv_b, o_ml_wq, o_ml_wk, o_ml_bi, o_ml_bf, o_ml_norm, o_w_out, final_norm):
    bsz, seq, d = x.shape
    depth = ffn1_norm.shape[0]
    h = x
    for layer in range(depth):
        j = layer // 2
        h = _ffn(h.reshape(bsz * seq, d), ffn1_norm[layer], ffn1_wgu[layer].astype(BF16),
                 ffn1_wd[layer].astype(BF16), final_norm, False).reshape(bsz, seq, d)
        if layer % 2 == 0:
            h = _even_mixer(h, mix_norm[layer], e_w_in[j], e_w_lr_up[j], e_b_lr[j], e_head_norm[j],
                            e_w_out[j], MIX_TS)
        else:
            h = _odd_mixer(h, mix_norm[layer], o_w_in[j], o_lru_conv_w[j], o_lru_conv_b[j],
                           o_lru_wa[j], o_lru_ba[j], o_lru_wx[j], o_lru_bx[j], o_lru_lambda[j],
                           o_ml_conv_w[j], o_ml_conv_b[j], o_ml_wq[j], o_ml_wk[j], o_ml_bi[j], o_ml_bf[j],
                           o_ml_norm[j], o_w_out[j], MIX_TS)
        h = _ffn(h.reshape(bsz * seq, d), ffn2_norm[layer], ffn2_wgu[layer].astype(BF16),
                 ffn2_wd[layer].astype(BF16), final_norm, layer == depth - 1).reshape(bsz, seq, d)
    return h
```

```python
import functools

import numpy as np
import jax
import jax.numpy as jnp
from jax import lax
from jax.experimental import pallas as pl
from jax.experimental.pallas import tpu as pltpu

F32 = jnp.float32
BF16 = jnp.bfloat16

D_MODEL = 1024
D_FF = 2816
GROUP_W = D_MODEL // 2
EPS = 1e-6

RET_HEADS = 4
RET_DK = 128
RET_CHUNK = 128
ROPE_BASE = 10000.0

GLA_HEADS = 4
GLA_DK = 64
GLA_DV = 128
GLA_RANK = 16
GLA_TAU = 16.0
GLA_CHUNK = 64
GLA_QK = GLA_HEADS * GLA_DK

LRU_W = GROUP_W
LRU_BLOCKS = 8
LRU_BS = LRU_W // LRU_BLOCKS
LRU_C = 8.0
CONV_W = 4

ML_HEADS = 4
ML_DH = 128
ML_CHUNK = 128

LANE = 128
SUBLANE = 8
VMEM_LIMIT = 56 * 1024 * 1024

E_RQ, E_RK, E_RV, E_RG = 0, 512, 1024, 1536
E_GQ, E_GK, E_GV, E_GG, E_LR = 2048, 2304, 2560, 3072, 3584
E_NP = 3712
O_LY, O_LX, O_MU, O_MV, O_MO, O_IF = 0, 512, 1024, 1536, 2048, 2560
O_NP = 2688

FFN_TM = 1024
FFN_SUB = 256
FFN_FC = 256
FFN_WCOLS = 512
FFN_WROWS = 256
MIX_TS = 256
MIX_G = 2
EVEN_LAG = 0
ODD_LAG = 6
PAIR = 2 * LANE


def _dot(a, b):
    return jnp.dot(a, b, preferred_element_type=F32)


def _dot_nt(a, b):
    return lax.dot_general(a, b, (((1,), (1,)), ((), ())), preferred_element_type=F32)


def _dot_tn(a, b):
    return lax.dot_general(a, b, (((0,), (0,)), ((), ())), preferred_element_type=F32)


def _rms(x):
    return x * lax.rsqrt(jnp.mean(x * x, axis=-1, keepdims=True) + EPS)


def _sigmoid(x):
    return 1.0 / (1.0 + jnp.exp(-x))


def _silu(x):
    return x * _sigmoid(x)


def _log_sigmoid(x):
    return jnp.minimum(x, 0.0) - jnp.log1p(jnp.exp(-jnp.abs(x)))


def _split_bf16(x):
    hi = x.astype(BF16)
    lo = (x - hi.astype(F32)).astype(BF16)
    return hi, lo


def _bdiag2(a, b):
    z = jnp.zeros_like(a)
    return jnp.concatenate([jnp.concatenate([a, z], axis=1), jnp.concatenate([z, b], axis=1)], axis=0)


def _round_robin(gens, lag=0):
    gens = list(gens)
    done = [False] * len(gens)
    rnd = 0
    while not all(done):
        for i, gen in enumerate(gens):
            if done[i] or rnd < lag * i:
                continue
            try:
                next(gen)
            except StopIteration:
                done[i] = True
        rnd += 1
        yield


def _run(gen):
    for _ in gen:
        pass


def _stage_weight(layer, w_hbm, w_bf, stage, sem, axis, width):
    n = w_bf.shape[axis] // width

    def window(j):
        sl = pl.ds(j * width, width)
        return (sl, slice(None)) if axis == 0 else (slice(None), sl)

    def copy(j):
        return pltpu.make_async_copy(w_hbm.at[(layer,) + window(j)], stage.at[j % 2], sem.at[j % 2])

    copy(0).start()
    for j in range(n):
        if j + 1 < n:
            copy(j + 1).start()
        copy(j).wait()
        w_bf[window(j)] = stage[j % 2].astype(BF16)


def _ffn_body(final, layer, x_ref, nw_ref, wgu_hbm, wd_hbm, fw_ref, o_ref,
              act_ref, wgu_ref, wd_ref, stage_gu, stage_d, sem_gu, sem_d):
    @pl.when(pl.program_id(0) == 0)
    def _():
        _stage_weight(layer, wgu_hbm, wgu_ref, stage_gu, sem_gu, 1, FFN_WCOLS)
        _stage_weight(layer, wd_hbm, wd_ref, stage_d, sem_d, 0, FFN_WROWS)

    for r0 in range(0, FFN_TM, FFN_SUB):
        rows = pl.ds(r0, FFN_SUB)
        x = x_ref[rows, :]
        xn = (_rms(x) * nw_ref[...]).astype(BF16)
        for j in range(D_FF // FFN_FC):
            c0 = j * FFN_FC
            g = _dot(xn, wgu_ref[:, c0:c0 + FFN_FC])
            u = _dot(xn, wgu_ref[:, D_FF + c0:D_FF + c0 + FFN_FC])
            act_ref[rows, c0:c0 + FFN_FC] = (_silu(g) * u).astype(BF16)
        h = x + 0.5 * _dot(act_ref[rows, :], wd_ref[...])
        if final:
            h = _rms(h) * fw_ref[...]
        o_ref[rows, :] = h


def _ffn(h, norm_w, wgu_all, wd_all, layer, final_w, final):
    t = h.shape[0]
    const = lambda shape: pl.BlockSpec(shape, lambda i: (0, 0), pipeline_mode=pl.Buffered(1))
    return pl.pallas_call(
        functools.partial(_ffn_body, final, layer),
        grid=(t // FFN_TM,),
        in_specs=[
            pl.BlockSpec((FFN_TM, D_MODEL), lambda i: (i, 0)),
            const((1, D_MODEL)),
            pl.BlockSpec(memory_space=pl.ANY),
            pl.BlockSpec(memory_space=pl.ANY),
            const((1, D_MODEL)),
        ],
        out_specs=pl.BlockSpec((FFN_TM, D_MODEL), lambda i: (i, 0)),
        out_shape=jax.ShapeDtypeStruct((t, D_MODEL), F32),
        scratch_shapes=[
            pltpu.VMEM((FFN_TM, D_FF), BF16),
            pltpu.VMEM((D_MODEL, 2 * D_FF), BF16),
            pltpu.VMEM((D_FF, D_MODEL), BF16),
            pltpu.VMEM((2, D_MODEL, FFN_WCOLS), F32),
            pltpu.VMEM((2, FFN_WROWS, D_MODEL), F32),
            pltpu.SemaphoreType.DMA((2,)),
            pltpu.SemaphoreType.DMA((2,)),
        ],
        compiler_params=pltpu.CompilerParams(
            dimension_semantics=("arbitrary",), vmem_limit_bytes=VMEM_LIMIT),
        name="ffn_final" if final else "ffn",
    )(h, norm_w.reshape(1, -1), wgu_all, wd_all, final_w.reshape(1, -1))


def _even_body(x_ref, nw_ref, win_ref, wlr_ref, blr_ref, cos_ref, sin_ref,
               dmat_ref, winb_ref, wendb_ref, dec_ref, tri_ref, ctile_ref,
               hnw_ref, wout_ref, o_ref,
               proj, labuf, obuf, ret_s, gla_st):
    groups = x_ref.shape[0]
    ts = x_ref.shape[1]

    @pl.when(pl.program_id(1) == 0)
    def _():
        ret_s[...] = jnp.zeros_like(ret_s)
        gla_st[...] = jnp.zeros_like(gla_st)

    def sequence(g):
        x = x_ref[g]
        pg = proj.at[g]
        hn = (_rms(x) * nw_ref[...]).astype(BF16)
        for c0 in range(0, E_NP, 512):
            c1 = min(c0 + 512, E_NP)
            pg[:, c0:c1] = _dot(hn, win_ref[:, c0:c1])
            yield
        glr = pg[:, E_LR:E_LR + LANE].astype(BF16)
        labuf[g] = _log_sigmoid(_dot(glr, wlr_ref[...]) + blr_ref[...]) * (1.0 / GLA_TAU)
        cos = cos_ref[...]
        sin = sin_ref[...]
        for c0 in range(0, 2 * RET_HEADS * RET_DK, RET_DK):
            xx = pg[:, c0:c0 + RET_DK]
            r = xx * cos + pltpu.roll(xx, RET_DK // 2, 1) * sin
            if c0 >= E_RK:
                r = r * (RET_DK ** -0.5)
            pg[:, c0:c0 + RET_DK] = r
        yield
        yield from _round_robin([
            _ret_chain(pg, obuf.at[g], ret_s.at[g], dmat_ref, winb_ref, wendb_ref, dec_ref, ts),
            _gla_chain(pg, labuf.at[g], obuf.at[g], gla_st.at[g], tri_ref, ctile_ref, ts)])
        o_ref[g] = x + _dot((obuf[g] * hnw_ref[...]).astype(BF16), wout_ref[...])
        yield

    _run(_round_robin([sequence(g) for g in range(groups)], lag=EVEN_LAG))


def _ret_chain(proj, obuf, ret_s, dmat_ref, winb_ref, wendb_ref, dec_ref, ts):
    for c in range(ts // RET_CHUNK):
        rows = pl.ds(c * RET_CHUNK, RET_CHUNK)
        stage = []
        for pr in range(RET_HEADS // 2):
            lo = pr * PAIR
            q2 = proj[rows, E_RQ + lo:E_RQ + lo + PAIR].astype(BF16)
            k2 = proj[rows, E_RK + lo:E_RK + lo + PAIR].astype(BF16)
            v2 = proj[rows, E_RV + lo:E_RV + lo + PAIR]
            s0 = ret_s[2 * pr]
            s1 = ret_s[2 * pr + 1]
            s_cat = _dot_nt(q2, _bdiag2(k2[:, :LANE], k2[:, LANE:]))
            inter = _dot(q2, _bdiag2(s0.astype(BF16), s1.astype(BF16)))
            full = _dot_tn(k2, (wendb_ref[pr] * v2).astype(BF16))
            ret_s[2 * pr] = dec_ref[2 * pr] * s0 + full[:LANE, :LANE]
            ret_s[2 * pr + 1] = dec_ref[2 * pr + 1] * s1 + full[LANE:, LANE:]
            stage.append((s_cat, inter, v2.astype(BF16)))
        yield
        outs = []
        for pr in range(RET_HEADS // 2):
            s_cat, inter, vb = stage[pr]
            p = (s_cat * dmat_ref[pr]).astype(BF16)
            outs.append(_dot(p, _bdiag2(vb[:, :LANE], vb[:, LANE:])) + inter * winb_ref[pr])
        yield
        for h in range(RET_HEADS):
            o = outs[h // 2][:, (h % 2) * LANE:(h % 2 + 1) * LANE]
            g = proj[rows, E_RG + h * RET_DK:E_RG + (h + 1) * RET_DK]
            obuf[rows, h * RET_DK:(h + 1) * RET_DK] = _rms(o) * _silu(g)
        yield


def _gla_chain(proj, labuf, obuf, gla_st, tri_ref, ctile_ref, ts):
    lane_head = lax.broadcasted_iota(jnp.int32, (1, GLA_QK), 1) // GLA_DK

    def stack_heads(z):
        return jnp.concatenate(
            [jnp.where(lane_head == h, z, 0.0) for h in range(GLA_HEADS)], axis=0).astype(BF16)

    zb = jnp.zeros((GLA_CHUNK, GLA_DV), BF16)
    for c in range(ts // GLA_CHUNK):
        rows = pl.ds(c * GLA_CHUNK, GLA_CHUNK)
        la_hi, la_lo = _split_bf16(labuf[rows, :])
        tri = tri_ref[...]
        b = _dot(tri, la_hi) + _dot(tri, la_lo)
        yield
        b_mid = b[GLA_CHUNK // 2 - 1:GLA_CHUNK // 2, :]
        b_end = b[GLA_CHUNK - 1:GLA_CHUNK, :]
        q = proj[rows, E_GQ:E_GQ + GLA_QK]
        k = proj[rows, E_GK:E_GK + GLA_QK] * (GLA_DK ** -0.5)
        vb = proj[rows, E_GV:E_GV + GLA_HEADS * GLA_DV].astype(BF16)
        vh = [vb[:, h * GLA_DV:(h + 1) * GLA_DV] for h in range(GLA_HEADS)]
        s_cat = _dot_nt((q * jnp.exp(b - b_mid)).astype(BF16), stack_heads(k * jnp.exp(b_mid - b)))
        st = gla_st[...]
        inter = _dot_nt(stack_heads(q * jnp.exp(b)), st.astype(BF16))
        upd = _dot_tn(jnp.concatenate(vh, axis=0), stack_heads(k * jnp.exp(b_end - b)))
        gla_st[...] = jnp.exp(b_end) * st + upd
        yield
        p = (s_cat * ctile_ref[...]).astype(BF16)
        vbd = jnp.concatenate(
            [jnp.concatenate([vh[h] if j == h else zb for j in range(GLA_HEADS)], axis=1)
             for h in range(GLA_HEADS)], axis=0)
        out_cat = _dot(p, vbd)
        yield
        for h in range(GLA_HEADS):
            g = proj[rows, E_GG + h * GLA_DV:E_GG + (h + 1) * GLA_DV]
            o = out_cat[:, h * GLA_DV:(h + 1) * GLA_DV] + inter[h * GLA_CHUNK:(h + 1) * GLA_CHUNK, :]
            obuf[rows, GROUP_W + h * GLA_DV:GROUP_W + (h + 1) * GLA_DV] = _rms(o) * _silu(g)
        yield


def _pair_lanes(t):
    return jnp.concatenate([t[0::2], t[1::2]], axis=-1)


def _even_mixer(h, norm_w, w_in, w_lr, b_lr, head_norm, w_out, ts):
    bsz, seq, _ = h.shape
    hh = jnp.arange(RET_HEADS, dtype=F32)
    log_g = jnp.log1p(-jnp.exp2(-5.0 - hh))
    idx = jnp.arange(RET_CHUNK, dtype=F32)
    rel = idx[:, None] - idx[None, :]
    causal = rel >= 0
    dmat = jnp.where(causal, jnp.exp(log_g[:, None, None] * jnp.where(causal, rel, 0.0)), 0.0)
    full = (RET_HEADS, RET_CHUNK, RET_DK)
    pair = (RET_HEADS // 2, RET_CHUNK, PAIR)
    w_end = jnp.broadcast_to(jnp.exp(log_g[:, None] * (RET_CHUNK - 1.0 - idx)[None, :])[:, :, None], full)
    w_inb = jnp.broadcast_to(jnp.exp(log_g[:, None] * (idx + 1.0)[None, :])[:, :, None], full)
    dec = jnp.broadcast_to(jnp.exp(log_g * RET_CHUNK)[:, None, None], full)

    half = RET_DK // 2
    inv = ROPE_BASE ** (-jnp.arange(half, dtype=F32) / half)
    ang = jnp.arange(seq).astype(F32)[:, None] * inv[None, :]
    cos2 = jnp.concatenate([jnp.cos(ang), jnp.cos(ang)], axis=-1)
    sin2 = jnp.concatenate([-jnp.sin(ang), jnp.sin(ang)], axis=-1)

    tril = jnp.tril(jnp.ones((GLA_CHUNK, GLA_CHUNK), F32))
    ctile = jnp.tile(tril, (1, GLA_HEADS))

    w_in_p = jnp.pad(w_in, ((0, 0), (0, E_NP - w_in.shape[1]))).astype(BF16)
    w_lr_p = jnp.pad(w_lr, ((0, LANE - GLA_RANK), (0, 0))).astype(BF16)

    const2 = lambda shape: pl.BlockSpec(shape, lambda b, s: (0, 0), pipeline_mode=pl.Buffered(1))
    const3 = lambda shape: pl.BlockSpec(shape, lambda b, s: (0, 0, 0), pipeline_mode=pl.Buffered(1))
    return pl.pallas_call(
        _even_body,
        grid=(bsz // MIX_G, seq // ts),
        in_specs=[
            pl.BlockSpec((MIX_G, ts, D_MODEL), lambda b, s: (b, s, 0)),
            const2((1, D_MODEL)),
            const2((D_MODEL, E_NP)),
            const2((LANE, GLA_QK)),
            const2((1, GLA_QK)),
            pl.BlockSpec((ts, RET_DK), lambda b, s: (s, 0)),
            pl.BlockSpec((ts, RET_DK), lambda b, s: (s, 0)),
            const3(pair), const3(pair), const3(pair), const3(full),
            const2((GLA_CHUNK, GLA_CHUNK)),
            const2((GLA_CHUNK, GLA_QK)),
            const2((1, D_MODEL)),
            const2((D_MODEL, D_MODEL)),
        ],
        out_specs=pl.BlockSpec((MIX_G, ts, D_MODEL), lambda b, s: (b, s, 0)),
        out_shape=jax.ShapeDtypeStruct(h.shape, F32),
        scratch_shapes=[
            pltpu.VMEM((MIX_G, ts, E_NP), F32),
            pltpu.VMEM((MIX_G, ts, GLA_QK), F32),
            pltpu.VMEM((MIX_G, ts, D_MODEL), F32),
            pltpu.VMEM((MIX_G,) + full, F32),
            pltpu.VMEM((MIX_G, GLA_DV, GLA_QK), F32),
        ],
        compiler_params=pltpu.CompilerParams(
            dimension_semantics=("parallel", "arbitrary"), vmem_limit_bytes=VMEM_LIMIT),
        name="even_mixer",
    )(h, norm_w.reshape(1, -1), w_in_p, w_lr_p, b_lr.reshape(1, -1), cos2, sin2,
      _pair_lanes(dmat), _pair_lanes(w_inb), _pair_lanes(w_end), dec, tril.astype(BF16), ctile,
      head_norm.reshape(1, -1), w_out.astype(BF16))


def _causal_conv(buf, x, w_ref, b_ref, ts):
    buf[SUBLANE:SUBLANE + ts, :] = x
    acc = b_ref[...] + w_ref[CONV_W - 1:CONV_W, :] * x
    for j in range(CONV_W - 1):
        acc = acc + w_ref[j:j + 1, :] * buf[pl.ds(SUBLANE - (CONV_W - 1) + j, ts), :]
    buf[0:SUBLANE, :] = buf[ts:ts + SUBLANE, :]
    return acc


def _odd_body(x_ref, nw_ref, win_ref, lcw_ref, lcb_ref, mcw_ref, mcb_ref,
              wa_ref, ba_ref, wx_ref, bx_ref, lam_ref, wqk_ref, bif_ref, tri_ref,
              mnw_ref, wout_ref, o_ref,
              proj, lbuf, mbuf, abuf, ubuf, qk, ibuf, fbuf, obuf, lru_h, ml_c, ml_m):
    groups = x_ref.shape[0]
    ts = x_ref.shape[1]

    @pl.when(pl.program_id(1) == 0)
    def _():
        lbuf[:, 0:SUBLANE, :] = jnp.zeros((groups, SUBLANE, LRU_W), F32)
        mbuf[:, 0:SUBLANE, :] = jnp.zeros((groups, SUBLANE, GROUP_W), F32)
        lru_h[...] = jnp.zeros_like(lru_h)
        ml_c[...] = jnp.zeros_like(ml_c)
        ml_m[...] = jnp.zeros_like(ml_m)

    def sequence(g):
        x = x_ref[g]
        hn = (_rms(x) * nw_ref[...]).astype(BF16)
        for c0 in range(0, O_NP, 512):
            c1 = min(c0 + 512, O_NP)
            proj[g, :, c0:c1] = _dot(hn, win_ref[:, c0:c1])
            yield
        yield from _round_robin([
            _lru_chain(proj.at[g], lbuf.at[g], abuf.at[g], ubuf.at[g], obuf.at[g], lru_h.at[g],
                       lcw_ref, lcb_ref, wa_ref, ba_ref, wx_ref, bx_ref, lam_ref, ts),
            _mlstm_chain(proj.at[g], mbuf.at[g], qk.at[g], ibuf.at[g], fbuf.at[g], obuf.at[g],
                         ml_c.at[g], ml_m.at[g], mcw_ref, mcb_ref, wqk_ref, bif_ref, tri_ref, mnw_ref, ts)])
        o_ref[g] = x + _dot(obuf[g].astype(BF16), wout_ref[...])
        yield

    _run(_round_robin([sequence(g) for g in range(groups)], lag=ODD_LAG))


def _lru_chain(proj, lbuf, abuf, ubuf, obuf, lru_h, lcw_ref, lcb_ref, wa_ref, ba_ref, wx_ref, bx_ref, lam_ref, ts):
    lxc = _causal_conv(lbuf, proj[:, O_LX:O_LX + LRU_W], lcw_ref, lcb_ref, ts)
    xb = lxc.astype(BF16)
    r = _sigmoid(_dot(xb, wa_ref[...]) + ba_ref[...])
    i = _sigmoid(_dot(xb, wx_ref[...]) + bx_ref[...])
    yield
    lam = lam_ref[...]
    softplus_neg_lam = jnp.maximum(-lam, 0.0) + jnp.log1p(jnp.exp(-jnp.abs(lam)))
    log_a = (-LRU_C) * r * softplus_neg_lam
    a = jnp.exp(log_a)
    u = jnp.sqrt(-jnp.tanh(log_a) * (a * a + 1.0)) * (i * lxc)
    rmod = lax.broadcasted_iota(jnp.int32, (ts, 1), 0) % SUBLANE
    d = 1
    while d < SUBLANE:
        keep = rmod >= d
        a_s = jnp.where(keep, pltpu.roll(a, d, 0), 1.0)
        u_s = jnp.where(keep, pltpu.roll(u, d, 0), 0.0)
        u = a * u_s + u
        a = a * a_s
        d *= 2
    abuf[...] = a
    ubuf[...] = u
    hprev = lru_h[0:1, :]
    for r0 in range(0, ts, SUBLANE):
        hg = ubuf[r0:r0 + SUBLANE, :] + abuf[r0:r0 + SUBLANE, :] * hprev
        ubuf[r0:r0 + SUBLANE, :] = hg
        hprev = hg[SUBLANE - 1:SUBLANE, :]
    lru_h[0:1, :] = hprev
    ly = proj[:, O_LY:O_LY + LRU_W]
    gelu = ly * (0.5 * (1.0 + jnp.tanh(np.float32(np.sqrt(2.0 / np.pi)) * (ly + 0.044715 * (ly * ly * ly)))))
    obuf[:, 0:LRU_W] = ubuf[...] * gelu
    yield


def _mlstm_chain(proj, mbuf, qk, ibuf, fbuf, obuf, ml_c, ml_m, mcw_ref, mcb_ref, wqk_ref, bif_ref, tri_ref,
                 mnw_ref, ts):
    mc = _silu(_causal_conv(mbuf, proj[:, O_MU:O_MU + GROUP_W], mcw_ref, mcb_ref, ts))
    for h in range(ML_HEADS):
        qkh = _dot(mc[:, h * ML_DH:(h + 1) * ML_DH].astype(BF16), wqk_ref[h])
        qk[:, h * ML_DH:(h + 1) * ML_DH] = qkh[:, :ML_DH]
        qk[:, GROUP_W + h * ML_DH:GROUP_W + (h + 1) * ML_DH] = qkh[:, ML_DH:] * (ML_DH ** -0.5)
    gates = proj[:, O_IF:O_IF + LANE] + bif_ref[...]
    ibuf[...] = gates
    fbuf[...] = pltpu.roll(_log_sigmoid(gates), LANE - ML_HEADS, 1)
    yield

    row_i = lax.broadcasted_iota(jnp.int32, (ML_CHUNK, ML_CHUNK), 0)
    col_i = lax.broadcasted_iota(jnp.int32, (ML_CHUNK, ML_CHUNK), 1)
    causal = row_i >= col_i
    ones_col = jnp.where(col_i == 0, 1.0, 0.0).astype(BF16)

    for c in range(ts // ML_CHUNK):
        rows = pl.ds(c * ML_CHUNK, ML_CHUNK)
        f_hi, f_lo = _split_bf16(fbuf[rows, :])
        tri = tri_ref[...]
        fc = _dot(tri, f_hi) + _dot(tri, f_lo)
        yield
        dm = ibuf[rows, :] - fc
        dm_t = dm.T
        b_end = fc[ML_CHUNK - 1:ML_CHUNK, :]
        a_col = b_end + dm
        a_max = jnp.max(a_col, axis=0, keepdims=True)
        w_end = jnp.exp(a_col - a_max)
        m_old = ml_m[0:1, :]
        m_new = jnp.maximum(b_end + m_old, a_max)
        g_old = jnp.exp(b_end + m_old - m_new)
        g_new = jnp.exp(a_max - m_new)
        ml_m[0:1, :] = m_new
        s_cat, qc, vh = [], [], []
        for pr in range(ML_HEADS // 2):
            lo = pr * PAIR
            q2 = qk[rows, lo:lo + PAIR].astype(BF16)
            k2f = qk[rows, GROUP_W + lo:GROUP_W + lo + PAIR]
            k2 = k2f.astype(BF16)
            s_cat.append(_dot_nt(q2, _bdiag2(k2[:, :LANE], k2[:, LANE:])))
            for j in range(2):
                h = 2 * pr + j
                v = proj[rows, O_MV + h * ML_DH:O_MV + (h + 1) * ML_DH].astype(BF16)
                cx = ml_c[h]
                qc.append(_dot(q2[:, j * LANE:(j + 1) * LANE], cx.astype(BF16)))
                wk = (w_end[:, h:h + 1] * k2f[:, j * LANE:(j + 1) * LANE]).astype(BF16)
                updx = _dot_tn(wk, jnp.concatenate([v, ones_col], axis=1))
                ml_c[h] = g_old[:, h:h + 1] * cx + g_new[:, h:h + 1] * updx
                vh.append(v)
        yield
        mxs, sms, dens, nums = [], [], [], []
        for h in range(ML_HEADS):
            d_row = dm_t[h:h + 1, :]
            m_h = m_old[:, h:h + 1]
            cm = jnp.max(jnp.where(causal, d_row, -jnp.inf), axis=1, keepdims=True)
            mx = jnp.maximum(cm, m_h)
            p = jnp.where(causal, jnp.exp(d_row - mx), 0.0)
            s = s_cat[h // 2][:, (h % 2) * LANE:(h % 2 + 1) * LANE] * p
            mxs.append(mx)
            sms.append(s.astype(BF16))
            dens.append(jnp.sum(s, axis=1, keepdims=True))
        for pr in range(ML_HEADS // 2):
            nums.append(_dot(jnp.concatenate([sms[2 * pr], sms[2 * pr + 1]], axis=1),
                             _bdiag2(vh[2 * pr], vh[2 * pr + 1])))
        yield
        for h in range(ML_HEADS):
            m_h = m_old[:, h:h + 1]
            w_inter = jnp.exp(m_h - mxs[h])
            num = nums[h // 2][:, (h % 2) * LANE:(h % 2 + 1) * LANE] + w_inter * qc[h][:, :ML_DH]
            den = dens[h] + w_inter * qc[h][:, ML_DH:ML_DH + 1]
            m_t = fc[:, h:h + 1] + mxs[h]
            hh = num / jnp.maximum(jnp.abs(den), jnp.exp(-m_t))
            og = _sigmoid(proj[rows, O_MO + h * ML_DH:O_MO + (h + 1) * ML_DH])
            hs = slice(GROUP_W + h * ML_DH, GROUP_W + (h + 1) * ML_DH)
            obuf[rows, hs] = _rms(og * hh) * mnw_ref[:, h * ML_DH:(h + 1) * ML_DH]
        yield


def _block_diag(w):
    g, n, _ = w.shape
    eye = jnp.eye(g, dtype=w.dtype)
    return (eye[:, None, :, None] * w[:, :, None, :]).reshape(g * n, g * n)


def _odd_mixer(h, norm_w, w_in, lcw, lcb, wa, ba, wx, bx, lam, mcw, mcb, wq, wk, bi, bf, ml_norm, w_out, ts):
    bsz, seq, _ = h.shape
    w_in_p = jnp.pad(w_in, ((0, 0), (0, O_NP - w_in.shape[1]))).astype(BF16)
    wqk = jnp.concatenate([wq, wk], axis=-1).astype(BF16)
    bif = jnp.pad(jnp.concatenate([bi, bf]), (0, LANE - 2 * ML_HEADS)).reshape(1, LANE)
    tri = jnp.tril(jnp.ones((ML_CHUNK, ML_CHUNK), F32)).astype(BF16)
    row = lambda t: t.reshape(1, -1)

    const2 = lambda shape: pl.BlockSpec(shape, lambda b, s: (0, 0), pipeline_mode=pl.Buffered(1))
    const3 = lambda shape: pl.BlockSpec(shape, lambda b, s: (0, 0, 0), pipeline_mode=pl.Buffered(1))
    return pl.pallas_call(
        _odd_body,
        grid=(bsz // MIX_G, seq // ts),
        in_specs=[
            pl.BlockSpec((MIX_G, ts, D_MODEL), lambda b, s: (b, s, 0)),
            const2((1, D_MODEL)),
            const2((D_MODEL, O_NP)),
            const2((CONV_W, LRU_W)), const2((1, LRU_W)),
            const2((CONV_W, GROUP_W)), const2((1, GROUP_W)),
            const2((LRU_W, LRU_W)), const2((1, LRU_W)),
            const2((LRU_W, LRU_W)), const2((1, LRU_W)),
            const2((1, LRU_W)),
            const3((ML_HEADS, ML_DH, 2 * ML_DH)),
            const2((1, LANE)),
            const2((ML_CHUNK, ML_CHUNK)),
            const2((1, GROUP_W)),
            const2((D_MODEL, D_MODEL)),
        ],
        out_specs=pl.BlockSpec((MIX_G, ts, D_MODEL), lambda b, s: (b, s, 0)),
        out_shape=jax.ShapeDtypeStruct(h.shape, F32),
        scratch_shapes=[
            pltpu.VMEM((MIX_G, ts, O_NP), F32),
            pltpu.VMEM((MIX_G, ts + SUBLANE, LRU_W), F32),
            pltpu.VMEM((MIX_G, ts + SUBLANE, GROUP_W), F32),
            pltpu.VMEM((MIX_G, ts, LRU_W), F32),
            pltpu.VMEM((MIX_G, ts, LRU_W), F32),
            pltpu.VMEM((MIX_G, ts, 2 * GROUP_W), F32),
            pltpu.VMEM((MIX_G, ts, LANE), F32),
            pltpu.VMEM((MIX_G, ts, LANE), F32),
            pltpu.VMEM((MIX_G, ts, D_MODEL), F32),
            pltpu.VMEM((MIX_G, SUBLANE, LRU_W), F32),
            pltpu.VMEM((MIX_G, ML_HEADS, ML_DH, 2 * ML_DH), F32),
            pltpu.VMEM((MIX_G, SUBLANE, LANE), F32),
        ],
        compiler_params=pltpu.CompilerParams(
            dimension_semantics=("parallel", "arbitrary"), vmem_limit_bytes=VMEM_LIMIT),
        name="odd_mixer",
    )(h, row(norm_w), w_in_p, lcw, row(lcb), mcw, row(mcb),
      _block_diag(wa).astype(BF16), row(ba), _block_diag(wx).astype(BF16), row(bx), row(lam),
      wqk, bif, tri, row(ml_norm), w_out.astype(BF16))


def kernel(x, ffn1_norm, ffn1_wgu, ffn1_wd, mix_norm, ffn2_norm, ffn2_wgu, ffn2_wd, e_w_in, e_w_lr_up, e_b_lr, e_head_norm, e_w_out, o_w_in, o_lru_conv_w, o_lru_conv_b, o_lru_wa, o_lru_ba, o_lru_wx, o_lru_bx, o_lru_lambda, o_ml_conv_w, o_ml_conv_b, o_ml_wq, o_ml_wk, o_ml_bi, o_ml_bf, o_ml_norm, o_w_out, final_norm):
    bsz, seq, d = x.shape
    depth = ffn1_norm.shape[0]
    h = x
    for layer in range(depth):
        j = layer // 2
        h = _ffn(h.reshape(bsz * seq, d), ffn1_norm[layer], ffn1_wgu, ffn1_wd, layer,
                 final_norm, False).reshape(bsz, seq, d)
        if layer % 2 == 0:
            h = _even_mixer(h, mix_norm[layer], e_w_in[j], e_w_lr_up[j], e_b_lr[j], e_head_norm[j],
                            e_w_out[j], MIX_TS)
        else:
            h = _odd_mixer(h, mix_norm[layer], o_w_in[j], o_lru_conv_w[j], o_lru_conv_b[j],
                           o_lru_wa[j], o_lru_ba[j], o_lru_wx[j], o_lru_bx[j], o_lru_lambda[j],
                           o_ml_conv_w[j], o_ml_conv_b[j], o_ml_wq[j], o_ml_wk[j], o_ml_bi[j], o_ml_bf[j],
                           o_ml_norm[j], o_w_out[j], MIX_TS)
        h = _ffn(h.reshape(bsz * seq, d), ffn2_norm[layer], ffn2_wgu, ffn2_wd, layer,
                 final_norm, layer == depth - 1).reshape(bsz, seq, d)
    return h
```

```python
import functools

import numpy as np
import jax
import jax.numpy as jnp
from jax import lax
from jax.experimental import pallas as pl
from jax.experimental.pallas import tpu as pltpu

F32 = jnp.float32
BF16 = jnp.bfloat16

D_MODEL = 1024
D_FF = 2816
GROUP_W = D_MODEL // 2
EPS = 1e-6

RET_HEADS = 4
RET_DK = 128
RET_CHUNK = 128
ROPE_BASE = 10000.0

GLA_HEADS = 4
GLA_DK = 64
GLA_DV = 128
GLA_RANK = 16
GLA_TAU = 16.0
GLA_CHUNK = 64
GLA_QK = GLA_HEADS * GLA_DK

LRU_W = GROUP_W
LRU_BLOCKS = 8
LRU_BS = LRU_W // LRU_BLOCKS
LRU_C = 8.0
CONV_W = 4

ML_HEADS = 4
ML_DH = 128
ML_CHUNK = 128

LANE = 128
SUBLANE = 8
VMEM_LIMIT = 56 * 1024 * 1024

E_RQ, E_RK, E_RV, E_RG = 0, 512, 1024, 1536
E_GQ, E_GK, E_GV, E_GG, E_LR = 2048, 2304, 2560, 3072, 3584
E_NP = 3712
O_LY, O_LX, O_MU, O_MV, O_MO, O_IF = 0, 512, 1024, 1536, 2048, 2560
O_NP = 2688

FFN_TM = 1024
FFN_SUB = 256
FFN_FC = 256
FFN_WCOLS = 512
FFN_WROWS = 256
MIX_TS = 256
MIX_G = 2
EVEN_LAG = 0
ODD_LAG = 6
PAIR = 2 * LANE


def _dot(a, b):
    return jnp.dot(a, b, preferred_element_type=F32)


def _dot_nt(a, b):
    return lax.dot_general(a, b, (((1,), (1,)), ((), ())), preferred_element_type=F32)


def _dot_tn(a, b):
    return lax.dot_general(a, b, (((0,), (0,)), ((), ())), preferred_element_type=F32)


def _rms(x):
    return x * lax.rsqrt(jnp.mean(x * x, axis=-1, keepdims=True) + EPS)


def _sigmoid(x):
    return 1.0 / (1.0 + jnp.exp(-x))


def _silu(x):
    return x * _sigmoid(x)


def _log_sigmoid(x):
    return jnp.minimum(x, 0.0) - jnp.log1p(jnp.exp(-jnp.abs(x)))


def _split_bf16(x):
    hi = x.astype(BF16)
    lo = (x - hi.astype(F32)).astype(BF16)
    return hi, lo


def _bdiag2(a, b):
    z = jnp.zeros_like(a)
    return jnp.concatenate([jnp.concatenate([a, z], axis=1), jnp.concatenate([z, b], axis=1)], axis=0)


def _round_robin(gens, lag=0):
    gens = list(gens)
    done = [False] * len(gens)
    rnd = 0
    while not all(done):
        for i, gen in enumerate(gens):
            if done[i] or rnd < lag * i:
                continue
            try:
                next(gen)
            except StopIteration:
                done[i] = True
        rnd += 1
        yield


def _run(gen):
    for _ in gen:
        pass


def _stage_weight(layer, w_hbm, w_bf, stage, sem, axis, width):
    n = w_bf.shape[axis] // width

    def window(j):
        sl = pl.ds(j * width, width)
        return (sl, slice(None)) if axis == 0 else (slice(None), sl)

    def copy(j):
        return pltpu.make_async_copy(w_hbm.at[(layer,) + window(j)], stage.at[j % 2], sem.at[j % 2])

    copy(0).start()
    for j in range(n):
        if j + 1 < n:
            copy(j + 1).start()
        copy(j).wait()
        w_bf[window(j)] = stage[j % 2].astype(BF16)


def _ffn_body(final, layer, x_ref, nw_ref, wgu_hbm, wd_hbm, fw_ref, o_ref,
              act_ref, wgu_ref, wd_ref, stage_gu, stage_d, sem_gu, sem_d):
    @pl.when(pl.program_id(0) == 0)
    def _():
        _stage_weight(layer, wgu_hbm, wgu_ref, stage_gu, sem_gu, 1, FFN_WCOLS)
        _stage_weight(layer, wd_hbm, wd_ref, stage_d, sem_d, 0, FFN_WROWS)

    for r0 in range(0, FFN_TM, FFN_SUB):
        rows = pl.ds(r0, FFN_SUB)
        x = x_ref[rows, :]
        xn = (_rms(x) * nw_ref[...]).astype(BF16)
        for j in range(D_FF // FFN_FC):
            c0 = j * FFN_FC
            g = _dot(xn, wgu_ref[:, c0:c0 + FFN_FC])
            u = _dot(xn, wgu_ref[:, D_FF + c0:D_FF + c0 + FFN_FC])
            act_ref[rows, c0:c0 + FFN_FC] = (_silu(g) * u).astype(BF16)
        h = x + 0.5 * _dot(act_ref[rows, :], wd_ref[...])
        if final:
            h = _rms(h) * fw_ref[...]
        o_ref[rows, :] = h


def _ffn(h, norm_w, wgu_all, wd_all, layer, final_w, final):
    t = h.shape[0]
    const = lambda shape: pl.BlockSpec(shape, lambda i: (0, 0), pipeline_mode=pl.Buffered(1))
    return pl.pallas_call(
        functools.partial(_ffn_body, final, layer),
        grid=(t // FFN_TM,),
        in_specs=[
            pl.BlockSpec((FFN_TM, D_MODEL), lambda i: (i, 0)),
            const((1, D_MODEL)),
            pl.BlockSpec(memory_space=pl.ANY),
            pl.BlockSpec(memory_space=pl.ANY),
            const((1, D_MODEL)),
        ],
        out_specs=pl.BlockSpec((FFN_TM, D_MODEL), lambda i: (i, 0)),
        out_shape=jax.ShapeDtypeStruct((t, D_MODEL), F32),
        scratch_shapes=[
            pltpu.VMEM((FFN_TM, D_FF), BF16),
            pltpu.VMEM((D_MODEL, 2 * D_FF), BF16),
            pltpu.VMEM((D_FF, D_MODEL), BF16),
            pltpu.VMEM((2, D_MODEL, FFN_WCOLS), F32),
            pltpu.VMEM((2, FFN_WROWS, D_MODEL), F32),
            pltpu.SemaphoreType.DMA((2,)),
            pltpu.SemaphoreType.DMA((2,)),
        ],
        compiler_params=pltpu.CompilerParams(
            dimension_semantics=("arbitrary",), vmem_limit_bytes=VMEM_LIMIT),
        name="ffn_final" if final else "ffn",
    )(h, norm_w.reshape(1, -1), wgu_all, wd_all, final_w.reshape(1, -1))


def _even_body(x_ref, nw_ref, win_ref, wlr_ref, blr_ref, cos_ref, sin_ref,
               dmat_ref, winb_ref, wendb_ref, dec_ref, tri_ref, ctile_ref,
               hnw_ref, wout_ref, o_ref,
               proj, labuf, obuf, ret_s, gla_st):
    groups = x_ref.shape[0]
    ts = x_ref.shape[1]

    @pl.when(pl.program_id(1) == 0)
    def _():
        ret_s[...] = jnp.zeros_like(ret_s)
        gla_st[...] = jnp.zeros_like(gla_st)

    def sequence(g):
        x = x_ref[g]
        pg = proj.at[g]
        hn = (_rms(x) * nw_ref[...]).astype(BF16)
        for c0 in range(0, E_NP, 512):
            c1 = min(c0 + 512, E_NP)
            pg[:, c0:c1] = _dot(hn, win_ref[:, c0:c1])
            yield
        glr = pg[:, E_LR:E_LR + LANE].astype(BF16)
        labuf[g] = _log_sigmoid(_dot(glr, wlr_ref[...]) + blr_ref[...]) * (1.0 / GLA_TAU)
        cos = cos_ref[...]
        sin = sin_ref[...]
        for c0 in range(0, 2 * RET_HEADS * RET_DK, RET_DK):
            xx = pg[:, c0:c0 + RET_DK]
            r = xx * cos + pltpu.roll(xx, RET_DK // 2, 1) * sin
            if c0 >= E_RK:
                r = r * (RET_DK ** -0.5)
            pg[:, c0:c0 + RET_DK] = r
        yield
        yield from _round_robin([
            _ret_chain(pg, obuf.at[g], ret_s.at[g], dmat_ref, winb_ref, wendb_ref, dec_ref, ts),
            _gla_chain(pg, labuf.at[g], obuf.at[g], gla_st.at[g], tri_ref, ctile_ref, ts)])
        o_ref[g] = x + _dot((obuf[g] * hnw_ref[...]).astype(BF16), wout_ref[...])
        yield

    _run(_round_robin([sequence(g) for g in range(groups)], lag=EVEN_LAG))


def _ret_chain(proj, obuf, ret_s, dmat_ref, winb_ref, wendb_ref, dec_ref, ts):
    for c in range(ts // RET_CHUNK):
        rows = pl.ds(c * RET_CHUNK, RET_CHUNK)
        stage = []
        for pr in range(RET_HEADS // 2):
            lo = pr * PAIR
            q2 = proj[rows, E_RQ + lo:E_RQ + lo + PAIR].astype(BF16)
            k2 = proj[rows, E_RK + lo:E_RK + lo + PAIR].astype(BF16)
            v2 = proj[rows, E_RV + lo:E_RV + lo + PAIR]
            s0 = ret_s[2 * pr]
            s1 = ret_s[2 * pr + 1]
            s_cat = _dot_nt(q2, _bdiag2(k2[:, :LANE], k2[:, LANE:]))
            inter = _dot(q2, _bdiag2(s0.astype(BF16), s1.astype(BF16)))
            full = _dot_tn(k2, (wendb_ref[pr] * v2).astype(BF16))
            ret_s[2 * pr] = dec_ref[2 * pr] * s0 + full[:LANE, :LANE]
            ret_s[2 * pr + 1] = dec_ref[2 * pr + 1] * s1 + full[LANE:, LANE:]
            stage.append((s_cat, inter, v2.astype(BF16)))
        yield
        outs = []
        for pr in range(RET_HEADS // 2):
            s_cat, inter, vb = stage[pr]
            p = (s_cat * dmat_ref[pr]).astype(BF16)
            outs.append(_dot(p, _bdiag2(vb[:, :LANE], vb[:, LANE:])) + inter * winb_ref[pr])
        yield
        for h in range(RET_HEADS):
            o = outs[h // 2][:, (h % 2) * LANE:(h % 2 + 1) * LANE]
            g = proj[rows, E_RG + h * RET_DK:E_RG + (h + 1) * RET_DK]
            obuf[rows, h * RET_DK:(h + 1) * RET_DK] = _rms(o) * _silu(g)
        yield


def _gla_chain(proj, labuf, obuf, gla_st, tri_ref, ctile_ref, ts):
    lane_head = lax.broadcasted_iota(jnp.int32, (1, GLA_QK), 1) // GLA_DK

    def stack_heads(z):
        return jnp.concatenate(
            [jnp.where(lane_head == h, z, 0.0) for h in range(GLA_HEADS)], axis=0).astype(BF16)

    zb = jnp.zeros((GLA_CHUNK, GLA_DV), BF16)
    for c in range(ts // GLA_CHUNK):
        rows = pl.ds(c * GLA_CHUNK, GLA_CHUNK)
        la_hi, la_lo = _split_bf16(labuf[rows, :])
        tri = tri_ref[...]
        b = _dot(tri, la_hi) + _dot(tri, la_lo)
        yield
        b_mid = b[GLA_CHUNK // 2 - 1:GLA_CHUNK // 2, :]
        b_end = b[GLA_CHUNK - 1:GLA_CHUNK, :]
        q = proj[rows, E_GQ:E_GQ + GLA_QK]
        k = proj[rows, E_GK:E_GK + GLA_QK] * (GLA_DK ** -0.5)
        vb = proj[rows, E_GV:E_GV + GLA_HEADS * GLA_DV].astype(BF16)
        vh = [vb[:, h * GLA_DV:(h + 1) * GLA_DV] for h in range(GLA_HEADS)]
        s_cat = _dot_nt((q * jnp.exp(b - b_mid)).astype(BF16), stack_heads(k * jnp.exp(b_mid - b)))
        st = gla_st[...]
        inter = _dot_nt(stack_heads(q * jnp.exp(b)), st.astype(BF16))
        upd = _dot_tn(jnp.concatenate(vh, axis=0), stack_heads(k * jnp.exp(b_end - b)))
        gla_st[...] = jnp.exp(b_end) * st + upd
        yield
        p = (s_cat * ctile_ref[...]).astype(BF16)
        vbd = jnp.concatenate(
            [jnp.concatenate([vh[h] if j == h else zb for j in range(GLA_HEADS)], axis=1)
             for h in range(GLA_HEADS)], axis=0)
        out_cat = _dot(p, vbd)
        yield
        for h in range(GLA_HEADS):
            g = proj[rows, E_GG + h * GLA_DV:E_GG + (h + 1) * GLA_DV]
            o = out_cat[:, h * GLA_DV:(h + 1) * GLA_DV] + inter[h * GLA_CHUNK:(h + 1) * GLA_CHUNK, :]
            obuf[rows, GROUP_W + h * GLA_DV:GROUP_W + (h + 1) * GLA_DV] = _rms(o) * _silu(g)
        yield


def _pair_lanes(t):
    return jnp.concatenate([t[0::2], t[1::2]], axis=-1)


def _even_mixer(h, norm_w, w_in, w_lr, b_lr, head_norm, w_out, ts):
    bsz, seq, _ = h.shape
    hh = jnp.arange(RET_HEADS, dtype=F32)
    log_g = jnp.log1p(-jnp.exp2(-5.0 - hh))
    idx = jnp.arange(RET_CHUNK, dtype=F32)
    rel = idx[:, None] - idx[None, :]
    causal = rel >= 0
    dmat = jnp.where(causal, jnp.exp(log_g[:, None, None] * jnp.where(causal, rel, 0.0)), 0.0)
    full = (RET_HEADS, RET_CHUNK, RET_DK)
    pair = (RET_HEADS // 2, RET_CHUNK, PAIR)
    w_end = jnp.broadcast_to(jnp.exp(log_g[:, None] * (RET_CHUNK - 1.0 - idx)[None, :])[:, :, None], full)
    w_inb = jnp.broadcast_to(jnp.exp(log_g[:, None] * (idx + 1.0)[None, :])[:, :, None], full)
    dec = jnp.broadcast_to(jnp.exp(log_g * RET_CHUNK)[:, None, None], full)

    half = RET_DK // 2
    inv = ROPE_BASE ** (-jnp.arange(half, dtype=F32) / half)
    ang = jnp.arange(seq).astype(F32)[:, None] * inv[None, :]
    cos2 = jnp.concatenate([jnp.cos(ang), jnp.cos(ang)], axis=-1)
    sin2 = jnp.concatenate([-jnp.sin(ang), jnp.sin(ang)], axis=-1)

    tril = jnp.tril(jnp.ones((GLA_CHUNK, GLA_CHUNK), F32))
    ctile = jnp.tile(tril, (1, GLA_HEADS))

    w_in_p = jnp.pad(w_in, ((0, 0), (0, E_NP - w_in.shape[1]))).astype(BF16)
    w_lr_p = jnp.pad(w_lr, ((0, LANE - GLA_RANK), (0, 0))).astype(BF16)

    const2 = lambda shape: pl.BlockSpec(shape, lambda b, s: (0, 0), pipeline_mode=pl.Buffered(1))
    const3 = lambda shape: pl.BlockSpec(shape, lambda b, s: (0, 0, 0), pipeline_mode=pl.Buffered(1))
    return pl.pallas_call(
        _even_body,
        grid=(bsz // MIX_G, seq // ts),
        in_specs=[
            pl.BlockSpec((MIX_G, ts, D_MODEL), lambda b, s: (b, s, 0)),
            const2((1, D_MODEL)),
            const2((D_MODEL, E_NP)),
            const2((LANE, GLA_QK)),
            const2((1, GLA_QK)),
            pl.BlockSpec((ts, RET_DK), lambda b, s: (s, 0)),
            pl.BlockSpec((ts, RET_DK), lambda b, s: (s, 0)),
            const3(pair), const3(pair), const3(pair), const3(full),
            const2((GLA_CHUNK, GLA_CHUNK)),
            const2((GLA_CHUNK, GLA_QK)),
            const2((1, D_MODEL)),
            const2((D_MODEL, D_MODEL)),
        ],
        out_specs=pl.BlockSpec((MIX_G, ts, D_MODEL), lambda b, s: (b, s, 0)),
        out_shape=jax.ShapeDtypeStruct(h.shape, F32),
        scratch_shapes=[
            pltpu.VMEM((MIX_G, ts, E_NP), F32),
            pltpu.VMEM((MIX_G, ts, GLA_QK), F32),
            pltpu.VMEM((MIX_G, ts, D_MODEL), F32),
            pltpu.VMEM((MIX_G,) + full, F32),
            pltpu.VMEM((MIX_G, GLA_DV, GLA_QK), F32),
        ],
        compiler_params=pltpu.CompilerParams(
            dimension_semantics=("parallel", "arbitrary"), vmem_limit_bytes=VMEM_LIMIT),
        name="even_mixer",
    )(h, norm_w.reshape(1, -1), w_in_p, w_lr_p, b_lr.reshape(1, -1), cos2, sin2,
      _pair_lanes(dmat), _pair_lanes(w_inb), _pair_lanes(w_end), dec, tril.astype(BF16), ctile,
      head_norm.reshape(1, -1), w_out.astype(BF16))


def _tiles(ref, t0, n, rows=slice(None)):
    return jnp.concatenate([ref[t0 + t, rows, :] for t in range(n)], axis=1)


def _set_tiles(ref, t0, val):
    for t in range(val.shape[1] // LANE):
        ref[t0 + t] = val[:, t * LANE:(t + 1) * LANE]


def _permute_rows(perm_ref, x_bf16):
    return _dot(perm_ref[...], x_bf16)


def _causal_conv(tail, xb, w_ref, b_ref):
    ts = xb.shape[0]
    keep = (CONV_W - 1) * SUBLANE
    last = xb[ts - keep:, :]
    prev = tail[...]
    first = lax.broadcasted_iota(jnp.int32, (SUBLANE, 1), 0) == 0
    fix = [jnp.where(first, pltpu.roll(prev[m * SUBLANE:(m + 1) * SUBLANE, :], 1, 0),
                     pltpu.roll(last[m * SUBLANE:(m + 1) * SUBLANE, :], 1, 0)) for m in range(CONV_W - 1)]
    tail[...] = last
    xpad = jnp.concatenate(fix + [xb], axis=0)
    acc = b_ref[...] + w_ref[CONV_W - 1:CONV_W, :] * xb
    for j in range(CONV_W - 1):
        acc = acc + w_ref[j:j + 1, :] * xpad[j * SUBLANE:j * SUBLANE + ts, :]
    return acc


def _odd_body(x_ref, nw_ref, win_ref, lcw_ref, lcb_ref, mcw_ref, mcb_ref,
              wa_ref, ba_ref, wx_ref, bx_ref, lam_ref, wqk_ref, bif_ref, tri_ref,
              mnw_ref, wout_ref, perm_ref, unperm_ref, o_ref,
              proj, ltail, mtail, qk, ibuf, fbuf, obuf, lru_h, ml_c, ml_m):
    groups = x_ref.shape[0]
    ts = x_ref.shape[1]

    @pl.when(pl.program_id(1) == 0)
    def _():
        ltail[...] = jnp.zeros_like(ltail)
        mtail[...] = jnp.zeros_like(mtail)
        lru_h[...] = jnp.zeros_like(lru_h)
        ml_c[...] = jnp.zeros_like(ml_c)
        ml_m[...] = jnp.zeros_like(ml_m)

    def sequence(g):
        x = x_ref[g]
        hn = (_rms(x) * nw_ref[...]).astype(BF16)
        hn_blocked = _permute_rows(perm_ref, hn).astype(BF16)
        for c0 in range(0, O_NP, 512):
            c1 = min(c0 + 512, O_NP)
            lhs = hn_blocked if c1 <= O_MV else hn
            _set_tiles(proj.at[g], c0 // LANE, _dot(lhs, win_ref[:, c0:c1]))
            yield
        yield from _round_robin([
            _lru_chain(proj.at[g], ltail.at[g], obuf.at[g], lru_h.at[g],
                       lcw_ref, lcb_ref, wa_ref, ba_ref, wx_ref, bx_ref, lam_ref, unperm_ref, ts),
            _mlstm_chain(proj.at[g], mtail.at[g], qk.at[g], ibuf.at[g], fbuf.at[g], obuf.at[g],
                         ml_c.at[g], ml_m.at[g], mcw_ref, mcb_ref, wqk_ref, bif_ref, tri_ref, mnw_ref,
                         unperm_ref, ts)])
        o_ref[g] = x + _dot(_tiles(obuf.at[g], 0, D_MODEL // LANE).astype(BF16), wout_ref[...])
        yield

    _run(_round_robin([sequence(g) for g in range(groups)], lag=ODD_LAG))


def _lru_chain(proj, ltail, obuf, lru_h, lcw_ref, lcb_ref, wa_ref, ba_ref, wx_ref, bx_ref, lam_ref, unperm_ref, ts):
    lxc = _causal_conv(ltail, _tiles(proj, O_LX // LANE, LRU_W // LANE), lcw_ref, lcb_ref)
    xb = lxc.astype(BF16)
    r = _sigmoid(_dot(xb, wa_ref[...]) + ba_ref[...])
    i = _sigmoid(_dot(xb, wx_ref[...]) + bx_ref[...])
    yield
    lam = lam_ref[...]
    softplus_neg_lam = jnp.maximum(-lam, 0.0) + jnp.log1p(jnp.exp(-jnp.abs(lam)))
    log_a = (-LRU_C) * r * softplus_neg_lam
    a = jnp.exp(log_a)
    u = jnp.sqrt(-jnp.tanh(log_a) * (a * a + 1.0)) * (i * lxc)
    n = ts // SUBLANE
    hs, ps = [u[0:SUBLANE, :]], [a[0:SUBLANE, :]]
    for j in range(1, n):
        aj = a[j * SUBLANE:(j + 1) * SUBLANE, :]
        hs.append(aj * hs[-1] + u[j * SUBLANE:(j + 1) * SUBLANE, :])
        ps.append(aj * ps[-1])
    sub = lax.broadcasted_iota(jnp.int32, (SUBLANE, 1), 0)
    pt, ht = ps[-1], hs[-1]
    d = 1
    while d < SUBLANE:
        keep = sub >= d
        p_s = jnp.where(keep, pltpu.roll(pt, d, 0), 1.0)
        h_s = jnp.where(keep, pltpu.roll(ht, d, 0), 0.0)
        ht = pt * h_s + ht
        pt = pt * p_s
        d *= 2
    hprev = lru_h[0:1, :]
    block_end = ht + pt * hprev
    carry = jnp.where(sub == 0, hprev, pltpu.roll(block_end, 1, 0))
    lru_h[0:1, :] = block_end[SUBLANE - 1:SUBLANE, :]
    hfull = jnp.concatenate([hs[j] + ps[j] * carry for j in range(n)], axis=0)
    ly = _tiles(proj, O_LY // LANE, LRU_W // LANE)
    gelu = ly * (0.5 * (1.0 + jnp.tanh(np.float32(np.sqrt(2.0 / np.pi)) * (ly + 0.044715 * (ly * ly * ly)))))
    _set_tiles(obuf, 0, _permute_rows(unperm_ref, (hfull * gelu).astype(BF16)))
    yield


def _mlstm_chain(proj, mtail, qk, ibuf, fbuf, obuf, ml_c, ml_m, mcw_ref, mcb_ref, wqk_ref, bif_ref, tri_ref,
                 mnw_ref, unperm_ref, ts):
    mc_blocked = _silu(_causal_conv(mtail, _tiles(proj, O_MU // LANE, GROUP_W // LANE), mcw_ref, mcb_ref))
    mc = _permute_rows(unperm_ref, mc_blocked.astype(BF16)).astype(BF16)
    for h in range(ML_HEADS):
        qkh = _dot(mc[:, h * ML_DH:(h + 1) * ML_DH], wqk_ref[h])
        qk[h] = qkh[:, :ML_DH]
        qk[ML_HEADS + h] = qkh[:, ML_DH:] * (ML_DH ** -0.5)
    gates = proj[O_IF // LANE] + bif_ref[...]
    ibuf[...] = gates
    fbuf[...] = pltpu.roll(_log_sigmoid(gates), LANE - ML_HEADS, 1)
    yield

    row_i = lax.broadcasted_iota(jnp.int32, (ML_CHUNK, ML_CHUNK), 0)
    col_i = lax.broadcasted_iota(jnp.int32, (ML_CHUNK, ML_CHUNK), 1)
    causal = row_i >= col_i
    ones_col = jnp.where(col_i == 0, 1.0, 0.0).astype(BF16)

    for c in range(ts // ML_CHUNK):
        rows = pl.ds(c * ML_CHUNK, ML_CHUNK)
        f_hi, f_lo = _split_bf16(fbuf[rows, :])
        tri = tri_ref[...]
        fc = _dot(tri, f_hi) + _dot(tri, f_lo)
        yield
        dm = ibuf[rows, :] - fc
        dm_t = dm.T
        b_end = fc[ML_CHUNK - 1:ML_CHUNK, :]
        a_col = b_end + dm
        a_max = jnp.max(a_col, axis=0, keepdims=True)
        w_end = jnp.exp(a_col - a_max)
        m_old = ml_m[0:1, :]
        m_new = jnp.maximum(b_end + m_old, a_max)
        g_old = jnp.exp(b_end + m_old - m_new)
        g_new = jnp.exp(a_max - m_new)
        ml_m[0:1, :] = m_new
        s_cat, qc, vh = [], [], []
        for pr in range(ML_HEADS // 2):
            q2 = _tiles(qk, 2 * pr, 2, rows).astype(BF16)
            k2f = _tiles(qk, ML_HEADS + 2 * pr, 2, rows)
            k2 = k2f.astype(BF16)
            s_cat.append(_dot_nt(q2, _bdiag2(k2[:, :LANE], k2[:, LANE:])))
            for j in range(2):
                h = 2 * pr + j
                v = proj[O_MV // LANE + h, rows, :].astype(BF16)
                cx = ml_c[h]
                qc.append(_dot(q2[:, j * LANE:(j + 1) * LANE], cx.astype(BF16)))
                wk = (w_end[:, h:h + 1] * k2f[:, j * LANE:(j + 1) * LANE]).astype(BF16)
                updx = _dot_tn(wk, jnp.concatenate([v, ones_col], axis=1))
                ml_c[h] = g_old[:, h:h + 1] * cx + g_new[:, h:h + 1] * updx
                vh.append(v)
        yield
        mxs, sms, dens, nums = [], [], [], []
        for h in range(ML_HEADS):
            d_row = dm_t[h:h + 1, :]
            m_h = m_old[:, h:h + 1]
            cm = jnp.max(jnp.where(causal, d_row, -jnp.inf), axis=1, keepdims=True)
            mx = jnp.maximum(cm, m_h)
            p = jnp.where(causal, jnp.exp(d_row - mx), 0.0)
            s = s_cat[h // 2][:, (h % 2) * LANE:(h % 2 + 1) * LANE] * p
            mxs.append(mx)
            sms.append(s.astype(BF16))
            dens.append(jnp.sum(s, axis=1, keepdims=True))
        for pr in range(ML_HEADS // 2):
            nums.append(_dot(jnp.concatenate([sms[2 * pr], sms[2 * pr + 1]], axis=1),
                             _bdiag2(vh[2 * pr], vh[2 * pr + 1])))
        yield
        for h in range(ML_HEADS):
            m_h = m_old[:, h:h + 1]
            w_inter = jnp.exp(m_h - mxs[h])
            num = nums[h // 2][:, (h % 2) * LANE:(h % 2 + 1) * LANE] + w_inter * qc[h][:, :ML_DH]
            den = dens[h] + w_inter * qc[h][:, ML_DH:ML_DH + 1]
            m_t = fc[:, h:h + 1] + mxs[h]
            hh = num / jnp.maximum(jnp.abs(den), jnp.exp(-m_t))
            og = _sigmoid(proj[O_MO // LANE + h, rows, :])
            obuf[GROUP_W // LANE + h, rows, :] = _rms(og * hh) * mnw_ref[:, h * ML_DH:(h + 1) * ML_DH]
        yield


def _block_diag(w):
    g, n, _ = w.shape
    eye = jnp.eye(g, dtype=w.dtype)
    return (eye[:, None, :, None] * w[:, :, None, :]).reshape(g * n, g * n)


def _odd_mixer(h, norm_w, w_in, lcw, lcb, wa, ba, wx, bx, lam, mcw, mcb, wq, wk, bi, bf, ml_norm, w_out, ts):
    bsz, seq, _ = h.shape
    w_in_p = jnp.pad(w_in, ((0, 0), (0, O_NP - w_in.shape[1]))).astype(BF16)
    wqk = jnp.concatenate([wq, wk], axis=-1).astype(BF16)
    bif = jnp.pad(jnp.concatenate([bi, bf]), (0, LANE - 2 * ML_HEADS)).reshape(1, LANE)
    tri = jnp.tril(jnp.ones((ML_CHUNK, ML_CHUNK), F32)).astype(BF16)
    r = jnp.arange(ts)
    time_of_row = (r % SUBLANE) * (ts // SUBLANE) + r // SUBLANE
    perm = (time_of_row[:, None] == r[None, :]).astype(BF16)
    row = lambda t: t.reshape(1, -1)

    const2 = lambda shape: pl.BlockSpec(shape, lambda b, s: (0, 0), pipeline_mode=pl.Buffered(1))
    const3 = lambda shape: pl.BlockSpec(shape, lambda b, s: (0, 0, 0), pipeline_mode=pl.Buffered(1))
    return pl.pallas_call(
        _odd_body,
        grid=(bsz // MIX_G, seq // ts),
        in_specs=[
            pl.BlockSpec((MIX_G, ts, D_MODEL), lambda b, s: (b, s, 0)),
            const2((1, D_MODEL)),
            const2((D_MODEL, O_NP)),
            const2((CONV_W, LRU_W)), const2((1, LRU_W)),
            const2((CONV_W, GROUP_W)), const2((1, GROUP_W)),
            const2((LRU_W, LRU_W)), const2((1, LRU_W)),
            const2((LRU_W, LRU_W)), const2((1, LRU_W)),
            const2((1, LRU_W)),
            const3((ML_HEADS, ML_DH, 2 * ML_DH)),
            const2((1, LANE)),
            const2((ML_CHUNK, ML_CHUNK)),
            const2((1, GROUP_W)),
            const2((D_MODEL, D_MODEL)),
            const2((ts, ts)), const2((ts, ts)),
        ],
        out_specs=pl.BlockSpec((MIX_G, ts, D_MODEL), lambda b, s: (b, s, 0)),
        out_shape=jax.ShapeDtypeStruct(h.shape, F32),
        scratch_shapes=[
            pltpu.VMEM((MIX_G, O_NP // LANE, ts, LANE), F32),
            pltpu.VMEM((MIX_G, (CONV_W - 1) * SUBLANE, LRU_W), F32),
            pltpu.VMEM((MIX_G, (CONV_W - 1) * SUBLANE, GROUP_W), F32),
            pltpu.VMEM((MIX_G, 2 * GROUP_W // LANE, ts, LANE), F32),
            pltpu.VMEM((MIX_G, ts, LANE), F32),
            pltpu.VMEM((MIX_G, ts, LANE), F32),
            pltpu.VMEM((MIX_G, D_MODEL // LANE, ts, LANE), F32),
            pltpu.VMEM((MIX_G, SUBLANE, LRU_W), F32),
            pltpu.VMEM((MIX_G, ML_HEADS, ML_DH, 2 * ML_DH), F32),
            pltpu.VMEM((MIX_G, SUBLANE, LANE), F32),
        ],
        compiler_params=pltpu.CompilerParams(
            dimension_semantics=("parallel", "arbitrary"), vmem_limit_bytes=VMEM_LIMIT),
        name="odd_mixer",
    )(h, row(norm_w), w_in_p, lcw, row(lcb), mcw, row(mcb),
      _block_diag(wa).astype(BF16), row(ba), _block_diag(wx).astype(BF16), row(bx), row(lam),
      wqk, bif, tri, row(ml_norm), w_out.astype(BF16), perm, perm.T)


def kernel(x, ffn1_norm, ffn1_wgu, ffn1_wd, mix_norm, ffn2_norm, ffn2_wgu, ffn2_wd, e_w_in, e_w_lr_up, e_b_lr, e_head_norm, e_w_out, o_w_in, o_lru_conv_w, o_lru_conv_b, o_lru_wa, o_lru_ba, o_lru_wx, o_lru_bx, o_lru_lambda, o_ml_conv_w, o_ml_conv_b, o_ml_wq, o_ml_wk, o_ml_bi, o_ml_bf, o_ml_norm, o_w_out, final_norm):
    bsz, seq, d = x.shape
    depth = ffn1_norm.shape[0]
    h = x
    for layer in range(depth):
        j = layer // 2
        h = _ffn(h.reshape(bsz * seq, d), ffn1_norm[layer], ffn1_wgu, ffn1_wd, layer,
                 final_norm, False).reshape(bsz, seq, d)
        if layer % 2 == 0:
            h = _even_mixer(h, mix_norm[layer], e_w_in[j], e_w_lr_up[j], e_b_lr[j], e_head_norm[j],
                            e_w_out[j], MIX_TS)
        else:
            h = _odd_mixer(h, mix_norm[layer], o_w_in[j], o_lru_conv_w[j], o_lru_conv_b[j],
                           o_lru_wa[j], o_lru_ba[j], o_lru_wx[j], o_lru_bx[j], o_lru_lambda[j],
                           o_ml_conv_w[j], o_ml_conv_b[j], o_ml_wq[j], o_ml_wk[j], o_ml_bi[j], o_ml_bf[j],
                           o_ml_norm[j], o_w_out[j], MIX_TS)
        h = _ffn(h.reshape(bsz * seq, d), ffn2_norm[layer], ffn2_wgu, ffn2_wd, layer,
                 final_norm, layer == depth - 1).reshape(bsz, seq, d)
    return h
```

```python
import functools

import numpy as np
import jax
import jax.numpy as jnp
from jax import lax
from jax.experimental import pallas as pl
from jax.experimental.pallas import tpu as pltpu

F32 = jnp.float32
BF16 = jnp.bfloat16

D_MODEL = 1024
D_FF = 2816
GROUP_W = D_MODEL // 2
EPS = 1e-6

RET_HEADS = 4
RET_DK = 128
RET_CHUNK = 128
ROPE_BASE = 10000.0

GLA_HEADS = 4
GLA_DK = 64
GLA_DV = 128
GLA_RANK = 16
GLA_TAU = 16.0
GLA_CHUNK = 64
GLA_QK = GLA_HEADS * GLA_DK

LRU_W = GROUP_W
LRU_BLOCKS = 8
LRU_BS = LRU_W // LRU_BLOCKS
LRU_C = 8.0
CONV_W = 4

ML_HEADS = 4
ML_DH = 128
ML_CHUNK = 128

LANE = 128
SUBLANE = 8
VMEM_LIMIT = 56 * 1024 * 1024

E_RQ, E_RK, E_RV, E_RG = 0, 512, 1024, 1536
E_GQ, E_GK, E_GV, E_GG, E_LR = 2048, 2304, 2560, 3072, 3584
E_NP = 3712
O_LY, O_LX, O_MU, O_MV, O_MO, O_IF = 0, 512, 1024, 1536, 2048, 2560
O_NP = 2688

FFN_TM = 1024
FFN_SUB = 256
FFN_FC = 256
FFN_WCOLS = 512
FFN_WROWS = 256
MIX_TS = 256
EVEN_G = 4
ODD_G = 2
MIX_SLAB = 512
EVEN_LAG = 0
ODD_LAG = 6
PAIR = 2 * LANE


def _dot(a, b):
    return jnp.dot(a, b, preferred_element_type=F32)


def _dot_nt(a, b):
    return lax.dot_general(a, b, (((1,), (1,)), ((), ())), preferred_element_type=F32)


def _dot_tn(a, b):
    return lax.dot_general(a, b, (((0,), (0,)), ((), ())), preferred_element_type=F32)


def _rms(x):
    return x * lax.rsqrt(jnp.mean(x * x, axis=-1, keepdims=True) + EPS)


def _sigmoid(x):
    return 1.0 / (1.0 + jnp.exp(-x))


def _silu(x):
    return x * _sigmoid(x)


def _log_sigmoid(x):
    return jnp.minimum(x, 0.0) - jnp.log1p(jnp.exp(-jnp.abs(x)))


def _split_bf16(x):
    hi = x.astype(BF16)
    lo = (x - hi.astype(F32)).astype(BF16)
    return hi, lo


def _bdiag2(a, b):
    z = jnp.zeros_like(a)
    return jnp.concatenate([jnp.concatenate([a, z], axis=1), jnp.concatenate([z, b], axis=1)], axis=0)


def _round_robin(gens, lag=0):
    gens = list(gens)
    done = [False] * len(gens)
    rnd = 0
    while not all(done):
        for i, gen in enumerate(gens):
            if done[i] or rnd < lag * i:
                continue
            try:
                next(gen)
            except StopIteration:
                done[i] = True
        rnd += 1
        yield


def _run(gen):
    for _ in gen:
        pass


def _stage_weight(layer, w_hbm, w_bf, stage, sem, axis, width):
    n = w_bf.shape[axis] // width

    def window(j):
        sl = pl.ds(j * width, width)
        return (sl, slice(None)) if axis == 0 else (slice(None), sl)

    def copy(j):
        return pltpu.make_async_copy(w_hbm.at[(layer,) + window(j)], stage.at[j % 2], sem.at[j % 2])

    copy(0).start()
    for j in range(n):
        if j + 1 < n:
            copy(j + 1).start()
        copy(j).wait()
        w_bf[window(j)] = stage[j % 2].astype(BF16)


def _ffn_body(final, layer, x_ref, nw_ref, wgu_hbm, wd_hbm, fw_ref, o_ref,
              act_ref, wgu_ref, wd_ref, stage_gu, stage_d, sem_gu, sem_d):
    @pl.when(pl.program_id(0) == 0)
    def _():
        _stage_weight(layer, wgu_hbm, wgu_ref, stage_gu, sem_gu, 1, FFN_WCOLS)
        _stage_weight(layer, wd_hbm, wd_ref, stage_d, sem_d, 0, FFN_WROWS)

    for r0 in range(0, FFN_TM, FFN_SUB):
        rows = pl.ds(r0, FFN_SUB)
        x = x_ref[rows, :]
        xn = (_rms(x) * nw_ref[...]).astype(BF16)
        for j in range(D_FF // FFN_FC):
            c0 = j * FFN_FC
            g = _dot(xn, wgu_ref[:, c0:c0 + FFN_FC])
            u = _dot(xn, wgu_ref[:, D_FF + c0:D_FF + c0 + FFN_FC])
            act_ref[rows, c0:c0 + FFN_FC] = (_silu(g) * u).astype(BF16)
        h = x + 0.5 * _dot(act_ref[rows, :], wd_ref[...])
        if final:
            h = _rms(h) * fw_ref[...]
        o_ref[rows, :] = h


def _ffn(h, norm_w, wgu_all, wd_all, layer, final_w, final):
    t = h.shape[0]
    const = lambda shape: pl.BlockSpec(shape, lambda i: (0, 0), pipeline_mode=pl.Buffered(1))
    return pl.pallas_call(
        functools.partial(_ffn_body, final, layer),
        grid=(t // FFN_TM,),
        in_specs=[
            pl.BlockSpec((FFN_TM, D_MODEL), lambda i: (i, 0)),
            const((1, D_MODEL)),
            pl.BlockSpec(memory_space=pl.ANY),
            pl.BlockSpec(memory_space=pl.ANY),
            const((1, D_MODEL)),
        ],
        out_specs=pl.BlockSpec((FFN_TM, D_MODEL), lambda i: (i, 0)),
        out_shape=jax.ShapeDtypeStruct((t, D_MODEL), F32),
        scratch_shapes=[
            pltpu.VMEM((FFN_TM, D_FF), BF16),
            pltpu.VMEM((D_MODEL, 2 * D_FF), BF16),
            pltpu.VMEM((D_FF, D_MODEL), BF16),
            pltpu.VMEM((2, D_MODEL, FFN_WCOLS), F32),
            pltpu.VMEM((2, FFN_WROWS, D_MODEL), F32),
            pltpu.SemaphoreType.DMA((2,)),
            pltpu.SemaphoreType.DMA((2,)),
        ],
        compiler_params=pltpu.CompilerParams(
            dimension_semantics=("arbitrary",), vmem_limit_bytes=VMEM_LIMIT),
        name="ffn_final" if final else "ffn",
    )(h, norm_w.reshape(1, -1), wgu_all, wd_all, final_w.reshape(1, -1))


def _even_body(layer, x_ref, nw_ref, win_hbm, wtail_ref, wlr_ref, blr_ref, cos_ref, sin_ref,
               dmat_ref, winb_ref, wendb_ref, dec_ref, tri_ref, ctile_ref,
               hnw_ref, wout_hbm, o_ref,
               proj, labuf, obuf, ret_s, gla_st, win_ref, wout_ref, wstage, wsem):
    groups = x_ref.shape[0]
    ts = x_ref.shape[1]

    @pl.when((pl.program_id(0) == 0) & (pl.program_id(1) == 0))
    def _():
        _stage_weight(layer, win_hbm, win_ref, wstage, wsem, 1, MIX_SLAB)
        win_ref[:, E_LR:E_NP] = wtail_ref[...]
        _stage_weight(layer, wout_hbm, wout_ref, wstage, wsem, 1, MIX_SLAB)

    @pl.when(pl.program_id(1) == 0)
    def _():
        ret_s[...] = jnp.zeros_like(ret_s)
        gla_st[...] = jnp.zeros_like(gla_st)

    def sequence(g):
        x = x_ref[g]
        pg = proj.at[g]
        hn = (_rms(x) * nw_ref[...]).astype(BF16)
        for c0 in range(0, E_NP, MIX_SLAB):
            c1 = min(c0 + MIX_SLAB, E_NP)
            pg[:, c0:c1] = _dot(hn, win_ref[:, c0:c1])
            yield
        glr = pg[:, E_LR:E_LR + LANE].astype(BF16)
        labuf[g] = _log_sigmoid(_dot(glr, wlr_ref[...]) + blr_ref[...]) * (1.0 / GLA_TAU)
        cos = cos_ref[...]
        sin = sin_ref[...]
        for c0 in range(0, 2 * RET_HEADS * RET_DK, RET_DK):
            xx = pg[:, c0:c0 + RET_DK]
            r = xx * cos + pltpu.roll(xx, RET_DK // 2, 1) * sin
            if c0 >= E_RK:
                r = r * (RET_DK ** -0.5)
            pg[:, c0:c0 + RET_DK] = r
        yield
        yield from _round_robin([
            _ret_chain(pg, obuf.at[g], ret_s.at[g], dmat_ref, winb_ref, wendb_ref, dec_ref, ts),
            _gla_chain(pg, labuf.at[g], obuf.at[g], gla_st.at[g], tri_ref, ctile_ref, ts)])
        o_ref[g] = x + _dot((obuf[g] * hnw_ref[...]).astype(BF16), wout_ref[...])
        yield

    _run(_round_robin([sequence(g) for g in range(groups)], lag=EVEN_LAG))


def _ret_chain(proj, obuf, ret_s, dmat_ref, winb_ref, wendb_ref, dec_ref, ts):
    for c in range(ts // RET_CHUNK):
        rows = pl.ds(c * RET_CHUNK, RET_CHUNK)
        stage = []
        for pr in range(RET_HEADS // 2):
            lo = pr * PAIR
            q2 = proj[rows, E_RQ + lo:E_RQ + lo + PAIR].astype(BF16)
            k2 = proj[rows, E_RK + lo:E_RK + lo + PAIR].astype(BF16)
            v2 = proj[rows, E_RV + lo:E_RV + lo + PAIR]
            s0 = ret_s[2 * pr]
            s1 = ret_s[2 * pr + 1]
            s_cat = _dot_nt(q2, _bdiag2(k2[:, :LANE], k2[:, LANE:]))
            inter = _dot(q2, _bdiag2(s0.astype(BF16), s1.astype(BF16)))
            full = _dot_tn(k2, (wendb_ref[pr] * v2).astype(BF16))
            ret_s[2 * pr] = dec_ref[2 * pr] * s0 + full[:LANE, :LANE]
            ret_s[2 * pr + 1] = dec_ref[2 * pr + 1] * s1 + full[LANE:, LANE:]
            stage.append((s_cat, inter, v2.astype(BF16)))
        yield
        outs = []
        for pr in range(RET_HEADS // 2):
            s_cat, inter, vb = stage[pr]
            p = (s_cat * dmat_ref[pr]).astype(BF16)
            outs.append(_dot(p, _bdiag2(vb[:, :LANE], vb[:, LANE:])) + inter * winb_ref[pr])
        yield
        for h in range(RET_HEADS):
            o = outs[h // 2][:, (h % 2) * LANE:(h % 2 + 1) * LANE]
            g = proj[rows, E_RG + h * RET_DK:E_RG + (h + 1) * RET_DK]
            obuf[rows, h * RET_DK:(h + 1) * RET_DK] = _rms(o) * _silu(g)
        yield


def _gla_chain(proj, labuf, obuf, gla_st, tri_ref, ctile_ref, ts):
    lane_head = lax.broadcasted_iota(jnp.int32, (1, GLA_QK), 1) // GLA_DK

    def stack_heads(z):
        return jnp.concatenate(
            [jnp.where(lane_head == h, z, 0.0) for h in range(GLA_HEADS)], axis=0).astype(BF16)

    zb = jnp.zeros((GLA_CHUNK, GLA_DV), BF16)
    for c in range(ts // GLA_CHUNK):
        rows = pl.ds(c * GLA_CHUNK, GLA_CHUNK)
        la_hi, la_lo = _split_bf16(labuf[rows, :])
        tri = tri_ref[...]
        b = _dot(tri, la_hi) + _dot(tri, la_lo)
        yield
        b_mid = b[GLA_CHUNK // 2 - 1:GLA_CHUNK // 2, :]
        b_end = b[GLA_CHUNK - 1:GLA_CHUNK, :]
        q = proj[rows, E_GQ:E_GQ + GLA_QK]
        k = proj[rows, E_GK:E_GK + GLA_QK] * (GLA_DK ** -0.5)
        vb = proj[rows, E_GV:E_GV + GLA_HEADS * GLA_DV].astype(BF16)
        vh = [vb[:, h * GLA_DV:(h + 1) * GLA_DV] for h in range(GLA_HEADS)]
        s_cat = _dot_nt((q * jnp.exp(b - b_mid)).astype(BF16), stack_heads(k * jnp.exp(b_mid - b)))
        st = gla_st[...]
        inter = _dot_nt(stack_heads(q * jnp.exp(b)), st.astype(BF16))
        upd = _dot_tn(jnp.concatenate(vh, axis=0), stack_heads(k * jnp.exp(b_end - b)))
        gla_st[...] = jnp.exp(b_end) * st + upd
        yield
        p = (s_cat * ctile_ref[...]).astype(BF16)
        vbd = jnp.concatenate(
            [jnp.concatenate([vh[h] if j == h else zb for j in range(GLA_HEADS)], axis=1)
             for h in range(GLA_HEADS)], axis=0)
        out_cat = _dot(p, vbd)
        yield
        for h in range(GLA_HEADS):
            g = proj[rows, E_GG + h * GLA_DV:E_GG + (h + 1) * GLA_DV]
            o = out_cat[:, h * GLA_DV:(h + 1) * GLA_DV] + inter[h * GLA_CHUNK:(h + 1) * GLA_CHUNK, :]
            obuf[rows, GROUP_W + h * GLA_DV:GROUP_W + (h + 1) * GLA_DV] = _rms(o) * _silu(g)
        yield


def _pair_lanes(t):
    return jnp.concatenate([t[0::2], t[1::2]], axis=-1)


def _even_mixer(h, norm_w, w_in_all, w_lr, b_lr, head_norm, w_out_all, layer, ts, grp):
    bsz, seq, _ = h.shape
    hh = jnp.arange(RET_HEADS, dtype=F32)
    log_g = jnp.log1p(-jnp.exp2(-5.0 - hh))
    idx = jnp.arange(RET_CHUNK, dtype=F32)
    rel = idx[:, None] - idx[None, :]
    causal = rel >= 0
    dmat = jnp.where(causal, jnp.exp(log_g[:, None, None] * jnp.where(causal, rel, 0.0)), 0.0)
    full = (RET_HEADS, RET_CHUNK, RET_DK)
    pair = (RET_HEADS // 2, RET_CHUNK, PAIR)
    w_end = jnp.broadcast_to(jnp.exp(log_g[:, None] * (RET_CHUNK - 1.0 - idx)[None, :])[:, :, None], full)
    w_inb = jnp.broadcast_to(jnp.exp(log_g[:, None] * (idx + 1.0)[None, :])[:, :, None], full)
    dec = jnp.broadcast_to(jnp.exp(log_g * RET_CHUNK)[:, None, None], full)

    half = RET_DK // 2
    inv = ROPE_BASE ** (-jnp.arange(half, dtype=F32) / half)
    ang = jnp.arange(seq).astype(F32)[:, None] * inv[None, :]
    cos2 = jnp.concatenate([jnp.cos(ang), jnp.cos(ang)], axis=-1)
    sin2 = jnp.concatenate([-jnp.sin(ang), jnp.sin(ang)], axis=-1)

    tril = jnp.tril(jnp.ones((GLA_CHUNK, GLA_CHUNK), F32))
    ctile = jnp.tile(tril, (1, GLA_HEADS))

    w_tail = w_in_all[layer][:, E_LR:]
    w_tail = jnp.pad(w_tail, ((0, 0), (0, E_NP - E_LR - w_tail.shape[1]))).astype(BF16)
    w_lr_p = jnp.pad(w_lr, ((0, LANE - GLA_RANK), (0, 0))).astype(BF16)
    any_space = pl.BlockSpec(memory_space=pl.ANY)

    const2 = lambda shape: pl.BlockSpec(shape, lambda b, s: (0, 0), pipeline_mode=pl.Buffered(1))
    const3 = lambda shape: pl.BlockSpec(shape, lambda b, s: (0, 0, 0), pipeline_mode=pl.Buffered(1))
    return pl.pallas_call(
        functools.partial(_even_body, layer),
        grid=(bsz // grp, seq // ts),
        in_specs=[
            pl.BlockSpec((grp, ts, D_MODEL), lambda b, s: (b, s, 0)),
            const2((1, D_MODEL)),
            any_space,
            const2((D_MODEL, LANE)),
            const2((LANE, GLA_QK)),
            const2((1, GLA_QK)),
            pl.BlockSpec((ts, RET_DK), lambda b, s: (s, 0)),
            pl.BlockSpec((ts, RET_DK), lambda b, s: (s, 0)),
            const3(pair), const3(pair), const3(pair), const3(full),
            const2((GLA_CHUNK, GLA_CHUNK)),
            const2((GLA_CHUNK, GLA_QK)),
            const2((1, D_MODEL)),
            any_space,
        ],
        out_specs=pl.BlockSpec((grp, ts, D_MODEL), lambda b, s: (b, s, 0)),
        out_shape=jax.ShapeDtypeStruct(h.shape, F32),
        scratch_shapes=[
            pltpu.VMEM((grp, ts, E_NP), F32),
            pltpu.VMEM((grp, ts, GLA_QK), F32),
            pltpu.VMEM((grp, ts, D_MODEL), F32),
            pltpu.VMEM((grp,) + full, F32),
            pltpu.VMEM((grp, GLA_DV, GLA_QK), F32),
            pltpu.VMEM((D_MODEL, E_NP), BF16),
            pltpu.VMEM((D_MODEL, D_MODEL), BF16),
            pltpu.VMEM((2, D_MODEL, MIX_SLAB), F32),
            pltpu.SemaphoreType.DMA((2,)),
        ],
        compiler_params=pltpu.CompilerParams(
            dimension_semantics=("arbitrary", "arbitrary"), vmem_limit_bytes=VMEM_LIMIT),
        name="even_mixer",
    )(h, norm_w.reshape(1, -1), w_in_all, w_tail, w_lr_p, b_lr.reshape(1, -1), cos2, sin2,
      _pair_lanes(dmat), _pair_lanes(w_inb), _pair_lanes(w_end), dec, tril.astype(BF16), ctile,
      head_norm.reshape(1, -1), w_out_all)


def _tiles(ref, t0, n, rows=slice(None)):
    return jnp.concatenate([ref[t0 + t, rows, :] for t in range(n)], axis=1)


def _set_tiles(ref, t0, val):
    for t in range(val.shape[1] // LANE):
        ref[t0 + t] = val[:, t * LANE:(t + 1) * LANE]


def _permute_rows(perm_ref, x_bf16):
    return _dot(perm_ref[...], x_bf16)


def _causal_conv(tail, xb, w_ref, b_ref):
    ts = xb.shape[0]
    keep = (CONV_W - 1) * SUBLANE
    last = xb[ts - keep:, :]
    prev = tail[...]
    first = lax.broadcasted_iota(jnp.int32, (SUBLANE, 1), 0) == 0
    fix = [jnp.where(first, pltpu.roll(prev[m * SUBLANE:(m + 1) * SUBLANE, :], 1, 0),
                     pltpu.roll(last[m * SUBLANE:(m + 1) * SUBLANE, :], 1, 0)) for m in range(CONV_W - 1)]
    tail[...] = last
    xpad = jnp.concatenate(fix + [xb], axis=0)
    acc = b_ref[...] + w_ref[CONV_W - 1:CONV_W, :] * xb
    for j in range(CONV_W - 1):
        acc = acc + w_ref[j:j + 1, :] * xpad[j * SUBLANE:j * SUBLANE + ts, :]
    return acc


def _odd_body(layer, x_ref, nw_ref, win_hbm, wtail_ref, lcw_ref, lcb_ref, mcw_ref, mcb_ref,
              wa_ref, ba_ref, wx_ref, bx_ref, lam_ref, wqk_ref, bif_ref, tri_ref,
              mnw_ref, wout_hbm, perm_ref, unperm_ref, o_ref,
              proj, ltail, mtail, qk, ibuf, fbuf, obuf, lru_h, ml_c, ml_m, win_ref, wout_ref, wstage, wsem):
    groups = x_ref.shape[0]
    ts = x_ref.shape[1]

    @pl.when((pl.program_id(0) == 0) & (pl.program_id(1) == 0))
    def _():
        _stage_weight(layer, win_hbm, win_ref, wstage, wsem, 1, MIX_SLAB)
        win_ref[:, O_IF:O_NP] = wtail_ref[...]
        _stage_weight(layer, wout_hbm, wout_ref, wstage, wsem, 1, MIX_SLAB)

    @pl.when(pl.program_id(1) == 0)
    def _():
        ltail[...] = jnp.zeros_like(ltail)
        mtail[...] = jnp.zeros_like(mtail)
        lru_h[...] = jnp.zeros_like(lru_h)
        ml_c[...] = jnp.zeros_like(ml_c)
        ml_m[...] = jnp.zeros_like(ml_m)

    def sequence(g):
        x = x_ref[g]
        hn = (_rms(x) * nw_ref[...]).astype(BF16)
        hn_blocked = _permute_rows(perm_ref, hn).astype(BF16)
        for c0 in range(0, O_NP, 512):
            c1 = min(c0 + 512, O_NP)
            lhs = hn_blocked if c1 <= O_MV else hn
            _set_tiles(proj.at[g], c0 // LANE, _dot(lhs, win_ref[:, c0:c1]))
            yield
        yield from _round_robin([
            _lru_chain(proj.at[g], ltail.at[g], obuf.at[g], lru_h.at[g],
                       lcw_ref, lcb_ref, wa_ref, ba_ref, wx_ref, bx_ref, lam_ref, unperm_ref, ts),
            _mlstm_chain(proj.at[g], mtail.at[g], qk.at[g], ibuf.at[g], fbuf.at[g], obuf.at[g],
                         ml_c.at[g], ml_m.at[g], mcw_ref, mcb_ref, wqk_ref, bif_ref, tri_ref, mnw_ref,
                         unperm_ref, ts)])
        o_ref[g] = x + _dot(_tiles(obuf.at[g], 0, D_MODEL // LANE).astype(BF16), wout_ref[...])
        yield

    _run(_round_robin([sequence(g) for g in range(groups)], lag=ODD_LAG))


def _lru_chain(proj, ltail, obuf, lru_h, lcw_ref, lcb_ref, wa_ref, ba_ref, wx_ref, bx_ref, lam_ref, unperm_ref, ts):
    lxc = _causal_conv(ltail, _tiles(proj, O_LX // LANE, LRU_W // LANE), lcw_ref, lcb_ref)
    xb = lxc.astype(BF16)
    r = _sigmoid(_dot(xb, wa_ref[...]) + ba_ref[...])
    i = _sigmoid(_dot(xb, wx_ref[...]) + bx_ref[...])
    yield
    lam = lam_ref[...]
    softplus_neg_lam = jnp.maximum(-lam, 0.0) + jnp.log1p(jnp.exp(-jnp.abs(lam)))
    log_a = (-LRU_C) * r * softplus_neg_lam
    a = jnp.exp(log_a)
    u = jnp.sqrt(-jnp.tanh(log_a) * (a * a + 1.0)) * (i * lxc)
    n = ts // SUBLANE
    hs, ps = [u[0:SUBLANE, :]], [a[0:SUBLANE, :]]
    for j in range(1, n):
        aj = a[j * SUBLANE:(j + 1) * SUBLANE, :]
        hs.append(aj * hs[-1] + u[j * SUBLANE:(j + 1) * SUBLANE, :])
        ps.append(aj * ps[-1])
    sub = lax.broadcasted_iota(jnp.int32, (SUBLANE, 1), 0)
    pt, ht = ps[-1], hs[-1]
    d = 1
    while d < SUBLANE:
        keep = sub >= d
        p_s = jnp.where(keep, pltpu.roll(pt, d, 0), 1.0)
        h_s = jnp.where(keep, pltpu.roll(ht, d, 0), 0.0)
        ht = pt * h_s + ht
        pt = pt * p_s
        d *= 2
    hprev = lru_h[0:1, :]
    block_end = ht + pt * hprev
    carry = jnp.where(sub == 0, hprev, pltpu.roll(block_end, 1, 0))
    lru_h[0:1, :] = block_end[SUBLANE - 1:SUBLANE, :]
    hfull = jnp.concatenate([hs[j] + ps[j] * carry for j in range(n)], axis=0)
    ly = _tiles(proj, O_LY // LANE, LRU_W // LANE)
    gelu = ly * (0.5 * (1.0 + jnp.tanh(np.float32(np.sqrt(2.0 / np.pi)) * (ly + 0.044715 * (ly * ly * ly)))))
    _set_tiles(obuf, 0, _permute_rows(unperm_ref, (hfull * gelu).astype(BF16)))
    yield


def _mlstm_chain(proj, mtail, qk, ibuf, fbuf, obuf, ml_c, ml_m, mcw_ref, mcb_ref, wqk_ref, bif_ref, tri_ref,
                 mnw_ref, unperm_ref, ts):
    mc_blocked = _silu(_causal_conv(mtail, _tiles(proj, O_MU // LANE, GROUP_W // LANE), mcw_ref, mcb_ref))
    mc = _permute_rows(unperm_ref, mc_blocked.astype(BF16)).astype(BF16)
    for h in range(ML_HEADS):
        qkh = _dot(mc[:, h * ML_DH:(h + 1) * ML_DH], wqk_ref[h])
        qk[h] = qkh[:, :ML_DH]
        qk[ML_HEADS + h] = qkh[:, ML_DH:] * (ML_DH ** -0.5)
    gates = proj[O_IF // LANE] + bif_ref[...]
    ibuf[...] = gates
    fbuf[...] = pltpu.roll(_log_sigmoid(gates), LANE - ML_HEADS, 1)
    yield

    row_i = lax.broadcasted_iota(jnp.int32, (ML_CHUNK, ML_CHUNK), 0)
    col_i = lax.broadcasted_iota(jnp.int32, (ML_CHUNK, ML_CHUNK), 1)
    causal = row_i >= col_i
    ones_col = jnp.where(col_i == 0, 1.0, 0.0).astype(BF16)

    for c in range(ts // ML_CHUNK):
        rows = pl.ds(c * ML_CHUNK, ML_CHUNK)
        f_hi, f_lo = _split_bf16(fbuf[rows, :])
        tri = tri_ref[...]
        fc = _dot(tri, f_hi) + _dot(tri, f_lo)
        yield
        dm = ibuf[rows, :] - fc
        dm_t = dm.T
        b_end = fc[ML_CHUNK - 1:ML_CHUNK, :]
        a_col = b_end + dm
        a_max = jnp.max(a_col, axis=0, keepdims=True)
        w_end = jnp.exp(a_col - a_max)
        m_old = ml_m[0:1, :]
        m_new = jnp.maximum(b_end + m_old, a_max)
        g_old = jnp.exp(b_end + m_old - m_new)
        g_new = jnp.exp(a_max - m_new)
        ml_m[0:1, :] = m_new
        s_cat, qc, vh = [], [], []
        for pr in range(ML_HEADS // 2):
            q2 = _tiles(qk, 2 * pr, 2, rows).astype(BF16)
            k2f = _tiles(qk, ML_HEADS + 2 * pr, 2, rows)
            k2 = k2f.astype(BF16)
            s_cat.append(_dot_nt(q2, _bdiag2(k2[:, :LANE], k2[:, LANE:])))
            for j in range(2):
                h = 2 * pr + j
                v = proj[O_MV // LANE + h, rows, :].astype(BF16)
                cx = ml_c[h]
                qc.append(_dot(q2[:, j * LANE:(j + 1) * LANE], cx.astype(BF16)))
                wk = (w_end[:, h:h + 1] * k2f[:, j * LANE:(j + 1) * LANE]).astype(BF16)
                updx = _dot_tn(wk, jnp.concatenate([v, ones_col], axis=1))
                ml_c[h] = g_old[:, h:h + 1] * cx + g_new[:, h:h + 1] * updx
                vh.append(v)
        yield
        mxs, sms, dens, nums = [], [], [], []
        for h in range(ML_HEADS):
            d_row = dm_t[h:h + 1, :]
            m_h = m_old[:, h:h + 1]
            cm = jnp.max(jnp.where(causal, d_row, -jnp.inf), axis=1, keepdims=True)
            mx = jnp.maximum(cm, m_h)
            p = jnp.where(causal, jnp.exp(d_row - mx), 0.0)
            s = s_cat[h // 2][:, (h % 2) * LANE:(h % 2 + 1) * LANE] * p
            mxs.append(mx)
            sms.append(s.astype(BF16))
            dens.append(jnp.sum(s, axis=1, keepdims=True))
        for pr in range(ML_HEADS // 2):
            nums.append(_dot(jnp.concatenate([sms[2 * pr], sms[2 * pr + 1]], axis=1),
                             _bdiag2(vh[2 * pr], vh[2 * pr + 1])))
        yield
        for h in range(ML_HEADS):
            m_h = m_old[:, h:h + 1]
            w_inter = jnp.exp(m_h - mxs[h])
            num = nums[h // 2][:, (h % 2) * LANE:(h % 2 + 1) * LANE] + w_inter * qc[h][:, :ML_DH]
            den = dens[h] + w_inter * qc[h][:, ML_DH:ML_DH + 1]
            m_t = fc[:, h:h + 1] + mxs[h]
            hh = num / jnp.maximum(jnp.abs(den), jnp.exp(-m_t))
            og = _sigmoid(proj[O_MO // LANE + h, rows, :])
            obuf[GROUP_W // LANE + h, rows, :] = _rms(og * hh) * mnw_ref[:, h * ML_DH:(h + 1) * ML_DH]
        yield


def _block_diag(w):
    g, n, _ = w.shape
    eye = jnp.eye(g, dtype=w.dtype)
    return (eye[:, None, :, None] * w[:, :, None, :]).reshape(g * n, g * n)


def _odd_mixer(h, norm_w, w_in_all, lcw, lcb, wa, ba, wx, bx, lam, mcw, mcb, wq, wk, bi, bf, ml_norm, w_out_all,
               layer, ts, grp):
    bsz, seq, _ = h.shape
    w_tail = w_in_all[layer][:, O_IF:]
    w_tail = jnp.pad(w_tail, ((0, 0), (0, O_NP - O_IF - w_tail.shape[1]))).astype(BF16)
    any_space = pl.BlockSpec(memory_space=pl.ANY)
    wqk = jnp.concatenate([wq, wk], axis=-1).astype(BF16)
    bif = jnp.pad(jnp.concatenate([bi, bf]), (0, LANE - 2 * ML_HEADS)).reshape(1, LANE)
    tri = jnp.tril(jnp.ones((ML_CHUNK, ML_CHUNK), F32)).astype(BF16)
    r = jnp.arange(ts)
    time_of_row = (r % SUBLANE) * (ts // SUBLANE) + r // SUBLANE
    perm = (time_of_row[:, None] == r[None, :]).astype(BF16)
    row = lambda t: t.reshape(1, -1)

    const2 = lambda shape: pl.BlockSpec(shape, lambda b, s: (0, 0), pipeline_mode=pl.Buffered(1))
    const3 = lambda shape: pl.BlockSpec(shape, lambda b, s: (0, 0, 0), pipeline_mode=pl.Buffered(1))
    return pl.pallas_call(
        functools.partial(_odd_body, layer),
        grid=(bsz // grp, seq // ts),
        in_specs=[
            pl.BlockSpec((grp, ts, D_MODEL), lambda b, s: (b, s, 0)),
            const2((1, D_MODEL)),
            any_space,
            const2((D_MODEL, LANE)),
            const2((CONV_W, LRU_W)), const2((1, LRU_W)),
            const2((CONV_W, GROUP_W)), const2((1, GROUP_W)),
            const2((LRU_W, LRU_W)), const2((1, LRU_W)),
            const2((LRU_W, LRU_W)), const2((1, LRU_W)),
            const2((1, LRU_W)),
            const3((ML_HEADS, ML_DH, 2 * ML_DH)),
            const2((1, LANE)),
            const2((ML_CHUNK, ML_CHUNK)),
            const2((1, GROUP_W)),
            any_space,
            const2((ts, ts)), const2((ts, ts)),
        ],
        out_specs=pl.BlockSpec((grp, ts, D_MODEL), lambda b, s: (b, s, 0)),
        out_shape=jax.ShapeDtypeStruct(h.shape, F32),
        scratch_shapes=[
            pltpu.VMEM((grp, O_NP // LANE, ts, LANE), F32),
            pltpu.VMEM((grp, (CONV_W - 1) * SUBLANE, LRU_W), F32),
            pltpu.VMEM((grp, (CONV_W - 1) * SUBLANE, GROUP_W), F32),
            pltpu.VMEM((grp, 2 * GROUP_W // LANE, ts, LANE), F32),
            pltpu.VMEM((grp, ts, LANE), F32),
            pltpu.VMEM((grp, ts, LANE), F32),
            pltpu.VMEM((grp, D_MODEL // LANE, ts, LANE), F32),
            pltpu.VMEM((grp, SUBLANE, LRU_W), F32),
            pltpu.VMEM((grp, ML_HEADS, ML_DH, 2 * ML_DH), F32),
            pltpu.VMEM((grp, SUBLANE, LANE), F32),
            pltpu.VMEM((D_MODEL, O_NP), BF16),
            pltpu.VMEM((D_MODEL, D_MODEL), BF16),
            pltpu.VMEM((2, D_MODEL, MIX_SLAB), F32),
            pltpu.SemaphoreType.DMA((2,)),
        ],
        compiler_params=pltpu.CompilerParams(
            dimension_semantics=("arbitrary", "arbitrary"), vmem_limit_bytes=VMEM_LIMIT),
        name="odd_mixer",
    )(h, row(norm_w), w_in_all, w_tail, lcw, row(lcb), mcw, row(mcb),
      _block_diag(wa).astype(BF16), row(ba), _block_diag(wx).astype(BF16), row(bx), row(lam),
      wqk, bif, tri, row(ml_norm), w_out_all, perm, perm.T)


def kernel(x, ffn1_norm, ffn1_wgu, ffn1_wd, mix_norm, ffn2_norm, ffn2_wgu, ffn2_wd, e_w_in, e_w_lr_up, e_b_lr, e_head_norm, e_w_out, o_w_in, o_lru_conv_w, o_lru_conv_b, o_lru_wa, o_lru_ba, o_lru_wx, o_lru_bx, o_lru_lambda, o_ml_conv_w, o_ml_conv_b, o_ml_wq, o_ml_wk, o_ml_bi, o_ml_bf, o_ml_norm, o_w_out, final_norm):
    bsz, seq, d = x.shape
    depth = ffn1_norm.shape[0]
    h = x
    for layer in range(depth):
        j = layer // 2
        h = _ffn(h.reshape(bsz * seq, d), ffn1_norm[layer], ffn1_wgu, ffn1_wd, layer,
                 final_norm, False).reshape(bsz, seq, d)
        if layer % 2 == 0:
            h = _even_mixer(h, mix_norm[layer], e_w_in, e_w_lr_up[j], e_b_lr[j], e_head_norm[j],
                            e_w_out, j, MIX_TS, EVEN_G)
        else:
            h = _odd_mixer(h, mix_norm[layer], o_w_in, o_lru_conv_w[j], o_lru_conv_b[j],
                           o_lru_wa[j], o_lru_ba[j], o_lru_wx[j], o_lru_bx[j], o_lru_lambda[j],
                           o_ml_conv_w[j], o_ml_conv_b[j], o_ml_wq[j], o_ml_wk[j], o_ml_bi[j], o_ml_bf[j],
                           o_ml_norm[j], o_w_out, j, MIX_TS, ODD_G)
        h = _ffn(h.reshape(bsz * seq, d), ffn2_norm[layer], ffn2_wgu, ffn2_wd, layer,
                 final_norm, layer == depth - 1).reshape(bsz, seq, d)
    return h
```

```python
import functools

import numpy as np
import jax
import jax.numpy as jnp
from jax import lax
from jax.experimental import pallas as pl
from jax.experimental.pallas import tpu as pltpu

F32 = jnp.float32
BF16 = jnp.bfloat16

D_MODEL = 1024
D_FF = 2816
GROUP_W = D_MODEL // 2
EPS = 1e-6

RET_HEADS = 4
RET_DK = 128
RET_CHUNK = 128
ROPE_BASE = 10000.0

GLA_HEADS = 4
GLA_DK = 64
GLA_DV = 128
GLA_RANK = 16
GLA_TAU = 16.0
GLA_CHUNK = 64
GLA_QK = GLA_HEADS * GLA_DK

LRU_W = GROUP_W
LRU_BLOCKS = 8
LRU_BS = LRU_W // LRU_BLOCKS
LRU_C = 8.0
CONV_W = 4

ML_HEADS = 4
ML_DH = 128
ML_CHUNK = 128

LANE = 128
SUBLANE = 8
VMEM_LIMIT = 56 * 1024 * 1024

E_RQ, E_RK, E_RV, E_RG = 0, 512, 1024, 1536
E_GQ, E_GK, E_GV, E_GG, E_LR = 2048, 2304, 2560, 3072, 3584
E_NP = 3712
O_LY, O_LX, O_MU, O_MV, O_MO, O_IF = 0, 512, 1024, 1536, 2048, 2560
O_NP = 2688

FFN_TM = 1024
FFN_SUB = 256
FFN_FC = 256
FFN_GU_ROWS = 64
FFN_D_ROWS = 256
STAGE_DEPTH = 4
MIX_TS = 256
EVEN_G = 4
ODD_G = 2
MIX_SLAB = 512
OUT_SLAB = 256
EVEN_LAG = 0
ODD_LAG = 6
PAIR = 2 * LANE


def _dot(a, b):
    return jnp.dot(a, b, preferred_element_type=F32)


def _dot_nt(a, b):
    return lax.dot_general(a, b, (((1,), (1,)), ((), ())), preferred_element_type=F32)


def _dot_tn(a, b):
    return lax.dot_general(a, b, (((0,), (0,)), ((), ())), preferred_element_type=F32)


def _rms(x):
    return x * lax.rsqrt(jnp.mean(x * x, axis=-1, keepdims=True) + EPS)


def _sigmoid(x):
    return 1.0 / (1.0 + jnp.exp(-x))


def _silu(x):
    return x * _sigmoid(x)


def _log_sigmoid(x):
    return jnp.minimum(x, 0.0) - jnp.log1p(jnp.exp(-jnp.abs(x)))


def _split_bf16(x):
    hi = x.astype(BF16)
    lo = (x - hi.astype(F32)).astype(BF16)
    return hi, lo


def _bdiag2(a, b):
    z = jnp.zeros_like(a)
    return jnp.concatenate([jnp.concatenate([a, z], axis=1), jnp.concatenate([z, b], axis=1)], axis=0)


def _round_robin(gens, lag=0):
    gens = list(gens)
    done = [False] * len(gens)
    rnd = 0
    while not all(done):
        for i, gen in enumerate(gens):
            if done[i] or rnd < lag * i:
                continue
            try:
                next(gen)
            except StopIteration:
                done[i] = True
        rnd += 1
        yield


def _run(gen):
    for _ in gen:
        pass


def _stage_weight(layer, w_hbm, w_bf, stage, sem, axis, width):
    n = w_bf.shape[axis] // width
    depth = stage.shape[0]

    def window(j):
        sl = pl.ds(j * width, width)
        return (sl, slice(None)) if axis == 0 else (slice(None), sl)

    def copy(j):
        return pltpu.make_async_copy(w_hbm.at[(layer,) + window(j)], stage.at[j % depth], sem.at[j % depth])

    for j in range(min(depth - 1, n)):
        copy(j).start()
    for j in range(n):
        if j + depth - 1 < n:
            copy(j + depth - 1).start()
        copy(j).wait()
        w_bf[window(j)] = stage[j % depth].astype(BF16)


def _ffn_body(final, layer, x_ref, nw_ref, wgu_hbm, wd_hbm, fw_ref, o_ref,
              act_ref, wgu_ref, wd_ref, stage_gu, stage_d, sem_gu, sem_d):
    @pl.when(pl.program_id(0) == 0)
    def _():
        _stage_weight(layer, wgu_hbm, wgu_ref, stage_gu, sem_gu, 0, FFN_GU_ROWS)
        _stage_weight(layer, wd_hbm, wd_ref, stage_d, sem_d, 0, FFN_D_ROWS)

    for r0 in range(0, FFN_TM, FFN_SUB):
        rows = pl.ds(r0, FFN_SUB)
        x = x_ref[rows, :]
        xn = (_rms(x) * nw_ref[...]).astype(BF16)
        for j in range(D_FF // FFN_FC):
            c0 = j * FFN_FC
            g = _dot(xn, wgu_ref[:, c0:c0 + FFN_FC])
            u = _dot(xn, wgu_ref[:, D_FF + c0:D_FF + c0 + FFN_FC])
            act_ref[rows, c0:c0 + FFN_FC] = (_silu(g) * u).astype(BF16)
        h = x + 0.5 * _dot(act_ref[rows, :], wd_ref[...])
        if final:
            h = _rms(h) * fw_ref[...]
        o_ref[rows, :] = h


def _ffn(h, norm_w, wgu_all, wd_all, layer, final_w, final):
    t = h.shape[0]
    const = lambda shape: pl.BlockSpec(shape, lambda i: (0, 0), pipeline_mode=pl.Buffered(1))
    return pl.pallas_call(
        functools.partial(_ffn_body, final, layer),
        grid=(t // FFN_TM,),
        in_specs=[
            pl.BlockSpec((FFN_TM, D_MODEL), lambda i: (i, 0)),
            const((1, D_MODEL)),
            pl.BlockSpec(memory_space=pl.ANY),
            pl.BlockSpec(memory_space=pl.ANY),
            const((1, D_MODEL)),
        ],
        out_specs=pl.BlockSpec((FFN_TM, D_MODEL), lambda i: (i, 0)),
        out_shape=jax.ShapeDtypeStruct((t, D_MODEL), F32),
        scratch_shapes=[
            pltpu.VMEM((FFN_TM, D_FF), BF16),
            pltpu.VMEM((D_MODEL, 2 * D_FF), BF16),
            pltpu.VMEM((D_FF, D_MODEL), BF16),
            pltpu.VMEM((STAGE_DEPTH, FFN_GU_ROWS, 2 * D_FF), F32),
            pltpu.VMEM((STAGE_DEPTH, FFN_D_ROWS, D_MODEL), F32),
            pltpu.SemaphoreType.DMA((STAGE_DEPTH,)),
            pltpu.SemaphoreType.DMA((STAGE_DEPTH,)),
        ],
        compiler_params=pltpu.CompilerParams(
            dimension_semantics=("arbitrary",), vmem_limit_bytes=VMEM_LIMIT),
        name="ffn_final" if final else "ffn",
    )(h, norm_w.reshape(1, -1), wgu_all, wd_all, final_w.reshape(1, -1))


def _even_body(layer, x_ref, nw_ref, win_hbm, wtail_ref, wlr_ref, blr_ref, cos_ref, sin_ref,
               dmat_ref, winb_ref, wendb_ref, dec_ref, tri_ref, ctile_ref,
               hnw_ref, wout_hbm, o_ref,
               proj, labuf, obuf, ret_s, gla_st, win_ref, wout_ref, wstage, wsem):
    groups = x_ref.shape[0]
    ts = x_ref.shape[1]

    @pl.when((pl.program_id(0) == 0) & (pl.program_id(1) == 0))
    def _():
        _stage_weight(layer, win_hbm, win_ref, wstage, wsem, 1, MIX_SLAB)
        win_ref[:, E_LR:E_NP] = wtail_ref[...]
        _stage_weight(layer, wout_hbm, wout_ref, wstage, wsem, 1, MIX_SLAB)

    @pl.when(pl.program_id(1) == 0)
    def _():
        ret_s[...] = jnp.zeros_like(ret_s)
        gla_st[...] = jnp.zeros_like(gla_st)

    def sequence(g):
        x = x_ref[g]
        pg = proj.at[g]
        hn = (_rms(x) * nw_ref[...]).astype(BF16)
        for c0 in range(0, E_NP, MIX_SLAB):
            c1 = min(c0 + MIX_SLAB, E_NP)
            pg[:, c0:c1] = _dot(hn, win_ref[:, c0:c1])
            yield
        glr = pg[:, E_LR:E_LR + LANE].astype(BF16)
        labuf[g] = _log_sigmoid(_dot(glr, wlr_ref[...]) + blr_ref[...]) * (1.0 / GLA_TAU)
        cos = cos_ref[...]
        sin = sin_ref[...]
        for c0 in range(0, 2 * RET_HEADS * RET_DK, RET_DK):
            xx = pg[:, c0:c0 + RET_DK]
            r = xx * cos + pltpu.roll(xx, RET_DK // 2, 1) * sin
            if c0 >= E_RK:
                r = r * (RET_DK ** -0.5)
            pg[:, c0:c0 + RET_DK] = r
        yield
        yield from _round_robin([
            _ret_chain(pg, obuf.at[g], ret_s.at[g], dmat_ref, winb_ref, wendb_ref, dec_ref, ts),
            _gla_chain(pg, labuf.at[g], obuf.at[g], gla_st.at[g], tri_ref, ctile_ref, ts)])
        ob = (obuf[g] * hnw_ref[...]).astype(BF16)
        for c0 in range(0, D_MODEL, OUT_SLAB):
            o_ref[g, :, c0:c0 + OUT_SLAB] = x[:, c0:c0 + OUT_SLAB] + _dot(ob, wout_ref[:, c0:c0 + OUT_SLAB])
            yield

    _run(_round_robin([sequence(g) for g in range(groups)], lag=EVEN_LAG))


def _ret_chain(proj, obuf, ret_s, dmat_ref, winb_ref, wendb_ref, dec_ref, ts):
    for c in range(ts // RET_CHUNK):
        rows = pl.ds(c * RET_CHUNK, RET_CHUNK)
        stage = []
        for pr in range(RET_HEADS // 2):
            lo = pr * PAIR
            q2 = proj[rows, E_RQ + lo:E_RQ + lo + PAIR].astype(BF16)
            k2 = proj[rows, E_RK + lo:E_RK + lo + PAIR].astype(BF16)
            v2 = proj[rows, E_RV + lo:E_RV + lo + PAIR]
            s0 = ret_s[2 * pr]
            s1 = ret_s[2 * pr + 1]
            s_cat = _dot_nt(q2, _bdiag2(k2[:, :LANE], k2[:, LANE:]))
            inter = _dot(q2, _bdiag2(s0.astype(BF16), s1.astype(BF16)))
            full = _dot_tn(k2, (wendb_ref[pr] * v2).astype(BF16))
            ret_s[2 * pr] = dec_ref[2 * pr] * s0 + full[:LANE, :LANE]
            ret_s[2 * pr + 1] = dec_ref[2 * pr + 1] * s1 + full[LANE:, LANE:]
            stage.append((s_cat, inter, v2.astype(BF16)))
        yield
        outs = []
        for pr in range(RET_HEADS // 2):
            s_cat, inter, vb = stage[pr]
            p = (s_cat * dmat_ref[pr]).astype(BF16)
            outs.append(_dot(p, _bdiag2(vb[:, :LANE], vb[:, LANE:])) + inter * winb_ref[pr])
        yield
        for h in range(RET_HEADS):
            o = outs[h // 2][:, (h % 2) * LANE:(h % 2 + 1) * LANE]
            g = proj[rows, E_RG + h * RET_DK:E_RG + (h + 1) * RET_DK]
            obuf[rows, h * RET_DK:(h + 1) * RET_DK] = _rms(o) * _silu(g)
        yield


def _gla_chain(proj, labuf, obuf, gla_st, tri_ref, ctile_ref, ts):
    lane_head = lax.broadcasted_iota(jnp.int32, (1, GLA_QK), 1) // GLA_DK

    def stack_heads(z):
        return jnp.concatenate(
            [jnp.where(lane_head == h, z, 0.0) for h in range(GLA_HEADS)], axis=0).astype(BF16)

    zb = jnp.zeros((GLA_CHUNK, GLA_DV), BF16)
    for c in range(ts // GLA_CHUNK):
        rows = pl.ds(c * GLA_CHUNK, GLA_CHUNK)
        la_hi, la_lo = _split_bf16(labuf[rows, :])
        tri = tri_ref[...]
        b = _dot(tri, la_hi) + _dot(tri, la_lo)
        yield
        b_mid = b[GLA_CHUNK // 2 - 1:GLA_CHUNK // 2, :]
        b_end = b[GLA_CHUNK - 1:GLA_CHUNK, :]
        q = proj[rows, E_GQ:E_GQ + GLA_QK]
        k = proj[rows, E_GK:E_GK + GLA_QK] * (GLA_DK ** -0.5)
        vb = proj[rows, E_GV:E_GV + GLA_HEADS * GLA_DV].astype(BF16)
        vh = [vb[:, h * GLA_DV:(h + 1) * GLA_DV] for h in range(GLA_HEADS)]
        s_cat = _dot_nt((q * jnp.exp(b - b_mid)).astype(BF16), stack_heads(k * jnp.exp(b_mid - b)))
        st = gla_st[...]
        inter = _dot_nt(stack_heads(q * jnp.exp(b)), st.astype(BF16))
        upd = _dot_tn(jnp.concatenate(vh, axis=0), stack_heads(k * jnp.exp(b_end - b)))
        gla_st[...] = jnp.exp(b_end) * st + upd
        yield
        p = (s_cat * ctile_ref[...]).astype(BF16)
        vbd = jnp.concatenate(
            [jnp.concatenate([vh[h] if j == h else zb for j in range(GLA_HEADS)], axis=1)
             for h in range(GLA_HEADS)], axis=0)
        out_cat = _dot(p, vbd)
        yield
        for h in range(GLA_HEADS):
            g = proj[rows, E_GG + h * GLA_DV:E_GG + (h + 1) * GLA_DV]
            o = out_cat[:, h * GLA_DV:(h + 1) * GLA_DV] + inter[h * GLA_CHUNK:(h + 1) * GLA_CHUNK, :]
            obuf[rows, GROUP_W + h * GLA_DV:GROUP_W + (h + 1) * GLA_DV] = _rms(o) * _silu(g)
        yield


def _pair_lanes(t):
    return jnp.concatenate([t[0::2], t[1::2]], axis=-1)


def _even_mixer(h, norm_w, w_in_all, w_lr, b_lr, head_norm, w_out_all, layer, ts, grp):
    bsz, seq, _ = h.shape
    hh = jnp.arange(RET_HEADS, dtype=F32)
    log_g = jnp.log1p(-jnp.exp2(-5.0 - hh))
    idx = jnp.arange(RET_CHUNK, dtype=F32)
    rel = idx[:, None] - idx[None, :]
    causal = rel >= 0
    dmat = jnp.where(causal, jnp.exp(log_g[:, None, None] * jnp.where(causal, rel, 0.0)), 0.0)
    full = (RET_HEADS, RET_CHUNK, RET_DK)
    pair = (RET_HEADS // 2, RET_CHUNK, PAIR)
    w_end = jnp.broadcast_to(jnp.exp(log_g[:, None] * (RET_CHUNK - 1.0 - idx)[None, :])[:, :, None], full)
    w_inb = jnp.broadcast_to(jnp.exp(log_g[:, None] * (idx + 1.0)[None, :])[:, :, None], full)
    dec = jnp.broadcast_to(jnp.exp(log_g * RET_CHUNK)[:, None, None], full)

    half = RET_DK // 2
    inv = ROPE_BASE ** (-jnp.arange(half, dtype=F32) / half)
    ang = jnp.arange(seq).astype(F32)[:, None] * inv[None, :]
    cos2 = jnp.concatenate([jnp.cos(ang), jnp.cos(ang)], axis=-1)
    sin2 = jnp.concatenate([-jnp.sin(ang), jnp.sin(ang)], axis=-1)

    tril = jnp.tril(jnp.ones((GLA_CHUNK, GLA_CHUNK), F32))
    ctile = jnp.tile(tril, (1, GLA_HEADS))

    w_tail = w_in_all[layer][:, E_LR:]
    w_tail = jnp.pad(w_tail, ((0, 0), (0, E_NP - E_LR - w_tail.shape[1]))).astype(BF16)
    w_lr_p = jnp.pad(w_lr, ((0, LANE - GLA_RANK), (0, 0))).astype(BF16)
    any_space = pl.BlockSpec(memory_space=pl.ANY)

    const2 = lambda shape: pl.BlockSpec(shape, lambda b, s: (0, 0), pipeline_mode=pl.Buffered(1))
    const3 = lambda shape: pl.BlockSpec(shape, lambda b, s: (0, 0, 0), pipeline_mode=pl.Buffered(1))
    return pl.pallas_call(
        functools.partial(_even_body, layer),
        grid=(bsz // grp, seq // ts),
        in_specs=[
            pl.BlockSpec((grp, ts, D_MODEL), lambda b, s: (b, s, 0)),
            const2((1, D_MODEL)),
            any_space,
            const2((D_MODEL, LANE)),
            const2((LANE, GLA_QK)),
            const2((1, GLA_QK)),
            pl.BlockSpec((ts, RET_DK), lambda b, s: (s, 0)),
            pl.BlockSpec((ts, RET_DK), lambda b, s: (s, 0)),
            const3(pair), const3(pair), const3(pair), const3(full),
            const2((GLA_CHUNK, GLA_CHUNK)),
            const2((GLA_CHUNK, GLA_QK)),
            const2((1, D_MODEL)),
            any_space,
        ],
        out_specs=pl.BlockSpec((grp, ts, D_MODEL), lambda b, s: (b, s, 0)),
        out_shape=jax.ShapeDtypeStruct(h.shape, F32),
        scratch_shapes=[
            pltpu.VMEM((grp, ts, E_NP), F32),
            pltpu.VMEM((grp, ts, GLA_QK), F32),
            pltpu.VMEM((grp, ts, D_MODEL), F32),
            pltpu.VMEM((grp,) + full, F32),
            pltpu.VMEM((grp, GLA_DV, GLA_QK), F32),
            pltpu.VMEM((D_MODEL, E_NP), BF16),
            pltpu.VMEM((D_MODEL, D_MODEL), BF16),
            pltpu.VMEM((2, D_MODEL, MIX_SLAB), F32),
            pltpu.SemaphoreType.DMA((2,)),
        ],
        compiler_params=pltpu.CompilerParams(
            dimension_semantics=("arbitrary", "arbitrary"), vmem_limit_bytes=VMEM_LIMIT),
        name="even_mixer",
    )(h, norm_w.reshape(1, -1), w_in_all, w_tail, w_lr_p, b_lr.reshape(1, -1), cos2, sin2,
      _pair_lanes(dmat), _pair_lanes(w_inb), _pair_lanes(w_end), dec, tril.astype(BF16), ctile,
      head_norm.reshape(1, -1), w_out_all)


def _tiles(ref, t0, n, rows=slice(None)):
    return jnp.concatenate([ref[t0 + t, rows, :] for t in range(n)], axis=1)


def _set_tiles(ref, t0, val):
    for t in range(val.shape[1] // LANE):
        ref[t0 + t] = val[:, t * LANE:(t + 1) * LANE]


def _permute_rows(perm_ref, x_bf16):
    return _dot(perm_ref[...], x_bf16)


def _causal_conv(tail, xb, w_ref, b_ref):
    ts = xb.shape[0]
    keep = (CONV_W - 1) * SUBLANE
    last = xb[ts - keep:, :]
    prev = tail[...]
    first = lax.broadcasted_iota(jnp.int32, (SUBLANE, 1), 0) == 0
    fix = [jnp.where(first, pltpu.roll(prev[m * SUBLANE:(m + 1) * SUBLANE, :], 1, 0),
                     pltpu.roll(last[m * SUBLANE:(m + 1) * SUBLANE, :], 1, 0)) for m in range(CONV_W - 1)]
    tail[...] = last
    xpad = jnp.concatenate(fix + [xb], axis=0)
    acc = b_ref[...] + w_ref[CONV_W - 1:CONV_W, :] * xb
    for j in range(CONV_W - 1):
        acc = acc + w_ref[j:j + 1, :] * xpad[j * SUBLANE:j * SUBLANE + ts, :]
    return acc


def _odd_body(layer, x_ref, nw_ref, win_hbm, wtail_ref, lcw_ref, lcb_ref, mcw_ref, mcb_ref,
              wa_ref, ba_ref, wx_ref, bx_ref, lam_ref, wqk_ref, bif_ref, tri_ref,
              mnw_ref, wout_hbm, perm_ref, unperm_ref, o_ref,
              proj, ltail, mtail, qk, ibuf, fbuf, obuf, lru_h, ml_c, ml_m, win_ref, wout_ref, wstage, wsem):
    groups = x_ref.shape[0]
    ts = x_ref.shape[1]

    @pl.when((pl.program_id(0) == 0) & (pl.program_id(1) == 0))
    def _():
        _stage_weight(layer, win_hbm, win_ref, wstage, wsem, 1, MIX_SLAB)
        win_ref[:, O_IF:O_NP] = wtail_ref[...]
        _stage_weight(layer, wout_hbm, wout_ref, wstage, wsem, 1, MIX_SLAB)

    @pl.when(pl.program_id(1) == 0)
    def _():
        ltail[...] = jnp.zeros_like(ltail)
        mtail[...] = jnp.zeros_like(mtail)
        lru_h[...] = jnp.zeros_like(lru_h)
        ml_c[...] = jnp.zeros_like(ml_c)
        ml_m[...] = jnp.zeros_like(ml_m)

    def sequence(g):
        x = x_ref[g]
        hn = (_rms(x) * nw_ref[...]).astype(BF16)
        hn_blocked = _permute_rows(perm_ref, hn).astype(BF16)
        for c0 in range(0, O_NP, 512):
            c1 = min(c0 + 512, O_NP)
            lhs = hn_blocked if c1 <= O_MV else hn
            _set_tiles(proj.at[g], c0 // LANE, _dot(lhs, win_ref[:, c0:c1]))
            yield
        yield from _round_robin([
            _lru_chain(proj.at[g], ltail.at[g], obuf.at[g], lru_h.at[g],
                       lcw_ref, lcb_ref, wa_ref, ba_ref, wx_ref, bx_ref, lam_ref, unperm_ref, ts),
            _mlstm_chain(proj.at[g], mtail.at[g], qk.at[g], ibuf.at[g], fbuf.at[g], obuf.at[g],
                         ml_c.at[g], ml_m.at[g], mcw_ref, mcb_ref, wqk_ref, bif_ref, tri_ref, mnw_ref,
                         unperm_ref, ts)])
        ob = _tiles(obuf.at[g], 0, D_MODEL // LANE).astype(BF16)
        for c0 in range(0, D_MODEL, OUT_SLAB):
            o_ref[g, :, c0:c0 + OUT_SLAB] = x[:, c0:c0 + OUT_SLAB] + _dot(ob, wout_ref[:, c0:c0 + OUT_SLAB])
            yield

    _run(_round_robin([sequence(g) for g in range(groups)], lag=ODD_LAG))


def _lru_chain(proj, ltail, obuf, lru_h, lcw_ref, lcb_ref, wa_ref, ba_ref, wx_ref, bx_ref, lam_ref, unperm_ref, ts):
    lxc = _causal_conv(ltail, _tiles(proj, O_LX // LANE, LRU_W // LANE), lcw_ref, lcb_ref)
    xb = lxc.astype(BF16)
    r = _sigmoid(_dot(xb, wa_ref[...]) + ba_ref[...])
    i = _sigmoid(_dot(xb, wx_ref[...]) + bx_ref[...])
    yield
    lam = lam_ref[...]
    softplus_neg_lam = jnp.maximum(-lam, 0.0) + jnp.log1p(jnp.exp(-jnp.abs(lam)))
    log_a = (-LRU_C) * r * softplus_neg_lam
    a = jnp.exp(log_a)
    u = jnp.sqrt(-jnp.tanh(log_a) * (a * a + 1.0)) * (i * lxc)
    n = ts // SUBLANE
    hs, ps = [u[0:SUBLANE, :]], [a[0:SUBLANE, :]]
    for j in range(1, n):
        aj = a[j * SUBLANE:(j + 1) * SUBLANE, :]
        hs.append(aj * hs[-1] + u[j * SUBLANE:(j + 1) * SUBLANE, :])
        ps.append(aj * ps[-1])
    sub = lax.broadcasted_iota(jnp.int32, (SUBLANE, 1), 0)
    pt, ht = ps[-1], hs[-1]
    d = 1
    while d < SUBLANE:
        keep = sub >= d
        p_s = jnp.where(keep, pltpu.roll(pt, d, 0), 1.0)
        h_s = jnp.where(keep, pltpu.roll(ht, d, 0), 0.0)
        ht = pt * h_s + ht
        pt = pt * p_s
        d *= 2
    hprev = lru_h[0:1, :]
    block_end = ht + pt * hprev
    carry = jnp.where(sub == 0, hprev, pltpu.roll(block_end, 1, 0))
    lru_h[0:1, :] = block_end[SUBLANE - 1:SUBLANE, :]
    hfull = jnp.concatenate([hs[j] + ps[j] * carry for j in range(n)], axis=0)
    ly = _tiles(proj, O_LY // LANE, LRU_W // LANE)
    gelu = ly * (0.5 * (1.0 + jnp.tanh(np.float32(np.sqrt(2.0 / np.pi)) * (ly + 0.044715 * (ly * ly * ly)))))
    _set_tiles(obuf, 0, _permute_rows(unperm_ref, (hfull * gelu).astype(BF16)))
    yield


def _mlstm_chain(proj, mtail, qk, ibuf, fbuf, obuf, ml_c, ml_m, mcw_ref, mcb_ref, wqk_ref, bif_ref, tri_ref,
                 mnw_ref, unperm_ref, ts):
    mc_blocked = _silu(_causal_conv(mtail, _tiles(proj, O_MU // LANE, GROUP_W // LANE), mcw_ref, mcb_ref))
    mc = _permute_rows(unperm_ref, mc_blocked.astype(BF16)).astype(BF16)
    for h in range(ML_HEADS):
        qkh = _dot(mc[:, h * ML_DH:(h + 1) * ML_DH], wqk_ref[h])
        qk[h] = qkh[:, :ML_DH]
        qk[ML_HEADS + h] = qkh[:, ML_DH:] * (ML_DH ** -0.5)
    gates = proj[O_IF // LANE] + bif_ref[...]
    ibuf[...] = gates
    fbuf[...] = pltpu.roll(_log_sigmoid(gates), LANE - ML_HEADS, 1)
    yield

    row_i = lax.broadcasted_iota(jnp.int32, (ML_CHUNK, ML_CHUNK), 0)
    col_i = lax.broadcasted_iota(jnp.int32, (ML_CHUNK, ML_CHUNK), 1)
    causal = row_i >= col_i
    ones_col = jnp.where(col_i == 0, 1.0, 0.0).astype(BF16)

    for c in range(ts // ML_CHUNK):
        rows = pl.ds(c * ML_CHUNK, ML_CHUNK)
        f_hi, f_lo = _split_bf16(fbuf[rows, :])
        tri = tri_ref[...]
        fc = _dot(tri, f_hi) + _dot(tri, f_lo)
        yield
        dm = ibuf[rows, :] - fc
        dm_t = dm.T
        b_end = fc[ML_CHUNK - 1:ML_CHUNK, :]
        a_col = b_end + dm
        a_max = jnp.max(a_col, axis=0, keepdims=True)
        w_end = jnp.exp(a_col - a_max)
        m_old = ml_m[0:1, :]
        m_new = jnp.maximum(b_end + m_old, a_max)
        g_old = jnp.exp(b_end + m_old - m_new)
        g_new = jnp.exp(a_max - m_new)
        ml_m[0:1, :] = m_new
        s_cat, qc, vh = [], [], []
        for pr in range(ML_HEADS // 2):
            q2 = _tiles(qk, 2 * pr, 2, rows).astype(BF16)
            k2f = _tiles(qk, ML_HEADS + 2 * pr, 2, rows)
            k2 = k2f.astype(BF16)
            s_cat.append(_dot_nt(q2, _bdiag2(k2[:, :LANE], k2[:, LANE:])))
            for j in range(2):
                h = 2 * pr + j
                v = proj[O_MV // LANE + h, rows, :].astype(BF16)
                cx = ml_c[h]
                qc.append(_dot(q2[:, j * LANE:(j + 1) * LANE], cx.astype(BF16)))
                wk = (w_end[:, h:h + 1] * k2f[:, j * LANE:(j + 1) * LANE]).astype(BF16)
                updx = _dot_tn(wk, jnp.concatenate([v, ones_col], axis=1))
                ml_c[h] = g_old[:, h:h + 1] * cx + g_new[:, h:h + 1] * updx
                vh.append(v)
        yield
        mxs, sms, dens, nums = [], [], [], []
        for h in range(ML_HEADS):
            d_row = dm_t[h:h + 1, :]
            m_h = m_old[:, h:h + 1]
            cm = jnp.max(jnp.where(causal, d_row, -jnp.inf), axis=1, keepdims=True)
            mx = jnp.maximum(cm, m_h)
            p = jnp.where(causal, jnp.exp(d_row - mx), 0.0)
            s = s_cat[h // 2][:, (h % 2) * LANE:(h % 2 + 1) * LANE] * p
            mxs.append(mx)
            sms.append(s.astype(BF16))
            dens.append(jnp.sum(s, axis=1, keepdims=True))
        for pr in range(ML_HEADS // 2):
            nums.append(_dot(jnp.concatenate([sms[2 * pr], sms[2 * pr + 1]], axis=1),
                             _bdiag2(vh[2 * pr], vh[2 * pr + 1])))
        yield
        for h in range(ML_HEADS):
            m_h = m_old[:, h:h + 1]
            w_inter = jnp.exp(m_h - mxs[h])
            num = nums[h // 2][:, (h % 2) * LANE:(h % 2 + 1) * LANE] + w_inter * qc[h][:, :ML_DH]
            den = dens[h] + w_inter * qc[h][:, ML_DH:ML_DH + 1]
            m_t = fc[:, h:h + 1] + mxs[h]
            hh = num / jnp.maximum(jnp.abs(den), jnp.exp(-m_t))
            og = _sigmoid(proj[O_MO // LANE + h, rows, :])
            obuf[GROUP_W // LANE + h, rows, :] = _rms(og * hh) * mnw_ref[:, h * ML_DH:(h + 1) * ML_DH]
        yield


def _block_diag(w):
    g, n, _ = w.shape
    eye = jnp.eye(g, dtype=w.dtype)
    return (eye[:, None, :, None] * w[:, :, None, :]).reshape(g * n, g * n)


def _odd_mixer(h, norm_w, w_in_all, lcw, lcb, wa, ba, wx, bx, lam, mcw, mcb, wq, wk, bi, bf, ml_norm, w_out_all,
               layer, ts, grp):
    bsz, seq, _ = h.shape
    w_tail = w_in_all[layer][:, O_IF:]
    w_tail = jnp.pad(w_tail, ((0, 0), (0, O_NP - O_IF - w_tail.shape[1]))).astype(BF16)
    any_space = pl.BlockSpec(memory_space=pl.ANY)
    wqk = jnp.concatenate([wq, wk], axis=-1).astype(BF16)
    bif = jnp.pad(jnp.concatenate([bi, bf]), (0, LANE - 2 * ML_HEADS)).reshape(1, LANE)
    tri = jnp.tril(jnp.ones((ML_CHUNK, ML_CHUNK), F32)).astype(BF16)
    r = jnp.arange(ts)
    time_of_row = (r % SUBLANE) * (ts // SUBLANE) + r // SUBLANE
    perm = (time_of_row[:, None] == r[None, :]).astype(BF16)
    row = lambda t: t.reshape(1, -1)

    const2 = lambda shape: pl.BlockSpec(shape, lambda b, s: (0, 0), pipeline_mode=pl.Buffered(1))
    const3 = lambda shape: pl.BlockSpec(shape, lambda b, s: (0, 0, 0), pipeline_mode=pl.Buffered(1))
    return pl.pallas_call(
        functools.partial(_odd_body, layer),
        grid=(bsz // grp, seq // ts),
        in_specs=[
            pl.BlockSpec((grp, ts, D_MODEL), lambda b, s: (b, s, 0)),
            const2((1, D_MODEL)),
            any_space,
            const2((D_MODEL, LANE)),
            const2((CONV_W, LRU_W)), const2((1, LRU_W)),
            const2((CONV_W, GROUP_W)), const2((1, GROUP_W)),
            const2((LRU_W, LRU_W)), const2((1, LRU_W)),
            const2((LRU_W, LRU_W)), const2((1, LRU_W)),
            const2((1, LRU_W)),
            const3((ML_HEADS, ML_DH, 2 * ML_DH)),
            const2((1, LANE)),
            const2((ML_CHUNK, ML_CHUNK)),
            const2((1, GROUP_W)),
            any_space,
            const2((ts, ts)), const2((ts, ts)),
        ],
        out_specs=pl.BlockSpec((grp, ts, D_MODEL), lambda b, s: (b, s, 0)),
        out_shape=jax.ShapeDtypeStruct(h.shape, F32),
        scratch_shapes=[
            pltpu.VMEM((grp, O_NP // LANE, ts, LANE), F32),
            pltpu.VMEM((grp, (CONV_W - 1) * SUBLANE, LRU_W), F32),
            pltpu.VMEM((grp, (CONV_W - 1) * SUBLANE, GROUP_W), F32),
            pltpu.VMEM((grp, 2 * GROUP_W // LANE, ts, LANE), F32),
            pltpu.VMEM((grp, ts, LANE), F32),
            pltpu.VMEM((grp, ts, LANE), F32),
            pltpu.VMEM((grp, D_MODEL // LANE, ts, LANE), F32),
            pltpu.VMEM((grp, SUBLANE, LRU_W), F32),
            pltpu.VMEM((grp, ML_HEADS, ML_DH, 2 * ML_DH), F32),
            pltpu.VMEM((grp, SUBLANE, LANE), F32),
            pltpu.VMEM((D_MODEL, O_NP), BF16),
            pltpu.VMEM((D_MODEL, D_MODEL), BF16),
            pltpu.VMEM((2, D_MODEL, MIX_SLAB), F32),
            pltpu.SemaphoreType.DMA((2,)),
        ],
        compiler_params=pltpu.CompilerParams(
            dimension_semantics=("arbitrary", "arbitrary"), vmem_limit_bytes=VMEM_LIMIT),
        name="odd_mixer",
    )(h, row(norm_w), w_in_all, w_tail, lcw, row(lcb), mcw, row(mcb),
      _block_diag(wa).astype(BF16), row(ba), _block_diag(wx).astype(BF16), row(bx), row(lam),
      wqk, bif, tri, row(ml_norm), w_out_all, perm, perm.T)


def kernel(x, ffn1_norm, ffn1_wgu, ffn1_wd, mix_norm, ffn2_norm, ffn2_wgu, ffn2_wd, e_w_in, e_w_lr_up, e_b_lr, e_head_norm, e_w_out, o_w_in, o_lru_conv_w, o_lru_conv_b, o_lru_wa, o_lru_ba, o_lru_wx, o_lru_bx, o_lru_lambda, o_ml_conv_w, o_ml_conv_b, o_ml_wq, o_ml_wk, o_ml_bi, o_ml_bf, o_ml_norm, o_w_out, final_norm):
    bsz, seq, d = x.shape
    depth = ffn1_norm.shape[0]
    h = x
    for layer in range(depth):
        j = layer // 2
        h = _ffn(h.reshape(bsz * seq, d), ffn1_norm[layer], ffn1_wgu, ffn1_wd, layer,
                 final_norm, False).reshape(bsz, seq, d)
        if layer % 2 == 0:
            h = _even_mixer(h, mix_norm[layer], e_w_in, e_w_lr_up[j], e_b_lr[j], e_head_norm[j],
                            e_w_out, j, MIX_TS, EVEN_G)
        else:
            h = _odd_mixer(h, mix_norm[layer], o_w_in, o_lru_conv_w[j], o_lru_conv_b[j],
                           o_lru_wa[j], o_lru_ba[j], o_lru_wx[j], o_lru_bx[j], o_lru_lambda[j],
                           o_ml_conv_w[j], o_ml_conv_b[j], o_ml_wq[j], o_ml_wk[j], o_ml_bi[j], o_ml_bf[j],
                           o_ml_norm[j], o_w_out, j, MIX_TS, ODD_G)
        h = _ffn(h.reshape(bsz * seq, d), ffn2_norm[layer], ffn2_wgu, ffn2_wd, layer,
                 final_norm, layer == depth - 1).reshape(bsz, seq, d)
    return h
```

```python
import functools

import numpy as np
import jax
import jax.numpy as jnp
from jax import lax
from jax.experimental import pallas as pl
from jax.experimental.pallas import tpu as pltpu

F32 = jnp.float32
BF16 = jnp.bfloat16

D_MODEL = 1024
D_FF = 2816
GROUP_W = D_MODEL // 2
EPS = 1e-6

RET_HEADS = 4
RET_DK = 128
RET_CHUNK = 128
ROPE_BASE = 10000.0

GLA_HEADS = 4
GLA_DK = 64
GLA_DV = 128
GLA_RANK = 16
GLA_TAU = 16.0
GLA_CHUNK = 64
GLA_QK = GLA_HEADS * GLA_DK

LRU_W = GROUP_W
LRU_BLOCKS = 8
LRU_BS = LRU_W // LRU_BLOCKS
LRU_C = 8.0
CONV_W = 4

ML_HEADS = 4
ML_DH = 128
ML_CHUNK = 128

LANE = 128
SUBLANE = 8
VMEM_LIMIT = 56 * 1024 * 1024

E_RQ, E_RK, E_RV, E_RG = 0, 512, 1024, 1536
E_GQ, E_GK, E_GV, E_GG, E_LR = 2048, 2304, 2560, 3072, 3584
E_NP = 3712
O_LY, O_LX, O_MU, O_MV, O_MO, O_IF = 0, 512, 1024, 1536, 2048, 2560
O_NP = 2688

FFN_TM = 1024
FFN_SUB = 256
FFN_FC = 256
FFN_GU_ROWS = 64
FFN_D_ROWS = 256
STAGE_DEPTH = 4
MIX_STAGE_DEPTH = 3
MIX_TS = 256
EVEN_G = 4
ODD_G = 2
MIX_SLAB = 512
OUT_SLAB = 256
EVEN_LAG = 0
ODD_LAG = 6
PAIR = 2 * LANE


def _dot(a, b):
    return jnp.dot(a, b, preferred_element_type=F32)


def _dot_nt(a, b):
    return lax.dot_general(a, b, (((1,), (1,)), ((), ())), preferred_element_type=F32)


def _dot_tn(a, b):
    return lax.dot_general(a, b, (((0,), (0,)), ((), ())), preferred_element_type=F32)


def _rms(x):
    return x * lax.rsqrt(jnp.mean(x * x, axis=-1, keepdims=True) + EPS)


def _sigmoid(x):
    return 1.0 / (1.0 + jnp.exp(-x))


def _silu(x):
    return x * _sigmoid(x)


def _log_sigmoid(x):
    return jnp.minimum(x, 0.0) - jnp.log1p(jnp.exp(-jnp.abs(x)))


def _split_bf16(x):
    hi = x.astype(BF16)
    lo = (x - hi.astype(F32)).astype(BF16)
    return hi, lo


def _bdiag2(a, b):
    z = jnp.zeros_like(a)
    return jnp.concatenate([jnp.concatenate([a, z], axis=1), jnp.concatenate([z, b], axis=1)], axis=0)


def _round_robin(gens, lag=0):
    gens = list(gens)
    done = [False] * len(gens)
    rnd = 0
    while not all(done):
        for i, gen in enumerate(gens):
            if done[i] or rnd < lag * i:
                continue
            try:
                next(gen)
            except StopIteration:
                done[i] = True
        rnd += 1
        yield


def _run(gen):
    for _ in gen:
        pass


def _stage_weight(layer, w_hbm, w_bf, stage, sem, axis, width):
    n = w_bf.shape[axis] // width
    depth = stage.shape[0]

    def window(j):
        sl = pl.ds(j * width, width)
        return (sl, slice(None)) if axis == 0 else (slice(None), sl)

    def copy(j):
        return pltpu.make_async_copy(w_hbm.at[(layer,) + window(j)], stage.at[j % depth], sem.at[j % depth])

    for j in range(min(depth - 1, n)):
        copy(j).start()
    yield
    for j in range(n):
        if j + depth - 1 < n:
            copy(j + depth - 1).start()
        copy(j).wait()
        w_bf[window(j)] = stage[j % depth].astype(BF16)


def _stage_all(*streams):
    for s in streams:
        next(s)
    for s in streams:
        _run(s)


def _ffn_body(final, layer, x_ref, nw_ref, wgu_hbm, wd_hbm, fw_ref, o_ref,
              act_ref, wgu_ref, wd_ref, stage_gu, stage_d, sem_gu, sem_d):
    @pl.when(pl.program_id(0) == 0)
    def _():
        _stage_all(_stage_weight(layer, wgu_hbm, wgu_ref, stage_gu, sem_gu, 0, FFN_GU_ROWS),
                   _stage_weight(layer, wd_hbm, wd_ref, stage_d, sem_d, 0, FFN_D_ROWS))

    for r0 in range(0, FFN_TM, FFN_SUB):
        rows = pl.ds(r0, FFN_SUB)
        x = x_ref[rows, :]
        xn = (_rms(x) * nw_ref[...]).astype(BF16)
        for j in range(D_FF // FFN_FC):
            c0 = j * FFN_FC
            g = _dot(xn, wgu_ref[:, c0:c0 + FFN_FC])
            u = _dot(xn, wgu_ref[:, D_FF + c0:D_FF + c0 + FFN_FC])
            act_ref[rows, c0:c0 + FFN_FC] = (_silu(g) * u).astype(BF16)
        h = x + 0.5 * _dot(act_ref[rows, :], wd_ref[...])
        if final:
            h = _rms(h) * fw_ref[...]
        o_ref[rows, :] = h


def _ffn(h, norm_w, wgu_all, wd_all, layer, final_w, final):
    t = h.shape[0]
    const = lambda shape: pl.BlockSpec(shape, lambda i: (0, 0), pipeline_mode=pl.Buffered(1))
    return pl.pallas_call(
        functools.partial(_ffn_body, final, layer),
        grid=(t // FFN_TM,),
        in_specs=[
            pl.BlockSpec((FFN_TM, D_MODEL), lambda i: (i, 0)),
            const((1, D_MODEL)),
            pl.BlockSpec(memory_space=pl.ANY),
            pl.BlockSpec(memory_space=pl.ANY),
            const((1, D_MODEL)),
        ],
        out_specs=pl.BlockSpec((FFN_TM, D_MODEL), lambda i: (i, 0)),
        out_shape=jax.ShapeDtypeStruct((t, D_MODEL), F32),
        scratch_shapes=[
            pltpu.VMEM((FFN_TM, D_FF), BF16),
            pltpu.VMEM((D_MODEL, 2 * D_FF), BF16),
            pltpu.VMEM((D_FF, D_MODEL), BF16),
            pltpu.VMEM((STAGE_DEPTH, FFN_GU_ROWS, 2 * D_FF), F32),
            pltpu.VMEM((STAGE_DEPTH, FFN_D_ROWS, D_MODEL), F32),
            pltpu.SemaphoreType.DMA((STAGE_DEPTH,)),
            pltpu.SemaphoreType.DMA((STAGE_DEPTH,)),
        ],
        compiler_params=pltpu.CompilerParams(
            dimension_semantics=("arbitrary",), vmem_limit_bytes=VMEM_LIMIT),
        name="ffn_final" if final else "ffn",
    )(h, norm_w.reshape(1, -1), wgu_all, wd_all, final_w.reshape(1, -1))


def _even_body(layer, x_ref, nw_ref, win_hbm, wtail_ref, wlr_ref, blr_ref, cos_ref, sin_ref,
               dmat_ref, winb_ref, wendb_ref, dec_ref, tri_ref, ctile_ref,
               hnw_ref, wout_hbm, o_ref,
               proj, labuf, obuf, ret_s, gla_st, win_ref, wout_ref, wstage, wsem):
    groups = x_ref.shape[0]
    ts = x_ref.shape[1]

    @pl.when((pl.program_id(0) == 0) & (pl.program_id(1) == 0))
    def _():
        _stage_all(_stage_weight(layer, win_hbm, win_ref, wstage, wsem, 1, MIX_SLAB))
        win_ref[:, E_LR:E_NP] = wtail_ref[...]
        _stage_all(_stage_weight(layer, wout_hbm, wout_ref, wstage, wsem, 1, MIX_SLAB))

    @pl.when(pl.program_id(1) == 0)
    def _():
        ret_s[...] = jnp.zeros_like(ret_s)
        gla_st[...] = jnp.zeros_like(gla_st)

    def sequence(g):
        x = x_ref[g]
        pg = proj.at[g]
        hn = (_rms(x) * nw_ref[...]).astype(BF16)
        for c0 in range(0, E_NP, MIX_SLAB):
            c1 = min(c0 + MIX_SLAB, E_NP)
            pg[:, c0:c1] = _dot(hn, win_ref[:, c0:c1])
            yield
        glr = pg[:, E_LR:E_LR + LANE].astype(BF16)
        labuf[g] = _log_sigmoid(_dot(glr, wlr_ref[...]) + blr_ref[...]) * (1.0 / GLA_TAU)
        cos = cos_ref[...]
        sin = sin_ref[...]
        for c0 in range(0, 2 * RET_HEADS * RET_DK, RET_DK):
            xx = pg[:, c0:c0 + RET_DK]
            r = xx * cos + pltpu.roll(xx, RET_DK // 2, 1) * sin
            if c0 >= E_RK:
                r = r * (RET_DK ** -0.5)
            pg[:, c0:c0 + RET_DK] = r
        yield
        yield from _round_robin([
            _ret_chain(pg, obuf.at[g], ret_s.at[g], dmat_ref, winb_ref, wendb_ref, dec_ref, ts),
            _gla_chain(pg, labuf.at[g], obuf.at[g], gla_st.at[g], tri_ref, ctile_ref, ts)])
        o_ref[g] = x + _dot((obuf[g] * hnw_ref[...]).astype(BF16), wout_ref[...])
        yield

    _run(_round_robin([sequence(g) for g in range(groups)], lag=EVEN_LAG))


def _ret_chain(proj, obuf, ret_s, dmat_ref, winb_ref, wendb_ref, dec_ref, ts):
    for c in range(ts // RET_CHUNK):
        rows = pl.ds(c * RET_CHUNK, RET_CHUNK)
        stage = []
        for pr in range(RET_HEADS // 2):
            lo = pr * PAIR
            q2 = proj[rows, E_RQ + lo:E_RQ + lo + PAIR].astype(BF16)
            k2 = proj[rows, E_RK + lo:E_RK + lo + PAIR].astype(BF16)
            v2 = proj[rows, E_RV + lo:E_RV + lo + PAIR]
            s0 = ret_s[2 * pr]
            s1 = ret_s[2 * pr + 1]
            s_cat = _dot_nt(q2, _bdiag2(k2[:, :LANE], k2[:, LANE:]))
            inter = _dot(q2, _bdiag2(s0.astype(BF16), s1.astype(BF16)))
            full = _dot_tn(k2, (wendb_ref[pr] * v2).astype(BF16))
            ret_s[2 * pr] = dec_ref[2 * pr] * s0 + full[:LANE, :LANE]
            ret_s[2 * pr + 1] = dec_ref[2 * pr + 1] * s1 + full[LANE:, LANE:]
            stage.append((s_cat, inter, v2.astype(BF16)))
        yield
        outs = []
        for pr in range(RET_HEADS // 2):
            s_cat, inter, vb = stage[pr]
            p = (s_cat * dmat_ref[pr]).astype(BF16)
            outs.append(_dot(p, _bdiag2(vb[:, :LANE], vb[:, LANE:])) + inter * winb_ref[pr])
        yield
        for h in range(RET_HEADS):
            o = outs[h // 2][:, (h % 2) * LANE:(h % 2 + 1) * LANE]
            g = proj[rows, E_RG + h * RET_DK:E_RG + (h + 1) * RET_DK]
            obuf[rows, h * RET_DK:(h + 1) * RET_DK] = _rms(o) * _silu(g)
        yield


def _gla_chain(proj, labuf, obuf, gla_st, tri_ref, ctile_ref, ts):
    lane_head = lax.broadcasted_iota(jnp.int32, (1, GLA_QK), 1) // GLA_DK

    def stack_heads(z):
        return jnp.concatenate(
            [jnp.where(lane_head == h, z, 0.0) for h in range(GLA_HEADS)], axis=0).astype(BF16)

    zb = jnp.zeros((GLA_CHUNK, GLA_DV), BF16)
    for c in range(ts // GLA_CHUNK):
        rows = pl.ds(c * GLA_CHUNK, GLA_CHUNK)
        la_hi, la_lo = _split_bf16(labuf[rows, :])
        tri = tri_ref[...]
        b = _dot(tri, la_hi) + _dot(tri, la_lo)
        yield
        b_mid = b[GLA_CHUNK // 2 - 1:GLA_CHUNK // 2, :]
        b_end = b[GLA_CHUNK - 1:GLA_CHUNK, :]
        q = proj[rows, E_GQ:E_GQ + GLA_QK]
        k = proj[rows, E_GK:E_GK + GLA_QK] * (GLA_DK ** -0.5)
        vb = proj[rows, E_GV:E_GV + GLA_HEADS * GLA_DV].astype(BF16)
        vh = [vb[:, h * GLA_DV:(h + 1) * GLA_DV] for h in range(GLA_HEADS)]
        s_cat = _dot_nt((q * jnp.exp(b - b_mid)).astype(BF16), stack_heads(k * jnp.exp(b_mid - b)))
        st = gla_st[...]
        inter = _dot_nt(stack_heads(q * jnp.exp(b)), st.astype(BF16))
        upd = _dot_tn(jnp.concatenate(vh, axis=0), stack_heads(k * jnp.exp(b_end - b)))
        gla_st[...] = jnp.exp(b_end) * st + upd
        yield
        p = (s_cat * ctile_ref[...]).astype(BF16)
        vbd = jnp.concatenate(
            [jnp.concatenate([vh[h] if j == h else zb for j in range(GLA_HEADS)], axis=1)
             for h in range(GLA_HEADS)], axis=0)
        out_cat = _dot(p, vbd)
        yield
        for h in range(GLA_HEADS):
            g = proj[rows, E_GG + h * GLA_DV:E_GG + (h + 1) * GLA_DV]
            o = out_cat[:, h * GLA_DV:(h + 1) * GLA_DV] + inter[h * GLA_CHUNK:(h + 1) * GLA_CHUNK, :]
            obuf[rows, GROUP_W + h * GLA_DV:GROUP_W + (h + 1) * GLA_DV] = _rms(o) * _silu(g)
        yield


def _pair_lanes(t):
    return jnp.concatenate([t[0::2], t[1::2]], axis=-1)


def _even_mixer(h, norm_w, w_in_all, w_lr, b_lr, head_norm, w_out_all, layer, ts, grp):
    bsz, seq, _ = h.shape
    hh = jnp.arange(RET_HEADS, dtype=F32)
    log_g = jnp.log1p(-jnp.exp2(-5.0 - hh))
    idx = jnp.arange(RET_CHUNK, dtype=F32)
    rel = idx[:, None] - idx[None, :]
    causal = rel >= 0
    dmat = jnp.where(causal, jnp.exp(log_g[:, None, None] * jnp.where(causal, rel, 0.0)), 0.0)
    full = (RET_HEADS, RET_CHUNK, RET_DK)
    pair = (RET_HEADS // 2, RET_CHUNK, PAIR)
    w_end = jnp.broadcast_to(jnp.exp(log_g[:, None] * (RET_CHUNK - 1.0 - idx)[None, :])[:, :, None], full)
    w_inb = jnp.broadcast_to(jnp.exp(log_g[:, None] * (idx + 1.0)[None, :])[:, :, None], full)
    dec = jnp.broadcast_to(jnp.exp(log_g * RET_CHUNK)[:, None, None], full)

    half = RET_DK // 2
    inv = ROPE_BASE ** (-jnp.arange(half, dtype=F32) / half)
    ang = jnp.arange(seq).astype(F32)[:, None] * inv[None, :]
    cos2 = jnp.concatenate([jnp.cos(ang), jnp.cos(ang)], axis=-1)
    sin2 = jnp.concatenate([-jnp.sin(ang), jnp.sin(ang)], axis=-1)

    tril = jnp.tril(jnp.ones((GLA_CHUNK, GLA_CHUNK), F32))
    ctile = jnp.tile(tril, (1, GLA_HEADS))

    w_tail = w_in_all[layer][:, E_LR:]
    w_tail = jnp.pad(w_tail, ((0, 0), (0, E_NP - E_LR - w_tail.shape[1]))).astype(BF16)
    w_lr_p = jnp.pad(w_lr, ((0, LANE - GLA_RANK), (0, 0))).astype(BF16)
    any_space = pl.BlockSpec(memory_space=pl.ANY)

    const2 = lambda shape: pl.BlockSpec(shape, lambda b, s: (0, 0), pipeline_mode=pl.Buffered(1))
    const3 = lambda shape: pl.BlockSpec(shape, lambda b, s: (0, 0, 0), pipeline_mode=pl.Buffered(1))
    return pl.pallas_call(
        functools.partial(_even_body, layer),
        grid=(bsz // grp, seq // ts),
        in_specs=[
            pl.BlockSpec((grp, ts, D_MODEL), lambda b, s: (b, s, 0)),
            const2((1, D_MODEL)),
            any_space,
            const2((D_MODEL, LANE)),
            const2((LANE, GLA_QK)),
            const2((1, GLA_QK)),
            pl.BlockSpec((ts, RET_DK), lambda b, s: (s, 0)),
            pl.BlockSpec((ts, RET_DK), lambda b, s: (s, 0)),
            const3(pair), const3(pair), const3(pair), const3(full),
            const2((GLA_CHUNK, GLA_CHUNK)),
            const2((GLA_CHUNK, GLA_QK)),
            const2((1, D_MODEL)),
            any_space,
        ],
        out_specs=pl.BlockSpec((grp, ts, D_MODEL), lambda b, s: (b, s, 0)),
        out_shape=jax.ShapeDtypeStruct(h.shape, F32),
        scratch_shapes=[
            pltpu.VMEM((grp, ts, E_NP), F32),
            pltpu.VMEM((grp, ts, GLA_QK), F32),
            pltpu.VMEM((grp, ts, D_MODEL), F32),
            pltpu.VMEM((grp,) + full, F32),
            pltpu.VMEM((grp, GLA_DV, GLA_QK), F32),
            pltpu.VMEM((D_MODEL, E_NP), BF16),
            pltpu.VMEM((D_MODEL, D_MODEL), BF16),
            pltpu.VMEM((MIX_STAGE_DEPTH, D_MODEL, MIX_SLAB), F32),
            pltpu.SemaphoreType.DMA((MIX_STAGE_DEPTH,)),
        ],
        compiler_params=pltpu.CompilerParams(
            dimension_semantics=("arbitrary", "arbitrary"), vmem_limit_bytes=VMEM_LIMIT),
        name="even_mixer",
    )(h, norm_w.reshape(1, -1), w_in_all, w_tail, w_lr_p, b_lr.reshape(1, -1), cos2, sin2,
      _pair_lanes(dmat), _pair_lanes(w_inb), _pair_lanes(w_end), dec, tril.astype(BF16), ctile,
      head_norm.reshape(1, -1), w_out_all)


def _tiles(ref, t0, n, rows=slice(None)):
    return jnp.concatenate([ref[t0 + t, rows, :] for t in range(n)], axis=1)


def _set_tiles(ref, t0, val):
    for t in range(val.shape[1] // LANE):
        ref[t0 + t] = val[:, t * LANE:(t + 1) * LANE]


def _permute_rows(perm_ref, x_bf16):
    return _dot(perm_ref[...], x_bf16)


def _causal_conv(tail, xb, w_ref, b_ref):
    ts = xb.shape[0]
    keep = (CONV_W - 1) * SUBLANE
    last = xb[ts - keep:, :]
    prev = tail[...]
    first = lax.broadcasted_iota(jnp.int32, (SUBLANE, 1), 0) == 0
    fix = [jnp.where(first, pltpu.roll(prev[m * SUBLANE:(m + 1) * SUBLANE, :], 1, 0),
                     pltpu.roll(last[m * SUBLANE:(m + 1) * SUBLANE, :], 1, 0)) for m in range(CONV_W - 1)]
    tail[...] = last
    xpad = jnp.concatenate(fix + [xb], axis=0)
    acc = b_ref[...] + w_ref[CONV_W - 1:CONV_W, :] * xb
    for j in range(CONV_W - 1):
        acc = acc + w_ref[j:j + 1, :] * xpad[j * SUBLANE:j * SUBLANE + ts, :]
    return acc


def _odd_body(layer, x_ref, nw_ref, win_hbm, wtail_ref, lcw_ref, lcb_ref, mcw_ref, mcb_ref,
              wa_ref, ba_ref, wx_ref, bx_ref, lam_ref, wqk_ref, bif_ref, tri_ref,
              mnw_ref, wout_hbm, perm_ref, unperm_ref, o_ref,
              proj, ltail, mtail, qk, ibuf, fbuf, obuf, lru_h, ml_c, ml_m, win_ref, wout_ref, wstage, wsem):
    groups = x_ref.shape[0]
    ts = x_ref.shape[1]

    @pl.when((pl.program_id(0) == 0) & (pl.program_id(1) == 0))
    def _():
        _stage_all(_stage_weight(layer, win_hbm, win_ref, wstage, wsem, 1, MIX_SLAB))
        win_ref[:, O_IF:O_NP] = wtail_ref[...]
        _stage_all(_stage_weight(layer, wout_hbm, wout_ref, wstage, wsem, 1, MIX_SLAB))

    @pl.when(pl.program_id(1) == 0)
    def _():
        ltail[...] = jnp.zeros_like(ltail)
        mtail[...] = jnp.zeros_like(mtail)
        lru_h[...] = jnp.zeros_like(lru_h)
        ml_c[...] = jnp.zeros_like(ml_c)
        ml_m[...] = jnp.zeros_like(ml_m)

    def sequence(g):
        x = x_ref[g]
        hn = (_rms(x) * nw_ref[...]).astype(BF16)
        hn_blocked = _permute_rows(perm_ref, hn).astype(BF16)
        for c0 in range(0, O_NP, 512):
            c1 = min(c0 + 512, O_NP)
            lhs = hn_blocked if c1 <= O_MV else hn
            _set_tiles(proj.at[g], c0 // LANE, _dot(lhs, win_ref[:, c0:c1]))
            yield
        yield from _round_robin([
            _lru_chain(proj.at[g], ltail.at[g], obuf.at[g], lru_h.at[g],
                       lcw_ref, lcb_ref, wa_ref, ba_ref, wx_ref, bx_ref, lam_ref, unperm_ref, ts),
            _mlstm_chain(proj.at[g], mtail.at[g], qk.at[g], ibuf.at[g], fbuf.at[g], obuf.at[g],
                         ml_c.at[g], ml_m.at[g], mcw_ref, mcb_ref, wqk_ref, bif_ref, tri_ref, mnw_ref,
                         unperm_ref, ts)])
        ob = _tiles(obuf.at[g], 0, D_MODEL // LANE).astype(BF16)
        for c0 in range(0, D_MODEL, OUT_SLAB):
            o_ref[g, :, c0:c0 + OUT_SLAB] = x[:, c0:c0 + OUT_SLAB] + _dot(ob, wout_ref[:, c0:c0 + OUT_SLAB])
            yield

    _run(_round_robin([sequence(g) for g in range(groups)], lag=ODD_LAG))


def _lru_chain(proj, ltail, obuf, lru_h, lcw_ref, lcb_ref, wa_ref, ba_ref, wx_ref, bx_ref, lam_ref, unperm_ref, ts):
    lxc = _causal_conv(ltail, _tiles(proj, O_LX // LANE, LRU_W // LANE), lcw_ref, lcb_ref)
    xb = lxc.astype(BF16)
    def gate(w_ref, b_ref):
        pre = jnp.concatenate([_dot(xb[:, c0:c0 + PAIR], w_ref[c0:c0 + PAIR, c0:c0 + PAIR])
                               for c0 in range(0, LRU_W, PAIR)], axis=1)
        return _sigmoid(pre + b_ref[...])

    r = gate(wa_ref, ba_ref)
    i = gate(wx_ref, bx_ref)
    yield
    lam = lam_ref[...]
    softplus_neg_lam = jnp.maximum(-lam, 0.0) + jnp.log1p(jnp.exp(-jnp.abs(lam)))
    log_a = (-LRU_C) * r * softplus_neg_lam
    a = jnp.exp(log_a)
    u = jnp.sqrt(-jnp.tanh(log_a) * (a * a + 1.0)) * (i * lxc)
    n = ts // SUBLANE
    hs, ps = [u[0:SUBLANE, :]], [a[0:SUBLANE, :]]
    for j in range(1, n):
        aj = a[j * SUBLANE:(j + 1) * SUBLANE, :]
        hs.append(aj * hs[-1] + u[j * SUBLANE:(j + 1) * SUBLANE, :])
        ps.append(aj * ps[-1])
    sub = lax.broadcasted_iota(jnp.int32, (SUBLANE, 1), 0)
    pt, ht = ps[-1], hs[-1]
    d = 1
    while d < SUBLANE:
        keep = sub >= d
        p_s = jnp.where(keep, pltpu.roll(pt, d, 0), 1.0)
        h_s = jnp.where(keep, pltpu.roll(ht, d, 0), 0.0)
        ht = pt * h_s + ht
        pt = pt * p_s
        d *= 2
    hprev = lru_h[0:1, :]
    block_end = ht + pt * hprev
    carry = jnp.where(sub == 0, hprev, pltpu.roll(block_end, 1, 0))
    lru_h[0:1, :] = block_end[SUBLANE - 1:SUBLANE, :]
    hfull = jnp.concatenate([hs[j] + ps[j] * carry for j in range(n)], axis=0)
    ly = _tiles(proj, O_LY // LANE, LRU_W // LANE)
    gelu = ly * (0.5 * (1.0 + jnp.tanh(np.float32(np.sqrt(2.0 / np.pi)) * (ly + 0.044715 * (ly * ly * ly)))))
    _set_tiles(obuf, 0, _permute_rows(unperm_ref, (hfull * gelu).astype(BF16)))
    yield


def _mlstm_chain(proj, mtail, qk, ibuf, fbuf, obuf, ml_c, ml_m, mcw_ref, mcb_ref, wqk_ref, bif_ref, tri_ref,
                 mnw_ref, unperm_ref, ts):
    mc_blocked = _silu(_causal_conv(mtail, _tiles(proj, O_MU // LANE, GROUP_W // LANE), mcw_ref, mcb_ref))
    mc = _permute_rows(unperm_ref, mc_blocked.astype(BF16)).astype(BF16)
    for h in range(ML_HEADS):
        qkh = _dot(mc[:, h * ML_DH:(h + 1) * ML_DH], wqk_ref[h])
        qk[h] = qkh[:, :ML_DH]
        qk[ML_HEADS + h] = qkh[:, ML_DH:] * (ML_DH ** -0.5)
    gates = proj[O_IF // LANE] + bif_ref[...]
    ibuf[...] = gates
    fbuf[...] = pltpu.roll(_log_sigmoid(gates), LANE - ML_HEADS, 1)
    yield

    row_i = lax.broadcasted_iota(jnp.int32, (ML_CHUNK, ML_CHUNK), 0)
    col_i = lax.broadcasted_iota(jnp.int32, (ML_CHUNK, ML_CHUNK), 1)
    causal = row_i >= col_i
    ones_col = jnp.where(col_i == 0, 1.0, 0.0).astype(BF16)

    for c in range(ts // ML_CHUNK):
        rows = pl.ds(c * ML_CHUNK, ML_CHUNK)
        f_hi, f_lo = _split_bf16(fbuf[rows, :])
        tri = tri_ref[...]
        fc = _dot(tri, f_hi) + _dot(tri, f_lo)
        yield
        dm = ibuf[rows, :] - fc
        dm_t = dm.T
        b_end = fc[ML_CHUNK - 1:ML_CHUNK, :]
        a_col = b_end + dm
        a_max = jnp.max(a_col, axis=0, keepdims=True)
        w_end = jnp.exp(a_col - a_max)
        m_old = ml_m[0:1, :]
        m_new = jnp.maximum(b_end + m_old, a_max)
        g_old = jnp.exp(b_end + m_old - m_new)
        g_new = jnp.exp(a_max - m_new)
        ml_m[0:1, :] = m_new
        s_cat, qc, vh = [], [], []
        for pr in range(ML_HEADS // 2):
            q2 = _tiles(qk, 2 * pr, 2, rows).astype(BF16)
            k2f = _tiles(qk, ML_HEADS + 2 * pr, 2, rows)
            k2 = k2f.astype(BF16)
            s_cat.append(_dot_nt(q2, _bdiag2(k2[:, :LANE], k2[:, LANE:])))
            for j in range(2):
                h = 2 * pr + j
                v = proj[O_MV // LANE + h, rows, :].astype(BF16)
                cx = ml_c[h]
                qc.append(_dot(q2[:, j * LANE:(j + 1) * LANE], cx.astype(BF16)))
                wk = (w_end[:, h:h + 1] * k2f[:, j * LANE:(j + 1) * LANE]).astype(BF16)
                updx = _dot_tn(wk, jnp.concatenate([v, ones_col], axis=1))
                ml_c[h] = g_old[:, h:h + 1] * cx + g_new[:, h:h + 1] * updx
                vh.append(v)
        yield
        mxs, sms, dens, nums = [], [], [], []
        for h in range(ML_HEADS):
            d_row = dm_t[h:h + 1, :]
            m_h = m_old[:, h:h + 1]
            cm = jnp.max(jnp.where(causal, d_row, -jnp.inf), axis=1, keepdims=True)
            mx = jnp.maximum(cm, m_h)
            p = jnp.where(causal, jnp.exp(d_row - mx), 0.0)
            s = s_cat[h // 2][:, (h % 2) * LANE:(h % 2 + 1) * LANE] * p
            mxs.append(mx)
            sms.append(s.astype(BF16))
            dens.append(jnp.sum(s, axis=1, keepdims=True))
        for pr in range(ML_HEADS // 2):
            nums.append(_dot(jnp.concatenate([sms[2 * pr], sms[2 * pr + 1]], axis=1),
                             _bdiag2(vh[2 * pr], vh[2 * pr + 1])))
        yield
        for h in range(ML_HEADS):
            m_h = m_old[:, h:h + 1]
            w_inter = jnp.exp(m_h - mxs[h])
            num = nums[h // 2][:, (h % 2) * LANE:(h % 2 + 1) * LANE] + w_inter * qc[h][:, :ML_DH]
            den = dens[h] + w_inter * qc[h][:, ML_DH:ML_DH + 1]
            m_t = fc[:, h:h + 1] + mxs[h]
            hh = num / jnp.maximum(jnp.abs(den), jnp.exp(-m_t))
            og = _sigmoid(proj[O_MO // LANE + h, rows, :])
            obuf[GROUP_W // LANE + h, rows, :] = _rms(og * hh) * mnw_ref[:, h * ML_DH:(h + 1) * ML_DH]
        yield


def _block_diag(w):
    g, n, _ = w.shape
    eye = jnp.eye(g, dtype=w.dtype)
    return (eye[:, None, :, None] * w[:, :, None, :]).reshape(g * n, g * n)


def _odd_mixer(h, norm_w, w_in_all, lcw, lcb, wa, ba, wx, bx, lam, mcw, mcb, wq, wk, bi, bf, ml_norm, w_out_all,
               layer, ts, grp):
    bsz, seq, _ = h.shape
    w_tail = w_in_all[layer][:, O_IF:]
    w_tail = jnp.pad(w_tail, ((0, 0), (0, O_NP - O_IF - w_tail.shape[1]))).astype(BF16)
    any_space = pl.BlockSpec(memory_space=pl.ANY)
    wqk = jnp.concatenate([wq, wk], axis=-1).astype(BF16)
    bif = jnp.pad(jnp.concatenate([bi, bf]), (0, LANE - 2 * ML_HEADS)).reshape(1, LANE)
    tri = jnp.tril(jnp.ones((ML_CHUNK, ML_CHUNK), F32)).astype(BF16)
    r = jnp.arange(ts)
    time_of_row = (r % SUBLANE) * (ts // SUBLANE) + r // SUBLANE
    perm = (time_of_row[:, None] == r[None, :]).astype(BF16)
    row = lambda t: t.reshape(1, -1)

    const2 = lambda shape: pl.BlockSpec(shape, lambda b, s: (0, 0), pipeline_mode=pl.Buffered(1))
    const3 = lambda shape: pl.BlockSpec(shape, lambda b, s: (0, 0, 0), pipeline_mode=pl.Buffered(1))
    return pl.pallas_call(
        functools.partial(_odd_body, layer),
        grid=(bsz // grp, seq // ts),
        in_specs=[
            pl.BlockSpec((grp, ts, D_MODEL), lambda b, s: (b, s, 0)),
            const2((1, D_MODEL)),
            any_space,
            const2((D_MODEL, LANE)),
            const2((CONV_W, LRU_W)), const2((1, LRU_W)),
            const2((CONV_W, GROUP_W)), const2((1, GROUP_W)),
            const2((LRU_W, LRU_W)), const2((1, LRU_W)),
            const2((LRU_W, LRU_W)), const2((1, LRU_W)),
            const2((1, LRU_W)),
            const3((ML_HEADS, ML_DH, 2 * ML_DH)),
            const2((1, LANE)),
            const2((ML_CHUNK, ML_CHUNK)),
            const2((1, GROUP_W)),
            any_space,
            const2((ts, ts)), const2((ts, ts)),
        ],
        out_specs=pl.BlockSpec((grp, ts, D_MODEL), lambda b, s: (b, s, 0)),
        out_shape=jax.ShapeDtypeStruct(h.shape, F32),
        scratch_shapes=[
            pltpu.VMEM((grp, O_NP // LANE, ts, LANE), F32),
            pltpu.VMEM((grp, (CONV_W - 1) * SUBLANE, LRU_W), F32),
            pltpu.VMEM((grp, (CONV_W - 1) * SUBLANE, GROUP_W), F32),
            pltpu.VMEM((grp, 2 * GROUP_W // LANE, ts, LANE), F32),
            pltpu.VMEM((grp, ts, LANE), F32),
            pltpu.VMEM((grp, ts, LANE), F32),
            pltpu.VMEM((grp, D_MODEL // LANE, ts, LANE), F32),
            pltpu.VMEM((grp, SUBLANE, LRU_W), F32),
            pltpu.VMEM((grp, ML_HEADS, ML_DH, 2 * ML_DH), F32),
            pltpu.VMEM((grp, SUBLANE, LANE), F32),
            pltpu.VMEM((D_MODEL, O_NP), BF16),
            pltpu.VMEM((D_MODEL, D_MODEL), BF16),
            pltpu.VMEM((MIX_STAGE_DEPTH, D_MODEL, MIX_SLAB), F32),
            pltpu.SemaphoreType.DMA((MIX_STAGE_DEPTH,)),
        ],
        compiler_params=pltpu.CompilerParams(
            dimension_semantics=("arbitrary", "arbitrary"), vmem_limit_bytes=VMEM_LIMIT),
        name="odd_mixer",
    )(h, row(norm_w), w_in_all, w_tail, lcw, row(lcb), mcw, row(mcb),
      _block_diag(wa).astype(BF16), row(ba), _block_diag(wx).astype(BF16), row(bx), row(lam),
      wqk, bif, tri, row(ml_norm), w_out_all, perm, perm.T)


def kernel(x, ffn1_norm, ffn1_wgu, ffn1_wd, mix_norm, ffn2_norm, ffn2_wgu, ffn2_wd, e_w_in, e_w_lr_up, e_b_lr, e_head_norm, e_w_out, o_w_in, o_lru_conv_w, o_lru_conv_b, o_lru_wa, o_lru_ba, o_lru_wx, o_lru_bx, o_lru_lambda, o_ml_conv_w, o_ml_conv_b, o_ml_wq, o_ml_wk, o_ml_bi, o_ml_bf, o_ml_norm, o_w_out, final_norm):
    bsz, seq, d = x.shape
    depth = ffn1_norm.shape[0]
    h = x
    for layer in range(depth):
        j = layer // 2
        h = _ffn(h.reshape(bsz * seq, d), ffn1_norm[layer], ffn1_wgu, ffn1_wd, layer,
                 final_norm, False).reshape(bsz, seq, d)
        if layer % 2 == 0:
            h = _even_mixer(h, mix_norm[layer], e_w_in, e_w_lr_up[j], e_b_lr[j], e_head_norm[j],
                            e_w_out, j, MIX_TS, EVEN_G)
        else:
            h = _odd_mixer(h, mix_norm[layer], o_w_in, o_lru_conv_w[j], o_lru_conv_b[j],
                           o_lru_wa[j], o_lru_ba[j], o_lru_wx[j], o_lru_bx[j], o_lru_lambda[j],
                           o_ml_conv_w[j], o_ml_conv_b[j], o_ml_wq[j], o_ml_wk[j], o_ml_bi[j], o_ml_bf[j],
                           o_ml_norm[j], o_w_out, j, MIX_TS, ODD_G)
        h = _ffn(h.reshape(bsz * seq, d), ffn2_norm[layer], ffn2_wgu, ffn2_wd, layer,
                 final_norm, layer == depth - 1).reshape(bsz, seq, d)
    return h
```

```python
import functools

import numpy as np
import jax
import jax.numpy as jnp
from jax import lax
from jax.experimental import pallas as pl
from jax.experimental.pallas import tpu as pltpu

F32 = jnp.float32
BF16 = jnp.bfloat16

D_MODEL = 1024
D_FF = 2816
GROUP_W = D_MODEL // 2
EPS = 1e-6

RET_HEADS = 4
RET_DK = 128
RET_CHUNK = 128
ROPE_BASE = 10000.0

GLA_HEADS = 4
GLA_DK = 64
GLA_DV = 128
GLA_RANK = 16
GLA_TAU = 16.0
GLA_CHUNK = 64
GLA_QK = GLA_HEADS * GLA_DK

LRU_W = GROUP_W
LRU_BLOCKS = 8
LRU_BS = LRU_W // LRU_BLOCKS
LRU_C = 8.0
CONV_W = 4

ML_HEADS = 4
ML_DH = 128
ML_CHUNK = 128

LANE = 128
SUBLANE = 8
VMEM_LIMIT = 56 * 1024 * 1024

E_RQ, E_RK, E_RV, E_RG = 0, 512, 1024, 1536
E_GQ, E_GK, E_GV, E_GG, E_LR = 2048, 2304, 2560, 3072, 3584
E_NP = 3712
O_LY, O_LX, O_MU, O_MV, O_MO, O_IF = 0, 512, 1024, 1536, 2048, 2560
O_NP = 2688

FFN_TM = 1024
FFN_SUB = 256
FFN_FC = 256
FFN_GU_ROWS = 64
FFN_D_ROWS = 256
STAGE_DEPTH = 4
MIX_STAGE_DEPTH = 3
MIX_TS = 256
EVEN_G = 4
ODD_G = 2
MIX_SLAB = 512
OUT_SLAB = 256
EVEN_LAG = 0
ODD_LAG = 6
PAIR = 2 * LANE


def _dot(a, b):
    return jnp.dot(a, b, preferred_element_type=F32)


def _dot_nt(a, b):
    return lax.dot_general(a, b, (((1,), (1,)), ((), ())), preferred_element_type=F32)


def _dot_tn(a, b):
    return lax.dot_general(a, b, (((0,), (0,)), ((), ())), preferred_element_type=F32)


def _rms(x):
    return x * lax.rsqrt(jnp.mean(x * x, axis=-1, keepdims=True) + EPS)


def _sigmoid(x):
    return 1.0 / (1.0 + jnp.exp(-x))


def _silu(x):
    return x * _sigmoid(x)


def _log_sigmoid(x):
    return jnp.minimum(x, 0.0) - jnp.log1p(jnp.exp(-jnp.abs(x)))


def _split_bf16(x):
    hi = x.astype(BF16)
    lo = (x - hi.astype(F32)).astype(BF16)
    return hi, lo


def _bdiag2(a, b):
    z = jnp.zeros_like(a)
    return jnp.concatenate([jnp.concatenate([a, z], axis=1), jnp.concatenate([z, b], axis=1)], axis=0)


def _round_robin(gens, lag=0):
    gens = list(gens)
    done = [False] * len(gens)
    rnd = 0
    while not all(done):
        for i, gen in enumerate(gens):
            if done[i] or rnd < lag * i:
                continue
            try:
                next(gen)
            except StopIteration:
                done[i] = True
        rnd += 1
        yield


def _run(gen):
    for _ in gen:
        pass


def _stage_weight(layer, w_hbm, w_bf, stage, sem, axis, width):
    n = w_bf.shape[axis] // width
    depth = stage.shape[0]

    def window(j):
        sl = pl.ds(j * width, width)
        return (sl, slice(None)) if axis == 0 else (slice(None), sl)

    def copy(j):
        return pltpu.make_async_copy(w_hbm.at[(layer,) + window(j)], stage.at[j % depth], sem.at[j % depth])

    for j in range(min(depth - 1, n)):
        copy(j).start()
    yield
    for j in range(n):
        if j + depth - 1 < n:
            copy(j + depth - 1).start()
        copy(j).wait()
        w_bf[window(j)] = stage[j % depth].astype(BF16)


def _stage_all(*streams):
    for s in streams:
        next(s)
    for s in streams:
        _run(s)


def _ffn_body(final, layer, x_ref, nw_ref, wgu_hbm, wd_hbm, fw_ref, o_ref,
              act_ref, wgu_ref, wd_ref, stage_gu, stage_d, sem_gu, sem_d):
    @pl.when(pl.program_id(0) == 0)
    def _():
        _stage_all(_stage_weight(layer, wgu_hbm, wgu_ref, stage_gu, sem_gu, 0, FFN_GU_ROWS),
                   _stage_weight(layer, wd_hbm, wd_ref, stage_d, sem_d, 0, FFN_D_ROWS))

    for r0 in range(0, FFN_TM, FFN_SUB):
        rows = pl.ds(r0, FFN_SUB)
        x = x_ref[rows, :]
        xn = (_rms(x) * nw_ref[...]).astype(BF16)
        for j in range(D_FF // FFN_FC):
            c0 = j * FFN_FC
            g = _dot(xn, wgu_ref[:, c0:c0 + FFN_FC])
            u = _dot(xn, wgu_ref[:, D_FF + c0:D_FF + c0 + FFN_FC])
            act_ref[rows, c0:c0 + FFN_FC] = (_silu(g) * u).astype(BF16)
        h = x + 0.5 * _dot(act_ref[rows, :], wd_ref[...])
        if final:
            h = _rms(h) * fw_ref[...]
        o_ref[rows, :] = h


def _ffn(h, norm_w, wgu_all, wd_all, layer, final_w, final):
    t = h.shape[0]
    const = lambda shape: pl.BlockSpec(shape, lambda i: (0, 0), pipeline_mode=pl.Buffered(1))
    return pl.pallas_call(
        functools.partial(_ffn_body, final, layer),
        grid=(t // FFN_TM,),
        in_specs=[
            pl.BlockSpec((FFN_TM, D_MODEL), lambda i: (i, 0)),
            const((1, D_MODEL)),
            pl.BlockSpec(memory_space=pl.ANY),
            pl.BlockSpec(memory_space=pl.ANY),
            const((1, D_MODEL)),
        ],
        out_specs=pl.BlockSpec((FFN_TM, D_MODEL), lambda i: (i, 0)),
        out_shape=jax.ShapeDtypeStruct((t, D_MODEL), F32),
        scratch_shapes=[
            pltpu.VMEM((FFN_TM, D_FF), BF16),
            pltpu.VMEM((D_MODEL, 2 * D_FF), BF16),
            pltpu.VMEM((D_FF, D_MODEL), BF16),
            pltpu.VMEM((STAGE_DEPTH, FFN_GU_ROWS, 2 * D_FF), F32),
            pltpu.VMEM((STAGE_DEPTH, FFN_D_ROWS, D_MODEL), F32),
            pltpu.SemaphoreType.DMA((STAGE_DEPTH,)),
            pltpu.SemaphoreType.DMA((STAGE_DEPTH,)),
        ],
        compiler_params=pltpu.CompilerParams(
            dimension_semantics=("arbitrary",), vmem_limit_bytes=VMEM_LIMIT),
        name="ffn_final" if final else "ffn",
    )(h, norm_w.reshape(1, -1), wgu_all, wd_all, final_w.reshape(1, -1))


def _even_body(layer, x_ref, nw_ref, win_hbm, wtail_ref, wlr_ref, blr_ref, cos_ref, sin_ref,
               dmat_ref, winb_ref, wendb_ref, dec_ref, tri_ref, ctile_ref,
               hnw_ref, wout_hbm, o_ref,
               proj, labuf, obuf, ret_s, gla_st, win_ref, wout_ref, wstage, wsem):
    groups = x_ref.shape[0]
    ts = x_ref.shape[1]

    @pl.when((pl.program_id(0) == 0) & (pl.program_id(1) == 0))
    def _():
        _stage_all(_stage_weight(layer, win_hbm, win_ref, wstage, wsem, 1, MIX_SLAB))
        win_ref[:, E_LR:E_NP] = wtail_ref[...]
        _stage_all(_stage_weight(layer, wout_hbm, wout_ref, wstage, wsem, 1, MIX_SLAB))

    @pl.when(pl.program_id(1) == 0)
    def _():
        ret_s[...] = jnp.zeros_like(ret_s)
        gla_st[...] = jnp.zeros_like(gla_st)

    def sequence(g):
        x = x_ref[g]
        pg = proj.at[g]
        hn = (_rms(x) * nw_ref[...]).astype(BF16)
        for c0 in range(0, E_NP, MIX_SLAB):
            c1 = min(c0 + MIX_SLAB, E_NP)
            pg[:, c0:c1] = _dot(hn, win_ref[:, c0:c1])
            yield
        glr = pg[:, E_LR:E_LR + LANE].astype(BF16)
        labuf[g] = _log_sigmoid(_dot(glr, wlr_ref[...]) + blr_ref[...]) * (1.0 / GLA_TAU)
        cos = cos_ref[...]
        sin = sin_ref[...]
        for c0 in range(0, 2 * RET_HEADS * RET_DK, RET_DK):
            xx = pg[:, c0:c0 + RET_DK]
            r = xx * cos + pltpu.roll(xx, RET_DK // 2, 1) * sin
            if c0 >= E_RK:
                r = r * (RET_DK ** -0.5)
            pg[:, c0:c0 + RET_DK] = r
        yield
        yield from _round_robin([
            _ret_chain(pg, obuf.at[g], ret_s.at[g], dmat_ref, winb_ref, wendb_ref, dec_ref, ts),
            _gla_chain(pg, labuf.at[g], obuf.at[g], gla_st.at[g], tri_ref, ctile_ref, ts)])
        o_ref[g] = x + _dot((obuf[g] * hnw_ref[...]).astype(BF16), wout_ref[...])
        yield

    _run(_round_robin([sequence(g) for g in range(groups)], lag=EVEN_LAG))


def _ret_chain(proj, obuf, ret_s, dmat_ref, winb_ref, wendb_ref, dec_ref, ts):
    for c in range(ts // RET_CHUNK):
        rows = pl.ds(c * RET_CHUNK, RET_CHUNK)
        stage = []
        for pr in range(RET_HEADS // 2):
            lo = pr * PAIR
            q2 = proj[rows, E_RQ + lo:E_RQ + lo + PAIR].astype(BF16)
            k2 = proj[rows, E_RK + lo:E_RK + lo + PAIR].astype(BF16)
            v2 = proj[rows, E_RV + lo:E_RV + lo + PAIR]
            s0 = ret_s[2 * pr]
            s1 = ret_s[2 * pr + 1]
            s_cat = _dot_nt(q2, _bdiag2(k2[:, :LANE], k2[:, LANE:]))
            inter = _dot(q2, _bdiag2(s0.astype(BF16), s1.astype(BF16)))
            full = _dot_tn(k2, (wendb_ref[pr] * v2).astype(BF16))
            ret_s[2 * pr] = dec_ref[2 * pr] * s0 + full[:LANE, :LANE]
            ret_s[2 * pr + 1] = dec_ref[2 * pr + 1] * s1 + full[LANE:, LANE:]
            stage.append((s_cat, inter, v2.astype(BF16)))
        yield
        outs = []
        for pr in range(RET_HEADS // 2):
            s_cat, inter, vb = stage[pr]
            p = (s_cat * dmat_ref[pr]).astype(BF16)
            outs.append(_dot(p, _bdiag2(vb[:, :LANE], vb[:, LANE:])) + inter * winb_ref[pr])
        yield
        for h in range(RET_HEADS):
            o = outs[h // 2][:, (h % 2) * LANE:(h % 2 + 1) * LANE]
            g = proj[rows, E_RG + h * RET_DK:E_RG + (h + 1) * RET_DK]
            obuf[rows, h * RET_DK:(h + 1) * RET_DK] = _rms(o) * _silu(g)
        yield


def _gla_chain(proj, labuf, obuf, gla_st, tri_ref, ctile_ref, ts):
    lane_head = lax.broadcasted_iota(jnp.int32, (1, GLA_QK), 1) // GLA_DK

    def stack_heads(z):
        return jnp.concatenate(
            [jnp.where(lane_head == h, z, 0.0) for h in range(GLA_HEADS)], axis=0).astype(BF16)

    zb = jnp.zeros((GLA_CHUNK, GLA_DV), BF16)
    for c in range(ts // GLA_CHUNK):
        rows = pl.ds(c * GLA_CHUNK, GLA_CHUNK)
        la_hi, la_lo = _split_bf16(labuf[rows, :])
        tri = tri_ref[...]
        b = _dot(tri, jnp.concatenate([la_hi, la_lo], axis=0))
        yield
        b_mid = b[GLA_CHUNK // 2 - 1:GLA_CHUNK // 2, :]
        b_end = b[GLA_CHUNK - 1:GLA_CHUNK, :]
        q = proj[rows, E_GQ:E_GQ + GLA_QK]
        k = proj[rows, E_GK:E_GK + GLA_QK] * (GLA_DK ** -0.5)
        vb = proj[rows, E_GV:E_GV + GLA_HEADS * GLA_DV].astype(BF16)
        vh = [vb[:, h * GLA_DV:(h + 1) * GLA_DV] for h in range(GLA_HEADS)]
        s_cat = _dot_nt((q * jnp.exp(b - b_mid)).astype(BF16), stack_heads(k * jnp.exp(b_mid - b)))
        st = gla_st[...]
        inter = _dot_nt(stack_heads(q * jnp.exp(b)), st.astype(BF16))
        upd = _dot_tn(jnp.concatenate(vh, axis=0), stack_heads(k * jnp.exp(b_end - b)))
        gla_st[...] = jnp.exp(b_end) * st + upd
        yield
        p = (s_cat * ctile_ref[...]).astype(BF16)
        vbd = jnp.concatenate(
            [jnp.concatenate([vh[h] if j == h else zb for j in range(GLA_HEADS)], axis=1)
             for h in range(GLA_HEADS)], axis=0)
        out_cat = _dot(p, vbd)
        yield
        for h in range(GLA_HEADS):
            g = proj[rows, E_GG + h * GLA_DV:E_GG + (h + 1) * GLA_DV]
            o = out_cat[:, h * GLA_DV:(h + 1) * GLA_DV] + inter[h * GLA_CHUNK:(h + 1) * GLA_CHUNK, :]
            obuf[rows, GROUP_W + h * GLA_DV:GROUP_W + (h + 1) * GLA_DV] = _rms(o) * _silu(g)
        yield


def _pair_lanes(t):
    return np.concatenate([t[0::2], t[1::2]], axis=-1)


def _even_mixer(h, norm_w, w_in_all, w_lr, b_lr, head_norm, w_out_all, layer, ts, grp):
    bsz, seq, _ = h.shape
    f32 = np.float32
    hh = np.arange(RET_HEADS, dtype=f32)
    log_g = np.log1p(-np.exp2(f32(-5.0) - hh)).astype(f32)
    idx = np.arange(RET_CHUNK, dtype=f32)
    rel = idx[:, None] - idx[None, :]
    causal = rel >= 0
    dmat = np.where(causal, np.exp(log_g[:, None, None] * np.where(causal, rel, f32(0.0))), f32(0.0)).astype(f32)
    full = (RET_HEADS, RET_CHUNK, RET_DK)
    pair = (RET_HEADS // 2, RET_CHUNK, PAIR)
    w_end = np.broadcast_to(np.exp(log_g[:, None] * (f32(RET_CHUNK - 1.0) - idx)[None, :])[:, :, None], full)
    w_inb = np.broadcast_to(np.exp(log_g[:, None] * (idx + f32(1.0))[None, :])[:, :, None], full)
    dec = np.ascontiguousarray(np.broadcast_to(np.exp(log_g * f32(RET_CHUNK))[:, None, None], full), dtype=f32)
    w_end = np.ascontiguousarray(w_end, dtype=f32)
    w_inb = np.ascontiguousarray(w_inb, dtype=f32)

    half = RET_DK // 2
    inv = (f32(ROPE_BASE) ** (-np.arange(half, dtype=f32) / f32(half))).astype(f32)
    ang = np.arange(seq).astype(f32)[:, None] * inv[None, :]
    cos2 = np.concatenate([np.cos(ang), np.cos(ang)], axis=-1).astype(f32)
    sin2 = np.concatenate([-np.sin(ang), np.sin(ang)], axis=-1).astype(f32)

    tril = np.tril(np.ones((GLA_CHUNK, GLA_CHUNK), f32))
    ctile = np.tile(tril, (1, GLA_HEADS))

    w_tail = w_in_all[layer][:, E_LR:]
    w_tail = jnp.pad(w_tail, ((0, 0), (0, E_NP - E_LR - w_tail.shape[1]))).astype(BF16)
    w_lr_p = jnp.pad(w_lr, ((0, LANE - GLA_RANK), (0, 0))).astype(BF16)
    any_space = pl.BlockSpec(memory_space=pl.ANY)

    const2 = lambda shape: pl.BlockSpec(shape, lambda b, s: (0, 0), pipeline_mode=pl.Buffered(1))
    const3 = lambda shape: pl.BlockSpec(shape, lambda b, s: (0, 0, 0), pipeline_mode=pl.Buffered(1))
    return pl.pallas_call(
        functools.partial(_even_body, layer),
        grid=(bsz // grp, seq // ts),
        in_specs=[
            pl.BlockSpec((grp, ts, D_MODEL), lambda b, s: (b, s, 0)),
            const2((1, D_MODEL)),
            any_space,
            const2((D_MODEL, LANE)),
            const2((LANE, GLA_QK)),
            const2((1, GLA_QK)),
            pl.BlockSpec((ts, RET_DK), lambda b, s: (s, 0)),
            pl.BlockSpec((ts, RET_DK), lambda b, s: (s, 0)),
            const3(pair), const3(pair), const3(pair), const3(full),
            const2((GLA_CHUNK, 2 * GLA_CHUNK)),
            const2((GLA_CHUNK, GLA_QK)),
            const2((1, D_MODEL)),
            any_space,
        ],
        out_specs=pl.BlockSpec((grp, ts, D_MODEL), lambda b, s: (b, s, 0)),
        out_shape=jax.ShapeDtypeStruct(h.shape, F32),
        scratch_shapes=[
            pltpu.VMEM((grp, ts, E_NP), F32),
            pltpu.VMEM((grp, ts, GLA_QK), F32),
            pltpu.VMEM((grp, ts, D_MODEL), F32),
            pltpu.VMEM((grp,) + full, F32),
            pltpu.VMEM((grp, GLA_DV, GLA_QK), F32),
            pltpu.VMEM((D_MODEL, E_NP), BF16),
            pltpu.VMEM((D_MODEL, D_MODEL), BF16),
            pltpu.VMEM((MIX_STAGE_DEPTH, D_MODEL, MIX_SLAB), F32),
            pltpu.SemaphoreType.DMA((MIX_STAGE_DEPTH,)),
        ],
        compiler_params=pltpu.CompilerParams(
            dimension_semantics=("arbitrary", "arbitrary"), vmem_limit_bytes=VMEM_LIMIT),
        name="even_mixer",
    )(h, norm_w.reshape(1, -1), w_in_all, w_tail, w_lr_p, b_lr.reshape(1, -1), cos2, sin2,
      _pair_lanes(dmat), _pair_lanes(w_inb), _pair_lanes(w_end), dec, np.tile(tril, (1, 2)).astype(BF16), ctile,
      head_norm.reshape(1, -1), w_out_all)


def _tiles(ref, t0, n, rows=slice(None)):
    return jnp.concatenate([ref[t0 + t, rows, :] for t in range(n)], axis=1)


def _set_tiles(ref, t0, val):
    for t in range(val.shape[1] // LANE):
        ref[t0 + t] = val[:, t * LANE:(t + 1) * LANE]


def _permute_rows(perm_ref, x_bf16):
    return _dot(perm_ref[...], x_bf16)


def _causal_conv(tail, xb, w_ref, b_ref):
    ts = xb.shape[0]
    keep = (CONV_W - 1) * SUBLANE
    last = xb[ts - keep:, :]
    prev = tail[...]
    first = lax.broadcasted_iota(jnp.int32, (SUBLANE, 1), 0) == 0
    fix = [jnp.where(first, pltpu.roll(prev[m * SUBLANE:(m + 1) * SUBLANE, :], 1, 0),
                     pltpu.roll(last[m * SUBLANE:(m + 1) * SUBLANE, :], 1, 0)) for m in range(CONV_W - 1)]
    tail[...] = last
    xpad = jnp.concatenate(fix + [xb], axis=0)
    acc = b_ref[...] + w_ref[CONV_W - 1:CONV_W, :] * xb
    for j in range(CONV_W - 1):
        acc = acc + w_ref[j:j + 1, :] * xpad[j * SUBLANE:j * SUBLANE + ts, :]
    return acc


def _odd_body(layer, x_ref, nw_ref, win_hbm, wtail_ref, lcw_ref, lcb_ref, mcw_ref, mcb_ref,
              wa_ref, ba_ref, wx_ref, bx_ref, lam_ref, wqk_ref, bif_ref, tri_ref,
              mnw_ref, wout_hbm, perm_ref, unperm_ref, o_ref,
              proj, ltail, mtail, qk, ibuf, fbuf, obuf, lru_h, ml_c, ml_m, win_ref, wout_ref, wstage, wsem):
    groups = x_ref.shape[0]
    ts = x_ref.shape[1]

    @pl.when((pl.program_id(0) == 0) & (pl.program_id(1) == 0))
    def _():
        _stage_all(_stage_weight(layer, win_hbm, win_ref, wstage, wsem, 1, MIX_SLAB))
        win_ref[:, O_IF:O_NP] = wtail_ref[...]
        _stage_all(_stage_weight(layer, wout_hbm, wout_ref, wstage, wsem, 1, MIX_SLAB))

    @pl.when(pl.program_id(1) == 0)
    def _():
        ltail[...] = jnp.zeros_like(ltail)
        mtail[...] = jnp.zeros_like(mtail)
        lru_h[...] = jnp.zeros_like(lru_h)
        ml_c[...] = jnp.zeros_like(ml_c)
        ml_m[...] = jnp.zeros_like(ml_m)

    def sequence(g):
        x = x_ref[g]
        hn = (_rms(x) * nw_ref[...]).astype(BF16)
        hn_blocked = _permute_rows(perm_ref, hn).astype(BF16)
        for c0 in range(0, O_NP, 512):
            c1 = min(c0 + 512, O_NP)
            lhs = hn_blocked if c1 <= O_MV else hn
            _set_tiles(proj.at[g], c0 // LANE, _dot(lhs, win_ref[:, c0:c1]))
            yield
        yield from _round_robin([
            _lru_chain(proj.at[g], ltail.at[g], obuf.at[g], lru_h.at[g],
                       lcw_ref, lcb_ref, wa_ref, ba_ref, wx_ref, bx_ref, lam_ref, unperm_ref, ts),
            _mlstm_chain(proj.at[g], mtail.at[g], qk.at[g], ibuf.at[g], fbuf.at[g], obuf.at[g],
                         ml_c.at[g], ml_m.at[g], mcw_ref, mcb_ref, wqk_ref, bif_ref, tri_ref, mnw_ref,
                         unperm_ref, ts)])
        ob = _tiles(obuf.at[g], 0, D_MODEL // LANE).astype(BF16)
        for c0 in range(0, D_MODEL, OUT_SLAB):
            o_ref[g, :, c0:c0 + OUT_SLAB] = x[:, c0:c0 + OUT_SLAB] + _dot(ob, wout_ref[:, c0:c0 + OUT_SLAB])
            yield

    _run(_round_robin([sequence(g) for g in range(groups)], lag=ODD_LAG))


def _lru_chain(proj, ltail, obuf, lru_h, lcw_ref, lcb_ref, wa_ref, ba_ref, wx_ref, bx_ref, lam_ref, unperm_ref, ts):
    lxc = _causal_conv(ltail, _tiles(proj, O_LX // LANE, LRU_W // LANE), lcw_ref, lcb_ref)
    xb = lxc.astype(BF16)
    def gate(w_ref, b_ref):
        pre = jnp.concatenate([_dot(xb[:, c0:c0 + PAIR], w_ref[c0:c0 + PAIR, c0:c0 + PAIR])
                               for c0 in range(0, LRU_W, PAIR)], axis=1)
        return _sigmoid(pre + b_ref[...])

    r = gate(wa_ref, ba_ref)
    i = gate(wx_ref, bx_ref)
    yield
    lam = lam_ref[...]
    softplus_neg_lam = jnp.maximum(-lam, 0.0) + jnp.log1p(jnp.exp(-jnp.abs(lam)))
    log_a = (-LRU_C) * r * softplus_neg_lam
    a = jnp.exp(log_a)
    u = jnp.sqrt(-jnp.tanh(log_a) * (a * a + 1.0)) * (i * lxc)
    n = ts // SUBLANE
    hs, ps = [u[0:SUBLANE, :]], [a[0:SUBLANE, :]]
    for j in range(1, n):
        aj = a[j * SUBLANE:(j + 1) * SUBLANE, :]
        hs.append(aj * hs[-1] + u[j * SUBLANE:(j + 1) * SUBLANE, :])
        ps.append(aj * ps[-1])
    sub = lax.broadcasted_iota(jnp.int32, (SUBLANE, 1), 0)
    pt, ht = ps[-1], hs[-1]
    d = 1
    while d < SUBLANE:
        keep = sub >= d
        p_s = jnp.where(keep, pltpu.roll(pt, d, 0), 1.0)
        h_s = jnp.where(keep, pltpu.roll(ht, d, 0), 0.0)
        ht = pt * h_s + ht
        pt = pt * p_s
        d *= 2
    hprev = lru_h[0:1, :]
    block_end = ht + pt * hprev
    carry = jnp.where(sub == 0, hprev, pltpu.roll(block_end, 1, 0))
    lru_h[0:1, :] = block_end[SUBLANE - 1:SUBLANE, :]
    hfull = jnp.concatenate([hs[j] + ps[j] * carry for j in range(n)], axis=0)
    ly = _tiles(proj, O_LY // LANE, LRU_W // LANE)
    gelu = ly * (0.5 * (1.0 + jnp.tanh(np.float32(np.sqrt(2.0 / np.pi)) * (ly + 0.044715 * (ly * ly * ly)))))
    _set_tiles(obuf, 0, _permute_rows(unperm_ref, (hfull * gelu).astype(BF16)))
    yield


def _mlstm_chain(proj, mtail, qk, ibuf, fbuf, obuf, ml_c, ml_m, mcw_ref, mcb_ref, wqk_ref, bif_ref, tri_ref,
                 mnw_ref, unperm_ref, ts):
    mc_blocked = _silu(_causal_conv(mtail, _tiles(proj, O_MU // LANE, GROUP_W // LANE), mcw_ref, mcb_ref))
    mc = _permute_rows(unperm_ref, mc_blocked.astype(BF16)).astype(BF16)
    for h in range(ML_HEADS):
        qkh = _dot(mc[:, h * ML_DH:(h + 1) * ML_DH], wqk_ref[h])
        qk[h] = qkh[:, :ML_DH]
        qk[ML_HEADS + h] = qkh[:, ML_DH:] * (ML_DH ** -0.5)
    gates = proj[O_IF // LANE] + bif_ref[...]
    ibuf[...] = gates
    fbuf[...] = pltpu.roll(_log_sigmoid(gates), LANE - ML_HEADS, 1)
    yield

    row_i = lax.broadcasted_iota(jnp.int32, (ML_CHUNK, ML_CHUNK), 0)
    col_i = lax.broadcasted_iota(jnp.int32, (ML_CHUNK, ML_CHUNK), 1)
    causal = row_i >= col_i
    ones_col = jnp.where(col_i == 0, 1.0, 0.0).astype(BF16)

    for c in range(ts // ML_CHUNK):
        rows = pl.ds(c * ML_CHUNK, ML_CHUNK)
        f_hi, f_lo = _split_bf16(fbuf[rows, :])
        tri = tri_ref[...]
        fc = _dot(tri, f_hi) + _dot(tri, f_lo)
        yield
        dm = ibuf[rows, :] - fc
        dm_t = dm.T
        b_end = fc[ML_CHUNK - 1:ML_CHUNK, :]
        a_col = b_end + dm
        a_max = jnp.max(a_col, axis=0, keepdims=True)
        w_end = jnp.exp(a_col - a_max)
        m_old = ml_m[0:1, :]
        m_new = jnp.maximum(b_end + m_old, a_max)
        g_old = jnp.exp(b_end + m_old - m_new)
        g_new = jnp.exp(a_max - m_new)
        ml_m[0:1, :] = m_new
        s_cat, qc, vh = [], [], []
        for pr in range(ML_HEADS // 2):
            q2 = _tiles(qk, 2 * pr, 2, rows).astype(BF16)
            k2f = _tiles(qk, ML_HEADS + 2 * pr, 2, rows)
            k2 = k2f.astype(BF16)
            s_cat.append(_dot_nt(q2, _bdiag2(k2[:, :LANE], k2[:, LANE:])))
            for j in range(2):
                h = 2 * pr + j
                v = proj[O_MV // LANE + h, rows, :].astype(BF16)
                cx = ml_c[h]
                qc.append(_dot(q2[:, j * LANE:(j + 1) * LANE], cx.astype(BF16)))
                wk = (w_end[:, h:h + 1] * k2f[:, j * LANE:(j + 1) * LANE]).astype(BF16)
                updx = _dot_tn(wk, jnp.concatenate([v, ones_col], axis=1))
                ml_c[h] = g_old[:, h:h + 1] * cx + g_new[:, h:h + 1] * updx
                vh.append(v)
        yield
        mxs, sms, dens, nums = [], [], [], []
        for h in range(ML_HEADS):
            d_row = dm_t[h:h + 1, :]
            m_h = m_old[:, h:h + 1]
            cm = jnp.max(jnp.where(causal, d_row, -jnp.inf), axis=1, keepdims=True)
            mx = jnp.maximum(cm, m_h)
            p = jnp.where(causal, jnp.exp(d_row - mx), 0.0)
            s = s_cat[h // 2][:, (h % 2) * LANE:(h % 2 + 1) * LANE] * p
            mxs.append(mx)
            sms.append(s.astype(BF16))
            dens.append(jnp.sum(s, axis=1, keepdims=True))
        for pr in range(ML_HEADS // 2):
            nums.append(_dot(jnp.concatenate([sms[2 * pr], sms[2 * pr + 1]], axis=1),
                             _bdiag2(vh[2 * pr], vh[2 * pr + 1])))
        yield
        for h in range(ML_HEADS):
            m_h = m_old[:, h:h + 1]
            w_inter = jnp.exp(m_h - mxs[h])
            num = nums[h // 2][:, (h % 2) * LANE:(h % 2 + 1) * LANE] + w_inter * qc[h][:, :ML_DH]
            den = dens[h] + w_inter * qc[h][:, ML_DH:ML_DH + 1]
            m_t = fc[:, h:h + 1] + mxs[h]
            hh = num / jnp.maximum(jnp.abs(den), jnp.exp(-m_t))
            og = _sigmoid(proj[O_MO // LANE + h, rows, :])
            obuf[GROUP_W // LANE + h, rows, :] = _rms(og * hh) * mnw_ref[:, h * ML_DH:(h + 1) * ML_DH]
        yield


def _block_diag(w):
    g, n, _ = w.shape
    eye = jnp.eye(g, dtype=w.dtype)
    return (eye[:, None, :, None] * w[:, :, None, :]).reshape(g * n, g * n)


def _odd_mixer(h, norm_w, w_in_all, lcw, lcb, wa, ba, wx, bx, lam, mcw, mcb, wq, wk, bi, bf, ml_norm, w_out_all,
               layer, ts, grp):
    bsz, seq, _ = h.shape
    w_tail = w_in_all[layer][:, O_IF:]
    w_tail = jnp.pad(w_tail, ((0, 0), (0, O_NP - O_IF - w_tail.shape[1]))).astype(BF16)
    any_space = pl.BlockSpec(memory_space=pl.ANY)
    wqk = jnp.concatenate([wq, wk], axis=-1).astype(BF16)
    bif = jnp.pad(jnp.concatenate([bi, bf]), (0, LANE - 2 * ML_HEADS)).reshape(1, LANE)
    tri = np.tril(np.ones((ML_CHUNK, ML_CHUNK), np.float32)).astype(BF16)
    r = np.arange(ts)
    time_of_row = (r % SUBLANE) * (ts // SUBLANE) + r // SUBLANE
    perm = (time_of_row[:, None] == r[None, :]).astype(BF16)
    row = lambda t: t.reshape(1, -1)

    const2 = lambda shape: pl.BlockSpec(shape, lambda b, s: (0, 0), pipeline_mode=pl.Buffered(1))
    const3 = lambda shape: pl.BlockSpec(shape, lambda b, s: (0, 0, 0), pipeline_mode=pl.Buffered(1))
    return pl.pallas_call(
        functools.partial(_odd_body, layer),
        grid=(bsz // grp, seq // ts),
        in_specs=[
            pl.BlockSpec((grp, ts, D_MODEL), lambda b, s: (b, s, 0)),
            const2((1, D_MODEL)),
            any_space,
            const2((D_MODEL, LANE)),
            const2((CONV_W, LRU_W)), const2((1, LRU_W)),
            const2((CONV_W, GROUP_W)), const2((1, GROUP_W)),
            const2((LRU_W, LRU_W)), const2((1, LRU_W)),
            const2((LRU_W, LRU_W)), const2((1, LRU_W)),
            const2((1, LRU_W)),
            const3((ML_HEADS, ML_DH, 2 * ML_DH)),
            const2((1, LANE)),
            const2((ML_CHUNK, ML_CHUNK)),
            const2((1, GROUP_W)),
            any_space,
            const2((ts, ts)), const2((ts, ts)),
        ],
        out_specs=pl.BlockSpec((grp, ts, D_MODEL), lambda b, s: (b, s, 0)),
        out_shape=jax.ShapeDtypeStruct(h.shape, F32),
        scratch_shapes=[
            pltpu.VMEM((grp, O_NP // LANE, ts, LANE), F32),
            pltpu.VMEM((grp, (CONV_W - 1) * SUBLANE, LRU_W), F32),
            pltpu.VMEM((grp, (CONV_W - 1) * SUBLANE, GROUP_W), F32),
            pltpu.VMEM((grp, 2 * GROUP_W // LANE, ts, LANE), F32),
            pltpu.VMEM((grp, ts, LANE), F32),
            pltpu.VMEM((grp, ts, LANE), F32),
            pltpu.VMEM((grp, D_MODEL // LANE, ts, LANE), F32),
            pltpu.VMEM((grp, SUBLANE, LRU_W), F32),
            pltpu.VMEM((grp, ML_HEADS, ML_DH, 2 * ML_DH), F32),
            pltpu.VMEM((grp, SUBLANE, LANE), F32),
            pltpu.VMEM((D_MODEL, O_NP), BF16),
            pltpu.VMEM((D_MODEL, D_MODEL), BF16),
            pltpu.VMEM((MIX_STAGE_DEPTH, D_MODEL, MIX_SLAB), F32),
            pltpu.SemaphoreType.DMA((MIX_STAGE_DEPTH,)),
        ],
        compiler_params=pltpu.CompilerParams(
            dimension_semantics=("arbitrary", "arbitrary"), vmem_limit_bytes=VMEM_LIMIT),
        name="odd_mixer",
    )(h, row(norm_w), w_in_all, w_tail, lcw, row(lcb), mcw, row(mcb),
      _block_diag(wa).astype(BF16), row(ba), _block_diag(wx).astype(BF16), row(bx), row(lam),
      wqk, bif, tri, row(ml_norm), w_out_all, perm, perm.T)


def kernel(x, ffn1_norm, ffn1_wgu, ffn1_wd, mix_norm, ffn2_norm, ffn2_wgu, ffn2_wd, e_w_in, e_w_lr_up, e_b_lr, e_head_norm, e_w_out, o_w_in, o_lru_conv_w, o_lru_conv_b, o_lru_wa, o_lru_ba, o_lru_wx, o_lru_bx, o_lru_lambda, o_ml_conv_w, o_ml_conv_b, o_ml_wq, o_ml_wk, o_ml_bi, o_ml_bf, o_ml_norm, o_w_out, final_norm):
    bsz, seq, d = x.shape
    depth = ffn1_norm.shape[0]
    h = x
    for layer in range(depth):
        j = layer // 2
        h = _ffn(h.reshape(bsz * seq, d), ffn1_norm[layer], ffn1_wgu, ffn1_wd, layer,
                 final_norm, False).reshape(bsz, seq, d)
        if layer % 2 == 0:
            h = _even_mixer(h, mix_norm[layer], e_w_in, e_w_lr_up[j], e_b_lr[j], e_head_norm[j],
                            e_w_out, j, MIX_TS, EVEN_G)
        else:
            h = _odd_mixer(h, mix_norm[layer], o_w_in, o_lru_conv_w[j], o_lru_conv_b[j],
                           o_lru_wa[j], o_lru_ba[j], o_lru_wx[j], o_lru_bx[j], o_lru_lambda[j],
                           o_ml_conv_w[j], o_ml_conv_b[j], o_ml_wq[j], o_ml_wk[j], o_ml_bi[j], o_ml_bf[j],
                           o_ml_norm[j], o_w_out, j, MIX_TS, ODD_G)
        h = _ffn(h.reshape(bsz * seq, d), ffn2_norm[layer], ffn2_wgu, ffn2_wd, layer,
                 final_norm, layer == depth - 1).reshape(bsz, seq, d)
    return h
```

```python
import functools

import numpy as np
import jax
import jax.numpy as jnp
from jax import lax
from jax.experimental import pallas as pl
from jax.experimental.pallas import tpu as pltpu

F32 = jnp.float32
BF16 = jnp.bfloat16

D_MODEL = 1024
D_FF = 2816
GROUP_W = D_MODEL // 2
EPS = 1e-6

RET_HEADS = 4
RET_DK = 128
RET_CHUNK = 128
ROPE_BASE = 10000.0

GLA_HEADS = 4
GLA_DK = 64
GLA_DV = 128
GLA_RANK = 16
GLA_TAU = 16.0
GLA_CHUNK = 64
GLA_QK = GLA_HEADS * GLA_DK

LRU_W = GROUP_W
LRU_BLOCKS = 8
LRU_BS = LRU_W // LRU_BLOCKS
LRU_C = 8.0
CONV_W = 4

ML_HEADS = 4
ML_DH = 128
ML_CHUNK = 128

LANE = 128
SUBLANE = 8
VMEM_LIMIT = 56 * 1024 * 1024

E_RQ, E_RK, E_RV, E_RG = 0, 512, 1024, 1536
E_GQ, E_GK, E_GV, E_GG, E_LR = 2048, 2304, 2560, 3072, 3584
E_NP = 3712
O_LY, O_LX, O_MU, O_MV, O_MO, O_IF = 0, 512, 1024, 1536, 2048, 2560
O_NP = 2688

FFN_TM = 1024
FFN_SUB = 256
FFN_FC = 256
FFN_GU_ROWS = 64
FFN_D_ROWS = 256
STAGE_DEPTH = 4
MIX_TS = 256
EVEN_G = 4
ODD_G = 2
MIX_SLAB = 512
OUT_SLAB = 256
EVEN_LAG = 0
ODD_LAG = 6
PAIR = 2 * LANE


def _dot(a, b):
    return jnp.dot(a, b, preferred_element_type=F32)


def _dot_nt(a, b):
    return lax.dot_general(a, b, (((1,), (1,)), ((), ())), preferred_element_type=F32)


def _dot_tn(a, b):
    return lax.dot_general(a, b, (((0,), (0,)), ((), ())), preferred_element_type=F32)


def _rms(x):
    return x * lax.rsqrt(jnp.mean(x * x, axis=-1, keepdims=True) + EPS)


def _sigmoid(x):
    return 1.0 / (1.0 + jnp.exp(-x))


def _silu(x):
    return x * _sigmoid(x)


def _log_sigmoid(x):
    return jnp.minimum(x, 0.0) - jnp.log1p(jnp.exp(-jnp.abs(x)))


def _split_bf16(x):
    hi = x.astype(BF16)
    lo = (x - hi.astype(F32)).astype(BF16)
    return hi, lo


def _bdiag2(a, b):
    z = jnp.zeros_like(a)
    return jnp.concatenate([jnp.concatenate([a, z], axis=1), jnp.concatenate([z, b], axis=1)], axis=0)


def _round_robin(gens, lag=0):
    gens = list(gens)
    done = [False] * len(gens)
    rnd = 0
    while not all(done):
        for i, gen in enumerate(gens):
            if done[i] or rnd < lag * i:
                continue
            try:
                next(gen)
            except StopIteration:
                done[i] = True
        rnd += 1
        yield


def _run(gen):
    for _ in gen:
        pass


def _stage_weight(layer, w_hbm, w_bf, stage, sem, axis, width):
    n = w_bf.shape[axis] // width
    depth = stage.shape[0]

    def window(j):
        sl = pl.ds(j * width, width)
        return (sl, slice(None)) if axis == 0 else (slice(None), sl)

    def copy(j):
        return pltpu.make_async_copy(w_hbm.at[(layer,) + window(j)], stage.at[j % depth], sem.at[j % depth])

    for j in range(min(depth - 1, n)):
        copy(j).start()
    yield
    for j in range(n):
        if j + depth - 1 < n:
            copy(j + depth - 1).start()
        copy(j).wait()
        w_bf[window(j)] = stage[j % depth].astype(BF16)


def _stage_all(*streams):
    for s in streams:
        next(s)
    for s in streams:
        _run(s)


def _ffn_body(final, layer, x_ref, nw_ref, wgu_hbm, wd_hbm, fw_ref, o_ref,
              act_ref, wgu_ref, wd_ref, stage_gu, stage_d, sem_gu, sem_d):
    @pl.when(pl.program_id(0) == 0)
    def _():
        _stage_all(_stage_weight(layer, wgu_hbm, wgu_ref, stage_gu, sem_gu, 0, FFN_GU_ROWS),
                   _stage_weight(layer, wd_hbm, wd_ref, stage_d, sem_d, 0, FFN_D_ROWS))

    for r0 in range(0, FFN_TM, FFN_SUB):
        rows = pl.ds(r0, FFN_SUB)
        x = x_ref[rows, :]
        xn = (_rms(x) * nw_ref[...]).astype(BF16)
        for j in range(D_FF // FFN_FC):
            c0 = j * FFN_FC
            g = _dot(xn, wgu_ref[:, c0:c0 + FFN_FC])
            u = _dot(xn, wgu_ref[:, D_FF + c0:D_FF + c0 + FFN_FC])
            act_ref[rows, c0:c0 + FFN_FC] = (_silu(g) * u).astype(BF16)
        h = x + 0.5 * _dot(act_ref[rows, :], wd_ref[...])
        if final:
            h = _rms(h) * fw_ref[...]
        o_ref[rows, :] = h


def _ffn(h, norm_w, wgu_all, wd_all, layer, final_w, final):
    t = h.shape[0]
    const = lambda shape: pl.BlockSpec(shape, lambda i: (0, 0), pipeline_mode=pl.Buffered(1))
    return pl.pallas_call(
        functools.partial(_ffn_body, final, layer),
        grid=(t // FFN_TM,),
        in_specs=[
            pl.BlockSpec((FFN_TM, D_MODEL), lambda i: (i, 0)),
            const((1, D_MODEL)),
            pl.BlockSpec(memory_space=pl.ANY),
            pl.BlockSpec(memory_space=pl.ANY),
            const((1, D_MODEL)),
        ],
        out_specs=pl.BlockSpec((FFN_TM, D_MODEL), lambda i: (i, 0)),
        out_shape=jax.ShapeDtypeStruct((t, D_MODEL), F32),
        scratch_shapes=[
            pltpu.VMEM((FFN_TM, D_FF), BF16),
            pltpu.VMEM((D_MODEL, 2 * D_FF), BF16),
            pltpu.VMEM((D_FF, D_MODEL), BF16),
            pltpu.VMEM((STAGE_DEPTH, FFN_GU_ROWS, 2 * D_FF), F32),
            pltpu.VMEM((STAGE_DEPTH, FFN_D_ROWS, D_MODEL), F32),
            pltpu.SemaphoreType.DMA((STAGE_DEPTH,)),
            pltpu.SemaphoreType.DMA((STAGE_DEPTH,)),
        ],
        compiler_params=pltpu.CompilerParams(
            dimension_semantics=("arbitrary",), vmem_limit_bytes=VMEM_LIMIT),
        name="ffn_final" if final else "ffn",
    )(h, norm_w.reshape(1, -1), wgu_all, wd_all, final_w.reshape(1, -1))


def _even_body(x_ref, nw_ref, win_ref, wtail_ref, wlr_ref, blr_ref, cos_ref, sin_ref,
               dmat_ref, winb_ref, wendb_ref, dec_ref, tri_ref, ctile_ref,
               hnw_ref, wout_ref, o_ref,
               proj, labuf, obuf, ret_s, gla_st):
    groups = x_ref.shape[0]
    ts = x_ref.shape[1]

    @pl.when(pl.program_id(1) == 0)
    def _():
        ret_s[...] = jnp.zeros_like(ret_s)
        gla_st[...] = jnp.zeros_like(gla_st)

    def sequence(g):
        x = x_ref[g]
        pg = proj.at[g]
        hn = (_rms(x) * nw_ref[...]).astype(BF16)
        for c0 in range(0, E_LR, MIX_SLAB):
            pg[:, c0:c0 + MIX_SLAB] = _dot(hn, win_ref[:, c0:c0 + MIX_SLAB])
            yield
        pg[:, E_LR:E_NP] = _dot(hn, wtail_ref[...])
        yield
        glr = pg[:, E_LR:E_LR + LANE].astype(BF16)
        labuf[g] = _log_sigmoid(_dot(glr, wlr_ref[...]) + blr_ref[...]) * (1.0 / GLA_TAU)
        cos = cos_ref[...]
        sin = sin_ref[...]
        for c0 in range(0, 2 * RET_HEADS * RET_DK, RET_DK):
            xx = pg[:, c0:c0 + RET_DK]
            r = xx * cos + pltpu.roll(xx, RET_DK // 2, 1) * sin
            if c0 >= E_RK:
                r = r * (RET_DK ** -0.5)
            pg[:, c0:c0 + RET_DK] = r
        yield
        yield from _round_robin([
            _ret_chain(pg, obuf.at[g], ret_s.at[g], dmat_ref, winb_ref, wendb_ref, dec_ref, ts),
            _gla_chain(pg, labuf.at[g], obuf.at[g], gla_st.at[g], tri_ref, ctile_ref, ts)])
        o_ref[g] = x + _dot((obuf[g] * hnw_ref[...]).astype(BF16), wout_ref[...])
        yield

    _run(_round_robin([sequence(g) for g in range(groups)], lag=EVEN_LAG))


def _ret_chain(proj, obuf, ret_s, dmat_ref, winb_ref, wendb_ref, dec_ref, ts):
    for c in range(ts // RET_CHUNK):
        rows = pl.ds(c * RET_CHUNK, RET_CHUNK)
        stage = []
        for pr in range(RET_HEADS // 2):
            lo = pr * PAIR
            q2 = proj[rows, E_RQ + lo:E_RQ + lo + PAIR].astype(BF16)
            k2 = proj[rows, E_RK + lo:E_RK + lo + PAIR].astype(BF16)
            v2 = proj[rows, E_RV + lo:E_RV + lo + PAIR]
            s0 = ret_s[2 * pr]
            s1 = ret_s[2 * pr + 1]
            s_cat = _dot_nt(q2, _bdiag2(k2[:, :LANE], k2[:, LANE:]))
            inter = _dot(q2, _bdiag2(s0.astype(BF16), s1.astype(BF16)))
            full = _dot_tn(k2, (wendb_ref[pr] * v2).astype(BF16))
            ret_s[2 * pr] = dec_ref[2 * pr] * s0 + full[:LANE, :LANE]
            ret_s[2 * pr + 1] = dec_ref[2 * pr + 1] * s1 + full[LANE:, LANE:]
            stage.append((s_cat, inter, v2.astype(BF16)))
        yield
        outs = []
        for pr in range(RET_HEADS // 2):
            s_cat, inter, vb = stage[pr]
            p = (s_cat * dmat_ref[pr]).astype(BF16)
            outs.append(_dot(p, _bdiag2(vb[:, :LANE], vb[:, LANE:])) + inter * winb_ref[pr])
        yield
        for h in range(RET_HEADS):
            o = outs[h // 2][:, (h % 2) * LANE:(h % 2 + 1) * LANE]
            g = proj[rows, E_RG + h * RET_DK:E_RG + (h + 1) * RET_DK]
            obuf[rows, h * RET_DK:(h + 1) * RET_DK] = _rms(o) * _silu(g)
        yield


def _gla_chain(proj, labuf, obuf, gla_st, tri_ref, ctile_ref, ts):
    lane_head = lax.broadcasted_iota(jnp.int32, (1, GLA_QK), 1) // GLA_DK

    def stack_heads(z):
        return jnp.concatenate(
            [jnp.where(lane_head == h, z, 0.0) for h in range(GLA_HEADS)], axis=0).astype(BF16)

    zb = jnp.zeros((GLA_CHUNK, GLA_DV), BF16)
    for c in range(ts // GLA_CHUNK):
        rows = pl.ds(c * GLA_CHUNK, GLA_CHUNK)
        la_hi, la_lo = _split_bf16(labuf[rows, :])
        tri = tri_ref[...]
        b = _dot(tri, jnp.concatenate([la_hi, la_lo], axis=0))
        yield
        b_mid = b[GLA_CHUNK // 2 - 1:GLA_CHUNK // 2, :]
        b_end = b[GLA_CHUNK - 1:GLA_CHUNK, :]
        q = proj[rows, E_GQ:E_GQ + GLA_QK]
        k = proj[rows, E_GK:E_GK + GLA_QK] * (GLA_DK ** -0.5)
        vb = proj[rows, E_GV:E_GV + GLA_HEADS * GLA_DV].astype(BF16)
        vh = [vb[:, h * GLA_DV:(h + 1) * GLA_DV] for h in range(GLA_HEADS)]
        s_cat = _dot_nt((q * jnp.exp(b - b_mid)).astype(BF16), stack_heads(k * jnp.exp(b_mid - b)))
        st = gla_st[...]
        inter = _dot_nt(stack_heads(q * jnp.exp(b)), st.astype(BF16))
        upd = _dot_tn(jnp.concatenate(vh, axis=0), stack_heads(k * jnp.exp(b_end - b)))
        gla_st[...] = jnp.exp(b_end) * st + upd
        yield
        p = (s_cat * ctile_ref[...]).astype(BF16)
        vbd = jnp.concatenate(
            [jnp.concatenate([vh[h] if j == h else zb for j in range(GLA_HEADS)], axis=1)
             for h in range(GLA_HEADS)], axis=0)
        out_cat = _dot(p, vbd)
        yield
        for h in range(GLA_HEADS):
            g = proj[rows, E_GG + h * GLA_DV:E_GG + (h + 1) * GLA_DV]
            o = out_cat[:, h * GLA_DV:(h + 1) * GLA_DV] + inter[h * GLA_CHUNK:(h + 1) * GLA_CHUNK, :]
            obuf[rows, GROUP_W + h * GLA_DV:GROUP_W + (h + 1) * GLA_DV] = _rms(o) * _silu(g)
        yield


def _pair_lanes(t):
    return np.concatenate([t[0::2], t[1::2]], axis=-1)


def _even_mixer(h, norm_w, w_in, w_lr, b_lr, head_norm, w_out, ts, grp):
    bsz, seq, _ = h.shape
    f32 = np.float32
    hh = np.arange(RET_HEADS, dtype=f32)
    log_g = np.log1p(-np.exp2(f32(-5.0) - hh)).astype(f32)
    idx = np.arange(RET_CHUNK, dtype=f32)
    rel = idx[:, None] - idx[None, :]
    causal = rel >= 0
    dmat = np.where(causal, np.exp(log_g[:, None, None] * np.where(causal, rel, f32(0.0))), f32(0.0)).astype(f32)
    full = (RET_HEADS, RET_CHUNK, RET_DK)
    pair = (RET_HEADS // 2, RET_CHUNK, PAIR)
    w_end = np.broadcast_to(np.exp(log_g[:, None] * (f32(RET_CHUNK - 1.0) - idx)[None, :])[:, :, None], full)
    w_inb = np.broadcast_to(np.exp(log_g[:, None] * (idx + f32(1.0))[None, :])[:, :, None], full)
    dec = np.ascontiguousarray(np.broadcast_to(np.exp(log_g * f32(RET_CHUNK))[:, None, None], full), dtype=f32)
    w_end = np.ascontiguousarray(w_end, dtype=f32)
    w_inb = np.ascontiguousarray(w_inb, dtype=f32)

    half = RET_DK // 2
    inv = (f32(ROPE_BASE) ** (-np.arange(half, dtype=f32) / f32(half))).astype(f32)
    ang = np.arange(seq).astype(f32)[:, None] * inv[None, :]
    cos2 = np.concatenate([np.cos(ang), np.cos(ang)], axis=-1).astype(f32)
    sin2 = np.concatenate([-np.sin(ang), np.sin(ang)], axis=-1).astype(f32)

    tril = np.tril(np.ones((GLA_CHUNK, GLA_CHUNK), f32))
    ctile = np.tile(tril, (1, GLA_HEADS))

    w_main = w_in[:, :E_LR].astype(BF16)
    w_tail = w_in[:, E_LR:]
    w_tail = jnp.pad(w_tail, ((0, 0), (0, E_NP - E_LR - w_tail.shape[1]))).astype(BF16)
    w_lr_p = jnp.pad(w_lr, ((0, LANE - GLA_RANK), (0, 0))).astype(BF16)

    const2 = lambda shape: pl.BlockSpec(shape, lambda b, s: (0, 0), pipeline_mode=pl.Buffered(1))
    const3 = lambda shape: pl.BlockSpec(shape, lambda b, s: (0, 0, 0), pipeline_mode=pl.Buffered(1))
    return pl.pallas_call(
        _even_body,
        grid=(bsz // grp, seq // ts),
        in_specs=[
            pl.BlockSpec((grp, ts, D_MODEL), lambda b, s: (b, s, 0)),
            const2((1, D_MODEL)),
            const2((D_MODEL, E_LR)),
            const2((D_MODEL, LANE)),
            const2((LANE, GLA_QK)),
            const2((1, GLA_QK)),
            pl.BlockSpec((ts, RET_DK), lambda b, s: (s, 0)),
            pl.BlockSpec((ts, RET_DK), lambda b, s: (s, 0)),
            const3(pair), const3(pair), const3(pair), const3(full),
            const2((GLA_CHUNK, 2 * GLA_CHUNK)),
            const2((GLA_CHUNK, GLA_QK)),
            const2((1, D_MODEL)),
            const2((D_MODEL, D_MODEL)),
        ],
        out_specs=pl.BlockSpec((grp, ts, D_MODEL), lambda b, s: (b, s, 0)),
        out_shape=jax.ShapeDtypeStruct(h.shape, F32),
        scratch_shapes=[
            pltpu.VMEM((grp, ts, E_NP), F32),
            pltpu.VMEM((grp, ts, GLA_QK), F32),
            pltpu.VMEM((grp, ts, D_MODEL), F32),
            pltpu.VMEM((grp,) + full, F32),
            pltpu.VMEM((grp, GLA_DV, GLA_QK), F32),
        ],
        compiler_params=pltpu.CompilerParams(
            dimension_semantics=("parallel", "arbitrary"), vmem_limit_bytes=VMEM_LIMIT),
        name="even_mixer",
    )(h, norm_w.reshape(1, -1), w_main, w_tail, w_lr_p, b_lr.reshape(1, -1), cos2, sin2,
      _pair_lanes(dmat), _pair_lanes(w_inb), _pair_lanes(w_end), dec, np.tile(tril, (1, 2)).astype(BF16), ctile,
      head_norm.reshape(1, -1), w_out.astype(BF16))


def _tiles(ref, t0, n, rows=slice(None)):
    return jnp.concatenate([ref[t0 + t, rows, :] for t in range(n)], axis=1)


def _set_tiles(ref, t0, val):
    for t in range(val.shape[1] // LANE):
        ref[t0 + t] = val[:, t * LANE:(t + 1) * LANE]


def _permute_rows(perm_ref, x_bf16):
    return _dot(perm_ref[...], x_bf16)


def _causal_conv(tail, xb, w_ref, b_ref):
    ts = xb.shape[0]
    keep = (CONV_W - 1) * SUBLANE
    last = xb[ts - keep:, :]
    prev = tail[...]
    first = lax.broadcasted_iota(jnp.int32, (SUBLANE, 1), 0) == 0
    fix = [jnp.where(first, pltpu.roll(prev[m * SUBLANE:(m + 1) * SUBLANE, :], 1, 0),
                     pltpu.roll(last[m * SUBLANE:(m + 1) * SUBLANE, :], 1, 0)) for m in range(CONV_W - 1)]
    tail[...] = last
    xpad = jnp.concatenate(fix + [xb], axis=0)
    acc = b_ref[...] + w_ref[CONV_W - 1:CONV_W, :] * xb
    for j in range(CONV_W - 1):
        acc = acc + w_ref[j:j + 1, :] * xpad[j * SUBLANE:j * SUBLANE + ts, :]
    return acc


def _odd_body(x_ref, nw_ref, win_ref, wtail_ref, lcw_ref, lcb_ref, mcw_ref, mcb_ref,
              wa_ref, ba_ref, wx_ref, bx_ref, lam_ref, wqk_ref, bif_ref, tri_ref,
              mnw_ref, wout_ref, perm_ref, unperm_ref, o_ref,
              proj, ltail, mtail, qk, ibuf, fbuf, obuf, lru_h, ml_c, ml_m):
    groups = x_ref.shape[0]
    ts = x_ref.shape[1]

    @pl.when(pl.program_id(1) == 0)
    def _():
        ltail[...] = jnp.zeros_like(ltail)
        mtail[...] = jnp.zeros_like(mtail)
        lru_h[...] = jnp.zeros_like(lru_h)
        ml_c[...] = jnp.zeros_like(ml_c)
        ml_m[...] = jnp.zeros_like(ml_m)

    def sequence(g):
        x = x_ref[g]
        hn = (_rms(x) * nw_ref[...]).astype(BF16)
        hn_blocked = _permute_rows(perm_ref, hn).astype(BF16)
        for c0 in range(0, O_IF, MIX_SLAB):
            lhs = hn_blocked if c0 + MIX_SLAB <= O_MV else hn
            _set_tiles(proj.at[g], c0 // LANE, _dot(lhs, win_ref[:, c0:c0 + MIX_SLAB]))
            yield
        proj[g, O_IF // LANE] = _dot(hn, wtail_ref[...])
        yield
        yield from _round_robin([
            _lru_chain(proj.at[g], ltail.at[g], obuf.at[g], lru_h.at[g],
                       lcw_ref, lcb_ref, wa_ref, ba_ref, wx_ref, bx_ref, lam_ref, unperm_ref, ts),
            _mlstm_chain(proj.at[g], mtail.at[g], qk.at[g], ibuf.at[g], fbuf.at[g], obuf.at[g],
                         ml_c.at[g], ml_m.at[g], mcw_ref, mcb_ref, wqk_ref, bif_ref, tri_ref, mnw_ref,
                         unperm_ref, ts)])
        ob = _tiles(obuf.at[g], 0, D_MODEL // LANE).astype(BF16)
        for c0 in range(0, D_MODEL, OUT_SLAB):
            o_ref[g, :, c0:c0 + OUT_SLAB] = x[:, c0:c0 + OUT_SLAB] + _dot(ob, wout_ref[:, c0:c0 + OUT_SLAB])
            yield

    _run(_round_robin([sequence(g) for g in range(groups)], lag=ODD_LAG))


def _lru_chain(proj, ltail, obuf, lru_h, lcw_ref, lcb_ref, wa_ref, ba_ref, wx_ref, bx_ref, lam_ref, unperm_ref, ts):
    lxc = _causal_conv(ltail, _tiles(proj, O_LX // LANE, LRU_W // LANE), lcw_ref, lcb_ref)
    xb = lxc.astype(BF16)
    def gate(w_ref, b_ref):
        pre = jnp.concatenate([_dot(xb[:, c0:c0 + PAIR], w_ref[c0:c0 + PAIR, c0:c0 + PAIR])
                               for c0 in range(0, LRU_W, PAIR)], axis=1)
        return _sigmoid(pre + b_ref[...])

    r = gate(wa_ref, ba_ref)
    i = gate(wx_ref, bx_ref)
    yield
    lam = lam_ref[...]
    softplus_neg_lam = jnp.maximum(-lam, 0.0) + jnp.log1p(jnp.exp(-jnp.abs(lam)))
    log_a = (-LRU_C) * r * softplus_neg_lam
    a = jnp.exp(log_a)
    u = jnp.sqrt(-jnp.tanh(log_a) * (a * a + 1.0)) * (i * lxc)
    n = ts // SUBLANE
    hs, ps = [u[0:SUBLANE, :]], [a[0:SUBLANE, :]]
    for j in range(1, n):
        aj = a[j * SUBLANE:(j + 1) * SUBLANE, :]
        hs.append(aj * hs[-1] + u[j * SUBLANE:(j + 1) * SUBLANE, :])
        ps.append(aj * ps[-1])
    sub = lax.broadcasted_iota(jnp.int32, (SUBLANE, 1), 0)
    pt, ht = ps[-1], hs[-1]
    d = 1
    while d < SUBLANE:
        keep = sub >= d
        p_s = jnp.where(keep, pltpu.roll(pt, d, 0), 1.0)
        h_s = jnp.where(keep, pltpu.roll(ht, d, 0), 0.0)
        ht = pt * h_s + ht
        pt = pt * p_s
        d *= 2
    hprev = lru_h[0:1, :]
    block_end = ht + pt * hprev
    carry = jnp.where(sub == 0, hprev, pltpu.roll(block_end, 1, 0))
    lru_h[0:1, :] = block_end[SUBLANE - 1:SUBLANE, :]
    hfull = jnp.concatenate([hs[j] + ps[j] * carry for j in range(n)], axis=0)
    ly = _tiles(proj, O_LY // LANE, LRU_W // LANE)
    gelu = ly * (0.5 * (1.0 + jnp.tanh(np.float32(np.sqrt(2.0 / np.pi)) * (ly + 0.044715 * (ly * ly * ly)))))
    _set_tiles(obuf, 0, _permute_rows(unperm_ref, (hfull * gelu).astype(BF16)))
    yield


def _mlstm_chain(proj, mtail, qk, ibuf, fbuf, obuf, ml_c, ml_m, mcw_ref, mcb_ref, wqk_ref, bif_ref, tri_ref,
                 mnw_ref, unperm_ref, ts):
    mc_blocked = _silu(_causal_conv(mtail, _tiles(proj, O_MU // LANE, GROUP_W // LANE), mcw_ref, mcb_ref))
    mc = _permute_rows(unperm_ref, mc_blocked.astype(BF16)).astype(BF16)
    for h in range(ML_HEADS):
        qkh = _dot(mc[:, h * ML_DH:(h + 1) * ML_DH], wqk_ref[h])
        qk[h] = qkh[:, :ML_DH]
        qk[ML_HEADS + h] = qkh[:, ML_DH:] * (ML_DH ** -0.5)
    gates = proj[O_IF // LANE] + bif_ref[...]
    ibuf[...] = gates
    fbuf[...] = pltpu.roll(_log_sigmoid(gates), LANE - ML_HEADS, 1)
    yield

    row_i = lax.broadcasted_iota(jnp.int32, (ML_CHUNK, ML_CHUNK), 0)
    col_i = lax.broadcasted_iota(jnp.int32, (ML_CHUNK, ML_CHUNK), 1)
    causal = row_i >= col_i
    ones_col = jnp.where(col_i == 0, 1.0, 0.0).astype(BF16)

    for c in range(ts // ML_CHUNK):
        rows = pl.ds(c * ML_CHUNK, ML_CHUNK)
        f_hi, f_lo = _split_bf16(fbuf[rows, :])
        tri = tri_ref[...]
        fc = _dot(tri, f_hi) + _dot(tri, f_lo)
        yield
        dm = ibuf[rows, :] - fc
        dm_t = dm.T
        b_end = fc[ML_CHUNK - 1:ML_CHUNK, :]
        a_col = b_end + dm
        a_max = jnp.max(a_col, axis=0, keepdims=True)
        w_end = jnp.exp(a_col - a_max)
        m_old = ml_m[0:1, :]
        m_new = jnp.maximum(b_end + m_old, a_max)
        g_old = jnp.exp(b_end + m_old - m_new)
        g_new = jnp.exp(a_max - m_new)
        ml_m[0:1, :] = m_new
        s_cat, qc, vh = [], [], []
        for pr in range(ML_HEADS // 2):
            q2 = _tiles(qk, 2 * pr, 2, rows).astype(BF16)
            k2f = _tiles(qk, ML_HEADS + 2 * pr, 2, rows)
            k2 = k2f.astype(BF16)
            s_cat.append(_dot_nt(q2, _bdiag2(k2[:, :LANE], k2[:, LANE:])))
            for j in range(2):
                h = 2 * pr + j
                v = proj[O_MV // LANE + h, rows, :].astype(BF16)
                cx = ml_c[h]
                qc.append(_dot(q2[:, j * LANE:(j + 1) * LANE], cx.astype(BF16)))
                wk = (w_end[:, h:h + 1] * k2f[:, j * LANE:(j + 1) * LANE]).astype(BF16)
                updx = _dot_tn(wk, jnp.concatenate([v, ones_col], axis=1))
                ml_c[h] = g_old[:, h:h + 1] * cx + g_new[:, h:h + 1] * updx
                vh.append(v)
        yield
        mxs, sms, dens, nums = [], [], [], []
        for h in range(ML_HEADS):
            d_row = dm_t[h:h + 1, :]
            m_h = m_old[:, h:h + 1]
            cm = jnp.max(jnp.where(causal, d_row, -jnp.inf), axis=1, keepdims=True)
            mx = jnp.maximum(cm, m_h)
            p = jnp.where(causal, jnp.exp(d_row - mx), 0.0)
            s = s_cat[h // 2][:, (h % 2) * LANE:(h % 2 + 1) * LANE] * p
            mxs.append(mx)
            sms.append(s.astype(BF16))
            dens.append(jnp.sum(s, axis=1, keepdims=True))
        for pr in range(ML_HEADS // 2):
            nums.append(_dot(jnp.concatenate([sms[2 * pr], sms[2 * pr + 1]], axis=1),
                             _bdiag2(vh[2 * pr], vh[2 * pr + 1])))
        yield
        for h in range(ML_HEADS):
            m_h = m_old[:, h:h + 1]
            w_inter = jnp.exp(m_h - mxs[h])
            num = nums[h // 2][:, (h % 2) * LANE:(h % 2 + 1) * LANE] + w_inter * qc[h][:, :ML_DH]
            den = dens[h] + w_inter * qc[h][:, ML_DH:ML_DH + 1]
            m_t = fc[:, h:h + 1] + mxs[h]
            hh = num / jnp.maximum(jnp.abs(den), jnp.exp(-m_t))
            og = _sigmoid(proj[O_MO // LANE + h, rows, :])
            obuf[GROUP_W // LANE + h, rows, :] = _rms(og * hh) * mnw_ref[:, h * ML_DH:(h + 1) * ML_DH]
        yield


def _block_diag(w):
    g, n, _ = w.shape
    eye = jnp.eye(g, dtype=w.dtype)
    return (eye[:, None, :, None] * w[:, :, None, :]).reshape(g * n, g * n)


def _odd_mixer(h, norm_w, w_in, lcw, lcb, wa, ba, wx, bx, lam, mcw, mcb, wq, wk, bi, bf, ml_norm, w_out, ts, grp):
    bsz, seq, _ = h.shape
    w_main = w_in[:, :O_IF].astype(BF16)
    w_tail = w_in[:, O_IF:]
    w_tail = jnp.pad(w_tail, ((0, 0), (0, O_NP - O_IF - w_tail.shape[1]))).astype(BF16)
    wqk = jnp.concatenate([wq, wk], axis=-1).astype(BF16)
    bif = jnp.pad(jnp.concatenate([bi, bf]), (0, LANE - 2 * ML_HEADS)).reshape(1, LANE)
    tri = np.tril(np.ones((ML_CHUNK, ML_CHUNK), np.float32)).astype(BF16)
    r = np.arange(ts)
    time_of_row = (r % SUBLANE) * (ts // SUBLANE) + r // SUBLANE
    perm = (time_of_row[:, None] == r[None, :]).astype(BF16)
    row = lambda t: t.reshape(1, -1)

    const2 = lambda shape: pl.BlockSpec(shape, lambda b, s: (0, 0), pipeline_mode=pl.Buffered(1))
    const3 = lambda shape: pl.BlockSpec(shape, lambda b, s: (0, 0, 0), pipeline_mode=pl.Buffered(1))
    return pl.pallas_call(
        _odd_body,
        grid=(bsz // grp, seq // ts),
        in_specs=[
            pl.BlockSpec((grp, ts, D_MODEL), lambda b, s: (b, s, 0)),
            const2((1, D_MODEL)),
            const2((D_MODEL, O_IF)),
            const2((D_MODEL, LANE)),
            const2((CONV_W, LRU_W)), const2((1, LRU_W)),
            const2((CONV_W, GROUP_W)), const2((1, GROUP_W)),
            const2((LRU_W, LRU_W)), const2((1, LRU_W)),
            const2((LRU_W, LRU_W)), const2((1, LRU_W)),
            const2((1, LRU_W)),
            const3((ML_HEADS, ML_DH, 2 * ML_DH)),
            const2((1, LANE)),
            const2((ML_CHUNK, ML_CHUNK)),
            const2((1, GROUP_W)),
            const2((D_MODEL, D_MODEL)),
            const2((ts, ts)), const2((ts, ts)),
        ],
        out_specs=pl.BlockSpec((grp, ts, D_MODEL), lambda b, s: (b, s, 0)),
        out_shape=jax.ShapeDtypeStruct(h.shape, F32),
        scratch_shapes=[
            pltpu.VMEM((grp, O_NP // LANE, ts, LANE), F32),
            pltpu.VMEM((grp, (CONV_W - 1) * SUBLANE, LRU_W), F32),
            pltpu.VMEM((grp, (CONV_W - 1) * SUBLANE, GROUP_W), F32),
            pltpu.VMEM((grp, 2 * GROUP_W // LANE, ts, LANE), F32),
            pltpu.VMEM((grp, ts, LANE), F32),
            pltpu.VMEM((grp, ts, LANE), F32),
            pltpu.VMEM((grp, D_MODEL // LANE, ts, LANE), F32),
            pltpu.VMEM((grp, SUBLANE, LRU_W), F32),
            pltpu.VMEM((grp, ML_HEADS, ML_DH, 2 * ML_DH), F32),
            pltpu.VMEM((grp, SUBLANE, LANE), F32),
        ],
        compiler_params=pltpu.CompilerParams(
            dimension_semantics=("parallel", "arbitrary"), vmem_limit_bytes=VMEM_LIMIT),
        name="odd_mixer",
    )(h, row(norm_w), w_main, w_tail, lcw, row(lcb), mcw, row(mcb),
      _block_diag(wa).astype(BF16), row(ba), _block_diag(wx).astype(BF16), row(bx), row(lam),
      wqk, bif, tri, row(ml_norm), w_out.astype(BF16), perm, perm.T)


def kernel(x, ffn1_norm, ffn1_wgu, ffn1_wd, mix_norm, ffn2_norm, ffn2_wgu, ffn2_wd, e_w_in, e_w_lr_up, e_b_lr, e_head_norm, e_w_out, o_w_in, o_lru_conv_w, o_lru_conv_b, o_lru_wa, o_lru_ba, o_lru_wx, o_lru_bx, o_lru_lambda, o_ml_conv_w, o_ml_conv_b, o_ml_wq, o_ml_wk, o_ml_bi, o_ml_bf, o_ml_norm, o_w_out, final_norm):
    bsz, seq, d = x.shape
    depth = ffn1_norm.shape[0]
    h = x
    for layer in range(depth):
        j = layer // 2
        h = _ffn(h.reshape(bsz * seq, d), ffn1_norm[layer], ffn1_wgu, ffn1_wd, layer,
                 final_norm, False).reshape(bsz, seq, d)
        if layer % 2 == 0:
            h = _even_mixer(h, mix_norm[layer], e_w_in[j], e_w_lr_up[j], e_b_lr[j], e_head_norm[j],
                            e_w_out[j], MIX_TS, EVEN_G)
        else:
            h = _odd_mixer(h, mix_norm[layer], o_w_in[j], o_lru_conv_w[j], o_lru_conv_b[j],
                           o_lru_wa[j], o_lru_ba[j], o_lru_wx[j], o_lru_bx[j], o_lru_lambda[j],
                           o_ml_conv_w[j], o_ml_conv_b[j], o_ml_wq[j], o_ml_wk[j], o_ml_bi[j], o_ml_bf[j],
                           o_ml_norm[j], o_w_out[j], MIX_TS, ODD_G)
        h = _ffn(h.reshape(bsz * seq, d), ffn2_norm[layer], ffn2_wgu, ffn2_wd, layer,
                 final_norm, layer == depth - 1).reshape(bsz, seq, d)
    return h
```

```python
import functools

import numpy as np
import jax
import jax.numpy as jnp
from jax import lax
from jax.experimental import pallas as pl
from jax.experimental.pallas import tpu as pltpu

F32 = jnp.float32
BF16 = jnp.bfloat16

D_MODEL = 1024
D_FF = 2816
GROUP_W = D_MODEL // 2
EPS = 1e-6

RET_HEADS = 4
RET_DK = 128
RET_CHUNK = 128
ROPE_BASE = 10000.0

GLA_HEADS = 4
GLA_DK = 64
GLA_DV = 128
GLA_RANK = 16
GLA_TAU = 16.0
GLA_CHUNK = 64
GLA_QK = GLA_HEADS * GLA_DK

LRU_W = GROUP_W
LRU_BLOCKS = 8
LRU_BS = LRU_W // LRU_BLOCKS
LRU_C = 8.0
CONV_W = 4

ML_HEADS = 4
ML_DH = 128
ML_CHUNK = 128

LANE = 128
SUBLANE = 8
VMEM_LIMIT = 56 * 1024 * 1024

E_RQ, E_RK, E_RV, E_RG = 0, 512, 1024, 1536
E_GQ, E_GK, E_GV, E_GG, E_LR = 2048, 2304, 2560, 3072, 3584
E_NP = 3712
O_LY, O_LX, O_MU, O_MV, O_MO, O_IF = 0, 512, 1024, 1536, 2048, 2560
O_NP = 2688

FFN_TM = 1024
FFN_SUB = 256
FFN_FC = 256
FFN_GU_ROWS = 64
FFN_D_ROWS = 256
STAGE_DEPTH = 4
MIX_TS = 256
EVEN_G = 4
ODD_G = 4
MIX_SLAB = 512
OUT_SLAB = 256
EVEN_LAG = 4
ODD_LAG = 5
PAIR = 2 * LANE


def _dot(a, b):
    return jnp.dot(a, b, preferred_element_type=F32)


def _dot_nt(a, b):
    return lax.dot_general(a, b, (((1,), (1,)), ((), ())), preferred_element_type=F32)


def _dot_tn(a, b):
    return lax.dot_general(a, b, (((0,), (0,)), ((), ())), preferred_element_type=F32)


def _rms(x):
    return x * lax.rsqrt(jnp.mean(x * x, axis=-1, keepdims=True) + EPS)


def _sigmoid(x):
    return 1.0 / (1.0 + jnp.exp(-x))


def _silu(x):
    return x * _sigmoid(x)


def _log_sigmoid(x):
    return jnp.minimum(x, 0.0) - jnp.log1p(jnp.exp(-jnp.abs(x)))


def _split_bf16(x):
    hi = x.astype(BF16)
    lo = (x - hi.astype(F32)).astype(BF16)
    return hi, lo


def _bdiag2(a, b):
    z = jnp.zeros_like(a)
    return jnp.concatenate([jnp.concatenate([a, z], axis=1), jnp.concatenate([z, b], axis=1)], axis=0)


def _round_robin(gens, lag=0):
    gens = list(gens)
    done = [False] * len(gens)
    rnd = 0
    while not all(done):
        for i, gen in enumerate(gens):
            if done[i] or rnd < lag * i:
                continue
            try:
                next(gen)
            except StopIteration:
                done[i] = True
        rnd += 1
        yield


def _run(gen):
    for _ in gen:
        pass


def _stage_weight(layer, w_hbm, w_bf, stage, sem, axis, width):
    n = w_bf.shape[axis] // width
    depth = stage.shape[0]

    def window(j):
        sl = pl.ds(j * width, width)
        return (sl, slice(None)) if axis == 0 else (slice(None), sl)

    def copy(j):
        return pltpu.make_async_copy(w_hbm.at[(layer,) + window(j)], stage.at[j % depth], sem.at[j % depth])

    for j in range(min(depth - 1, n)):
        copy(j).start()
    yield
    for j in range(n):
        if j + depth - 1 < n:
            copy(j + depth - 1).start()
        copy(j).wait()
        w_bf[window(j)] = stage[j % depth].astype(BF16)


def _stage_all(*streams):
    for s in streams:
        next(s)
    for s in streams:
        _run(s)


def _ffn_body(final, layer, x_ref, nw_ref, wgu_hbm, wd_hbm, fw_ref, o_ref,
              act_ref, wgu_ref, wd_ref, stage_gu, stage_d, sem_gu, sem_d):
    @pl.when(pl.program_id(0) == 0)
    def _():
        _stage_all(_stage_weight(layer, wgu_hbm, wgu_ref, stage_gu, sem_gu, 0, FFN_GU_ROWS),
                   _stage_weight(layer, wd_hbm, wd_ref, stage_d, sem_d, 0, FFN_D_ROWS))

    for r0 in range(0, FFN_TM, FFN_SUB):
        rows = pl.ds(r0, FFN_SUB)
        x = x_ref[rows, :]
        xn = (_rms(x) * nw_ref[...]).astype(BF16)
        for j in range(D_FF // FFN_FC):
            c0 = j * FFN_FC
            g = _dot(xn, wgu_ref[:, c0:c0 + FFN_FC])
            u = _dot(xn, wgu_ref[:, D_FF + c0:D_FF + c0 + FFN_FC])
            act_ref[rows, c0:c0 + FFN_FC] = (_silu(g) * u).astype(BF16)
        h = x + 0.5 * _dot(act_ref[rows, :], wd_ref[...])
        if final:
            h = _rms(h) * fw_ref[...]
        o_ref[rows, :] = h


def _ffn(h, norm_w, wgu_all, wd_all, layer, final_w, final):
    t = h.shape[0]
    const = lambda shape: pl.BlockSpec(shape, lambda i: (0, 0), pipeline_mode=pl.Buffered(1))
    return pl.pallas_call(
        functools.partial(_ffn_body, final, layer),
        grid=(t // FFN_TM,),
        in_specs=[
            pl.BlockSpec((FFN_TM, D_MODEL), lambda i: (i, 0)),
            const((1, D_MODEL)),
            pl.BlockSpec(memory_space=pl.ANY),
            pl.BlockSpec(memory_space=pl.ANY),
            const((1, D_MODEL)),
        ],
        out_specs=pl.BlockSpec((FFN_TM, D_MODEL), lambda i: (i, 0)),
        out_shape=jax.ShapeDtypeStruct((t, D_MODEL), F32),
        scratch_shapes=[
            pltpu.VMEM((FFN_TM, D_FF), BF16),
            pltpu.VMEM((D_MODEL, 2 * D_FF), BF16),
            pltpu.VMEM((D_FF, D_MODEL), BF16),
            pltpu.VMEM((STAGE_DEPTH, FFN_GU_ROWS, 2 * D_FF), F32),
            pltpu.VMEM((STAGE_DEPTH, FFN_D_ROWS, D_MODEL), F32),
            pltpu.SemaphoreType.DMA((STAGE_DEPTH,)),
            pltpu.SemaphoreType.DMA((STAGE_DEPTH,)),
        ],
        compiler_params=pltpu.CompilerParams(
            dimension_semantics=("arbitrary",), vmem_limit_bytes=VMEM_LIMIT),
        name="ffn_final" if final else "ffn",
    )(h, norm_w.reshape(1, -1), wgu_all, wd_all, final_w.reshape(1, -1))


def _even_body(x_ref, nw_ref, win_ref, wtail_ref, wlr_ref, blr_ref, cos_ref, sin_ref,
               dmat_ref, winb_ref, wendb_ref, dec_ref, tri_ref, ctile_ref,
               hnw_ref, wout_ref, o_ref,
               proj, labuf, obuf, ret_s, gla_st):
    groups = x_ref.shape[0]
    ts = x_ref.shape[1]

    @pl.when(pl.program_id(1) == 0)
    def _():
        ret_s[...] = jnp.zeros_like(ret_s)
        gla_st[...] = jnp.zeros_like(gla_st)

    def sequence(g):
        x = x_ref[g]
        pg = proj.at[g]
        hn = (_rms(x) * nw_ref[...]).astype(BF16)
        for c0 in range(0, E_LR, MIX_SLAB):
            pg[:, c0:c0 + MIX_SLAB] = _dot(hn, win_ref[:, c0:c0 + MIX_SLAB])
            yield
        pg[:, E_LR:E_NP] = _dot(hn, wtail_ref[...])
        yield
        glr = pg[:, E_LR:E_LR + LANE].astype(BF16)
        labuf[g] = _log_sigmoid(_dot(glr, wlr_ref[...]) + blr_ref[...]) * (1.0 / GLA_TAU)
        cos = cos_ref[...]
        sin = sin_ref[...]
        for c0 in range(0, 2 * RET_HEADS * RET_DK, RET_DK):
            xx = pg[:, c0:c0 + RET_DK]
            r = xx * cos + pltpu.roll(xx, RET_DK // 2, 1) * sin
            if c0 >= E_RK:
                r = r * (RET_DK ** -0.5)
            pg[:, c0:c0 + RET_DK] = r
        yield
        yield from _round_robin([
            _ret_chain(pg, obuf.at[g], ret_s.at[g], dmat_ref, winb_ref, wendb_ref, dec_ref, ts),
            _gla_chain(pg, labuf.at[g], obuf.at[g], gla_st.at[g], tri_ref, ctile_ref, ts)])
        o_ref[g] = x + _dot((obuf[g] * hnw_ref[...]).astype(BF16), wout_ref[...])
        yield

    _run(_round_robin([sequence(g) for g in range(groups)], lag=EVEN_LAG))


def _ret_chain(proj, obuf, ret_s, dmat_ref, winb_ref, wendb_ref, dec_ref, ts):
    for c in range(ts // RET_CHUNK):
        rows = pl.ds(c * RET_CHUNK, RET_CHUNK)
        stage = []
        for pr in range(RET_HEADS // 2):
            lo = pr * PAIR
            q2 = proj[rows, E_RQ + lo:E_RQ + lo + PAIR].astype(BF16)
            k2 = proj[rows, E_RK + lo:E_RK + lo + PAIR].astype(BF16)
            v2 = proj[rows, E_RV + lo:E_RV + lo + PAIR]
            s0 = ret_s[2 * pr]
            s1 = ret_s[2 * pr + 1]
            s_cat = _dot_nt(q2, _bdiag2(k2[:, :LANE], k2[:, LANE:]))
            inter = _dot(q2, _bdiag2(s0.astype(BF16), s1.astype(BF16)))
            full = _dot_tn(k2, (wendb_ref[pr] * v2).astype(BF16))
            ret_s[2 * pr] = dec_ref[2 * pr] * s0 + full[:LANE, :LANE]
            ret_s[2 * pr + 1] = dec_ref[2 * pr + 1] * s1 + full[LANE:, LANE:]
            stage.append((s_cat, inter, v2.astype(BF16)))
        yield
        outs = []
        for pr in range(RET_HEADS // 2):
            s_cat, inter, vb = stage[pr]
            p = (s_cat * dmat_ref[pr]).astype(BF16)
            outs.append(_dot(p, _bdiag2(vb[:, :LANE], vb[:, LANE:])) + inter * winb_ref[pr])
        yield
        for h in range(RET_HEADS):
            o = outs[h // 2][:, (h % 2) * LANE:(h % 2 + 1) * LANE]
            g = proj[rows, E_RG + h * RET_DK:E_RG + (h + 1) * RET_DK]
            obuf[rows, h * RET_DK:(h + 1) * RET_DK] = _rms(o) * _silu(g)
        yield


def _gla_chain(proj, labuf, obuf, gla_st, tri_ref, ctile_ref, ts):
    lane_head = lax.broadcasted_iota(jnp.int32, (1, GLA_QK), 1) // GLA_DK

    def stack_heads(z):
        return jnp.concatenate(
            [jnp.where(lane_head == h, z, 0.0) for h in range(GLA_HEADS)], axis=0).astype(BF16)

    zb = jnp.zeros((GLA_CHUNK, GLA_DV), BF16)
    for c in range(ts // GLA_CHUNK):
        rows = pl.ds(c * GLA_CHUNK, GLA_CHUNK)
        la_hi, la_lo = _split_bf16(labuf[rows, :])
        tri = tri_ref[...]
        b = _dot(tri, jnp.concatenate([la_hi, la_lo], axis=0))
        yield
        b_mid = b[GLA_CHUNK // 2 - 1:GLA_CHUNK // 2, :]
        b_end = b[GLA_CHUNK - 1:GLA_CHUNK, :]
        q = proj[rows, E_GQ:E_GQ + GLA_QK]
        k = proj[rows, E_GK:E_GK + GLA_QK] * (GLA_DK ** -0.5)
        vb = proj[rows, E_GV:E_GV + GLA_HEADS * GLA_DV].astype(BF16)
        vh = [vb[:, h * GLA_DV:(h + 1) * GLA_DV] for h in range(GLA_HEADS)]
        s_cat = _dot_nt((q * jnp.exp(b - b_mid)).astype(BF16), stack_heads(k * jnp.exp(b_mid - b)))
        st = gla_st[...]
        inter = _dot_nt(stack_heads(q * jnp.exp(b)), st.astype(BF16))
        upd = _dot_tn(jnp.concatenate(vh, axis=0), stack_heads(k * jnp.exp(b_end - b)))
        gla_st[...] = jnp.exp(b_end) * st + upd
        yield
        p = (s_cat * ctile_ref[...]).astype(BF16)
        vbd = jnp.concatenate(
            [jnp.concatenate([vh[h] if j == h else zb for j in range(GLA_HEADS)], axis=1)
             for h in range(GLA_HEADS)], axis=0)
        out_cat = _dot(p, vbd)
        yield
        for h in range(GLA_HEADS):
            g = proj[rows, E_GG + h * GLA_DV:E_GG + (h + 1) * GLA_DV]
            o = out_cat[:, h * GLA_DV:(h + 1) * GLA_DV] + inter[h * GLA_CHUNK:(h + 1) * GLA_CHUNK, :]
            obuf[rows, GROUP_W + h * GLA_DV:GROUP_W + (h + 1) * GLA_DV] = _rms(o) * _silu(g)
        yield


def _pair_lanes(t):
    return np.concatenate([t[0::2], t[1::2]], axis=-1)


def _even_mixer(h, norm_w, w_in, w_lr, b_lr, head_norm, w_out, ts, grp):
    bsz, seq, _ = h.shape
    f32 = np.float32
    hh = np.arange(RET_HEADS, dtype=f32)
    log_g = np.log1p(-np.exp2(f32(-5.0) - hh)).astype(f32)
    idx = np.arange(RET_CHUNK, dtype=f32)
    rel = idx[:, None] - idx[None, :]
    causal = rel >= 0
    dmat = np.where(causal, np.exp(log_g[:, None, None] * np.where(causal, rel, f32(0.0))), f32(0.0)).astype(f32)
    full = (RET_HEADS, RET_CHUNK, RET_DK)
    pair = (RET_HEADS // 2, RET_CHUNK, PAIR)
    w_end = np.broadcast_to(np.exp(log_g[:, None] * (f32(RET_CHUNK - 1.0) - idx)[None, :])[:, :, None], full)
    w_inb = np.broadcast_to(np.exp(log_g[:, None] * (idx + f32(1.0))[None, :])[:, :, None], full)
    dec = np.ascontiguousarray(np.broadcast_to(np.exp(log_g * f32(RET_CHUNK))[:, None, None], full), dtype=f32)
    w_end = np.ascontiguousarray(w_end, dtype=f32)
    w_inb = np.ascontiguousarray(w_inb, dtype=f32)

    half = RET_DK // 2
    inv = (f32(ROPE_BASE) ** (-np.arange(half, dtype=f32) / f32(half))).astype(f32)
    ang = np.arange(seq).astype(f32)[:, None] * inv[None, :]
    cos2 = np.concatenate([np.cos(ang), np.cos(ang)], axis=-1).astype(f32)
    sin2 = np.concatenate([-np.sin(ang), np.sin(ang)], axis=-1).astype(f32)

    tril = np.tril(np.ones((GLA_CHUNK, GLA_CHUNK), f32))
    ctile = np.tile(tril, (1, GLA_HEADS))

    w_main = w_in[:, :E_LR].astype(BF16)
    w_tail = w_in[:, E_LR:]
    w_tail = jnp.pad(w_tail, ((0, 0), (0, E_NP - E_LR - w_tail.shape[1]))).astype(BF16)
    w_lr_p = jnp.pad(w_lr, ((0, LANE - GLA_RANK), (0, 0))).astype(BF16)

    const2 = lambda shape: pl.BlockSpec(shape, lambda b, s: (0, 0), pipeline_mode=pl.Buffered(1))
    const3 = lambda shape: pl.BlockSpec(shape, lambda b, s: (0, 0, 0), pipeline_mode=pl.Buffered(1))
    return pl.pallas_call(
        _even_body,
        grid=(bsz // grp, seq // ts),
        in_specs=[
            pl.BlockSpec((grp, ts, D_MODEL), lambda b, s: (b, s, 0)),
            const2((1, D_MODEL)),
            const2((D_MODEL, E_LR)),
            const2((D_MODEL, LANE)),
            const2((LANE, GLA_QK)),
            const2((1, GLA_QK)),
            pl.BlockSpec((ts, RET_DK), lambda b, s: (s, 0)),
            pl.BlockSpec((ts, RET_DK), lambda b, s: (s, 0)),
            const3(pair), const3(pair), const3(pair), const3(full),
            const2((GLA_CHUNK, 2 * GLA_CHUNK)),
            const2((GLA_CHUNK, GLA_QK)),
            const2((1, D_MODEL)),
            const2((D_MODEL, D_MODEL)),
        ],
        out_specs=pl.BlockSpec((grp, ts, D_MODEL), lambda b, s: (b, s, 0)),
        out_shape=jax.ShapeDtypeStruct(h.shape, F32),
        scratch_shapes=[
            pltpu.VMEM((grp, ts, E_NP), F32),
            pltpu.VMEM((grp, ts, GLA_QK), F32),
            pltpu.VMEM((grp, ts, D_MODEL), F32),
            pltpu.VMEM((grp,) + full, F32),
            pltpu.VMEM((grp, GLA_DV, GLA_QK), F32),
        ],
        compiler_params=pltpu.CompilerParams(
            dimension_semantics=("parallel", "arbitrary"), vmem_limit_bytes=VMEM_LIMIT),
        name="even_mixer",
    )(h, norm_w.reshape(1, -1), w_main, w_tail, w_lr_p, b_lr.reshape(1, -1), cos2, sin2,
      _pair_lanes(dmat), _pair_lanes(w_inb), _pair_lanes(w_end), dec, np.tile(tril, (1, 2)).astype(BF16), ctile,
      head_norm.reshape(1, -1), w_out.astype(BF16))


def _tiles(ref, t0, n, rows=slice(None)):
    return jnp.concatenate([ref[t0 + t, rows, :] for t in range(n)], axis=1)


def _set_tiles(ref, t0, val):
    for t in range(val.shape[1] // LANE):
        ref[t0 + t] = val[:, t * LANE:(t + 1) * LANE]


def _permute_rows(perm_ref, x_bf16):
    return _dot(perm_ref[...], x_bf16)


def _causal_conv(tail, xb, w_ref, b_ref):
    ts = xb.shape[0]
    keep = (CONV_W - 1) * SUBLANE
    last = xb[ts - keep:, :]
    prev = tail[...]
    first = lax.broadcasted_iota(jnp.int32, (SUBLANE, 1), 0) == 0
    fix = [jnp.where(first, pltpu.roll(prev[m * SUBLANE:(m + 1) * SUBLANE, :], 1, 0),
                     pltpu.roll(last[m * SUBLANE:(m + 1) * SUBLANE, :], 1, 0)) for m in range(CONV_W - 1)]
    tail[...] = last
    xpad = jnp.concatenate(fix + [xb], axis=0)
    acc = b_ref[...] + w_ref[CONV_W - 1:CONV_W, :] * xb
    for j in range(CONV_W - 1):
        acc = acc + w_ref[j:j + 1, :] * xpad[j * SUBLANE:j * SUBLANE + ts, :]
    return acc


def _odd_body(x_ref, nw_ref, win_ref, wtail_ref, lcw_ref, lcb_ref, mcw_ref, mcb_ref,
              wa_ref, ba_ref, wx_ref, bx_ref, lam_ref, wqk_ref, bif_ref, tri_ref,
              mnw_ref, wout_ref, perm_ref, unperm_ref, o_ref,
              proj, ltail, mtail, qk, ibuf, fbuf, obuf, lru_h, ml_c, ml_m):
    groups = x_ref.shape[0]
    ts = x_ref.shape[1]

    @pl.when(pl.program_id(1) == 0)
    def _():
        ltail[...] = jnp.zeros_like(ltail)
        mtail[...] = jnp.zeros_like(mtail)
        lru_h[...] = jnp.zeros_like(lru_h)
        ml_c[...] = jnp.zeros_like(ml_c)
        ml_m[...] = jnp.zeros_like(ml_m)

    def sequence(g):
        x = x_ref[g]
        hn = (_rms(x) * nw_ref[...]).astype(BF16)
        hn_blocked = _permute_rows(perm_ref, hn).astype(BF16)
        for c0 in range(0, O_IF, MIX_SLAB):
            lhs = hn_blocked if c0 + MIX_SLAB <= O_MV else hn
            _set_tiles(proj.at[g], c0 // LANE, _dot(lhs, win_ref[:, c0:c0 + MIX_SLAB]))
            yield
        proj[g, O_IF // LANE] = _dot(hn, wtail_ref[...])
        yield
        yield from _round_robin([
            _lru_chain(proj.at[g], ltail.at[g], obuf.at[g], lru_h.at[g],
                       lcw_ref, lcb_ref, wa_ref, ba_ref, wx_ref, bx_ref, lam_ref, unperm_ref, ts),
            _mlstm_chain(proj.at[g], mtail.at[g], qk.at[g], ibuf.at[g], fbuf.at[g], obuf.at[g],
                         ml_c.at[g], ml_m.at[g], mcw_ref, mcb_ref, wqk_ref, bif_ref, tri_ref, mnw_ref,
                         unperm_ref, ts)])
        ob = _tiles(obuf.at[g], 0, D_MODEL // LANE).astype(BF16)
        for c0 in range(0, D_MODEL, OUT_SLAB):
            o_ref[g, :, c0:c0 + OUT_SLAB] = x[:, c0:c0 + OUT_SLAB] + _dot(ob, wout_ref[:, c0:c0 + OUT_SLAB])
            yield

    _run(_round_robin([sequence(g) for g in range(groups)], lag=ODD_LAG))


def _lru_chain(proj, ltail, obuf, lru_h, lcw_ref, lcb_ref, wa_ref, ba_ref, wx_ref, bx_ref, lam_ref, unperm_ref, ts):
    lxc = _causal_conv(ltail, _tiles(proj, O_LX // LANE, LRU_W // LANE), lcw_ref, lcb_ref)
    xb = lxc.astype(BF16)
    def gate(w_ref, b_ref):
        pre = jnp.concatenate([_dot(xb[:, c0:c0 + PAIR], w_ref[c0:c0 + PAIR, c0:c0 + PAIR])
                               for c0 in range(0, LRU_W, PAIR)], axis=1)
        return _sigmoid(pre + b_ref[...])

    r = gate(wa_ref, ba_ref)
    i = gate(wx_ref, bx_ref)
    yield
    lam = lam_ref[...]
    softplus_neg_lam = jnp.maximum(-lam, 0.0) + jnp.log1p(jnp.exp(-jnp.abs(lam)))
    log_a = (-LRU_C) * r * softplus_neg_lam
    a = jnp.exp(log_a)
    u = jnp.sqrt(-jnp.tanh(log_a) * (a * a + 1.0)) * (i * lxc)
    n = ts // SUBLANE
    hs, ps = [u[0:SUBLANE, :]], [a[0:SUBLANE, :]]
    for j in range(1, n):
        aj = a[j * SUBLANE:(j + 1) * SUBLANE, :]
        hs.append(aj * hs[-1] + u[j * SUBLANE:(j + 1) * SUBLANE, :])
        ps.append(aj * ps[-1])
    sub = lax.broadcasted_iota(jnp.int32, (SUBLANE, 1), 0)
    pt, ht = ps[-1], hs[-1]
    d = 1
    while d < SUBLANE:
        keep = sub >= d
        p_s = jnp.where(keep, pltpu.roll(pt, d, 0), 1.0)
        h_s = jnp.where(keep, pltpu.roll(ht, d, 0), 0.0)
        ht = pt * h_s + ht
        pt = pt * p_s
        d *= 2
    hprev = lru_h[0:1, :]
    block_end = ht + pt * hprev
    carry = jnp.where(sub == 0, hprev, pltpu.roll(block_end, 1, 0))
    lru_h[0:1, :] = block_end[SUBLANE - 1:SUBLANE, :]
    hfull = jnp.concatenate([hs[j] + ps[j] * carry for j in range(n)], axis=0)
    ly = _tiles(proj, O_LY // LANE, LRU_W // LANE)
    gelu = ly * (0.5 * (1.0 + jnp.tanh(np.float32(np.sqrt(2.0 / np.pi)) * (ly + 0.044715 * (ly * ly * ly)))))
    _set_tiles(obuf, 0, _permute_rows(unperm_ref, (hfull * gelu).astype(BF16)))
    yield


def _mlstm_chain(proj, mtail, qk, ibuf, fbuf, obuf, ml_c, ml_m, mcw_ref, mcb_ref, wqk_ref, bif_ref, tri_ref,
                 mnw_ref, unperm_ref, ts):
    mc_blocked = _silu(_causal_conv(mtail, _tiles(proj, O_MU // LANE, GROUP_W // LANE), mcw_ref, mcb_ref))
    mc = _permute_rows(unperm_ref, mc_blocked.astype(BF16)).astype(BF16)
    for h in range(ML_HEADS):
        qkh = _dot(mc[:, h * ML_DH:(h + 1) * ML_DH], wqk_ref[h])
        qk[h] = qkh[:, :ML_DH]
        qk[ML_HEADS + h] = qkh[:, ML_DH:] * (ML_DH ** -0.5)
    gates = proj[O_IF // LANE] + bif_ref[...]
    ibuf[...] = gates
    fbuf[...] = pltpu.roll(_log_sigmoid(gates), LANE - ML_HEADS, 1)
    yield

    row_i = lax.broadcasted_iota(jnp.int32, (ML_CHUNK, ML_CHUNK), 0)
    col_i = lax.broadcasted_iota(jnp.int32, (ML_CHUNK, ML_CHUNK), 1)
    causal = row_i >= col_i
    ones_col = jnp.where(col_i == 0, 1.0, 0.0).astype(BF16)

    for c in range(ts // ML_CHUNK):
        rows = pl.ds(c * ML_CHUNK, ML_CHUNK)
        f_hi, f_lo = _split_bf16(fbuf[rows, :])
        tri = tri_ref[...]
        fc = _dot(tri, f_hi) + _dot(tri, f_lo)
        yield
        dm = ibuf[rows, :] - fc
        dm_t = dm.T
        b_end = fc[ML_CHUNK - 1:ML_CHUNK, :]
        a_col = b_end + dm
        a_max = jnp.max(a_col, axis=0, keepdims=True)
        w_end = jnp.exp(a_col - a_max)
        m_old = ml_m[0:1, :]
        m_new = jnp.maximum(b_end + m_old, a_max)
        g_old = jnp.exp(b_end + m_old - m_new)
        g_new = jnp.exp(a_max - m_new)
        ml_m[0:1, :] = m_new
        s_cat, qc, vh = [], [], []
        for pr in range(ML_HEADS // 2):
            q2 = _tiles(qk, 2 * pr, 2, rows).astype(BF16)
            k2f = _tiles(qk, ML_HEADS + 2 * pr, 2, rows)
            k2 = k2f.astype(BF16)
            s_cat.append(_dot_nt(q2, _bdiag2(k2[:, :LANE], k2[:, LANE:])))
            for j in range(2):
                h = 2 * pr + j
                v = proj[O_MV // LANE + h, rows, :].astype(BF16)
                cx = ml_c[h]
                qc.append(_dot(q2[:, j * LANE:(j + 1) * LANE], cx.astype(BF16)))
                wk = (w_end[:, h:h + 1] * k2f[:, j * LANE:(j + 1) * LANE]).astype(BF16)
                updx = _dot_tn(wk, jnp.concatenate([v, ones_col], axis=1))
                ml_c[h] = g_old[:, h:h + 1] * cx + g_new[:, h:h + 1] * updx
                vh.append(v)
        yield
        mxs, sms, dens, nums = [], [], [], []
        for h in range(ML_HEADS):
            d_row = dm_t[h:h + 1, :]
            m_h = m_old[:, h:h + 1]
            cm = jnp.max(jnp.where(causal, d_row, -jnp.inf), axis=1, keepdims=True)
            mx = jnp.maximum(cm, m_h)
            p = jnp.where(causal, jnp.exp(d_row - mx), 0.0)
            s = s_cat[h // 2][:, (h % 2) * LANE:(h % 2 + 1) * LANE] * p
            mxs.append(mx)
            sms.append(s.astype(BF16))
            dens.append(jnp.sum(s, axis=1, keepdims=True))
        for pr in range(ML_HEADS // 2):
            nums.append(_dot(jnp.concatenate([sms[2 * pr], sms[2 * pr + 1]], axis=1),
                             _bdiag2(vh[2 * pr], vh[2 * pr + 1])))
        yield
        for h in range(ML_HEADS):
            m_h = m_old[:, h:h + 1]
            w_inter = jnp.exp(m_h - mxs[h])
            num = nums[h // 2][:, (h % 2) * LANE:(h % 2 + 1) * LANE] + w_inter * qc[h][:, :ML_DH]
            den = dens[h] + w_inter * qc[h][:, ML_DH:ML_DH + 1]
            m_t = fc[:, h:h + 1] + mxs[h]
            hh = num / jnp.maximum(jnp.abs(den), jnp.exp(-m_t))
            og = _sigmoid(proj[O_MO // LANE + h, rows, :])
            obuf[GROUP_W // LANE + h, rows, :] = _rms(og * hh) * mnw_ref[:, h * ML_DH:(h + 1) * ML_DH]
        yield


def _block_diag(w):
    g, n, _ = w.shape
    eye = jnp.eye(g, dtype=w.dtype)
    return (eye[:, None, :, None] * w[:, :, None, :]).reshape(g * n, g * n)


def _odd_mixer(h, norm_w, w_in, lcw, lcb, wa, ba, wx, bx, lam, mcw, mcb, wq, wk, bi, bf, ml_norm, w_out, ts, grp):
    bsz, seq, _ = h.shape
    w_main = w_in[:, :O_IF].astype(BF16)
    w_tail = w_in[:, O_IF:]
    w_tail = jnp.pad(w_tail, ((0, 0), (0, O_NP - O_IF - w_tail.shape[1]))).astype(BF16)
    wqk = jnp.concatenate([wq, wk], axis=-1).astype(BF16)
    bif = jnp.pad(jnp.concatenate([bi, bf]), (0, LANE - 2 * ML_HEADS)).reshape(1, LANE)
    tri = np.tril(np.ones((ML_CHUNK, ML_CHUNK), np.float32)).astype(BF16)
    r = np.arange(ts)
    time_of_row = (r % SUBLANE) * (ts // SUBLANE) + r // SUBLANE
    perm = (time_of_row[:, None] == r[None, :]).astype(BF16)
    row = lambda t: t.reshape(1, -1)

    const2 = lambda shape: pl.BlockSpec(shape, lambda b, s: (0, 0), pipeline_mode=pl.Buffered(1))
    const3 = lambda shape: pl.BlockSpec(shape, lambda b, s: (0, 0, 0), pipeline_mode=pl.Buffered(1))
    return pl.pallas_call(
        _odd_body,
        grid=(bsz // grp, seq // ts),
        in_specs=[
            pl.BlockSpec((grp, ts, D_MODEL), lambda b, s: (b, s, 0)),
            const2((1, D_MODEL)),
            const2((D_MODEL, O_IF)),
            const2((D_MODEL, LANE)),
            const2((CONV_W, LRU_W)), const2((1, LRU_W)),
            const2((CONV_W, GROUP_W)), const2((1, GROUP_W)),
            const2((LRU_W, LRU_W)), const2((1, LRU_W)),
            const2((LRU_W, LRU_W)), const2((1, LRU_W)),
            const2((1, LRU_W)),
            const3((ML_HEADS, ML_DH, 2 * ML_DH)),
            const2((1, LANE)),
            const2((ML_CHUNK, ML_CHUNK)),
            const2((1, GROUP_W)),
            const2((D_MODEL, D_MODEL)),
            const2((ts, ts)), const2((ts, ts)),
        ],
        out_specs=pl.BlockSpec((grp, ts, D_MODEL), lambda b, s: (b, s, 0)),
        out_shape=jax.ShapeDtypeStruct(h.shape, F32),
        scratch_shapes=[
            pltpu.VMEM((grp, O_NP // LANE, ts, LANE), F32),
            pltpu.VMEM((grp, (CONV_W - 1) * SUBLANE, LRU_W), F32),
            pltpu.VMEM((grp, (CONV_W - 1) * SUBLANE, GROUP_W), F32),
            pltpu.VMEM((grp, 2 * GROUP_W // LANE, ts, LANE), F32),
            pltpu.VMEM((grp, ts, LANE), F32),
            pltpu.VMEM((grp, ts, LANE), F32),
            pltpu.VMEM((grp, D_MODEL // LANE, ts, LANE), F32),
            pltpu.VMEM((grp, SUBLANE, LRU_W), F32),
            pltpu.VMEM((grp, ML_HEADS, ML_DH, 2 * ML_DH), F32),
            pltpu.VMEM((grp, SUBLANE, LANE), F32),
        ],
        compiler_params=pltpu.CompilerParams(
            dimension_semantics=("parallel", "arbitrary"), vmem_limit_bytes=VMEM_LIMIT),
        name="odd_mixer",
    )(h, row(norm_w), w_main, w_tail, lcw, row(lcb), mcw, row(mcb),
      _block_diag(wa).astype(BF16), row(ba), _block_diag(wx).astype(BF16), row(bx), row(lam),
      wqk, bif, tri, row(ml_norm), w_out.astype(BF16), perm, perm.T)


def kernel(x, ffn1_norm, ffn1_wgu, ffn1_wd, mix_norm, ffn2_norm, ffn2_wgu, ffn2_wd, e_w_in, e_w_lr_up, e_b_lr, e_head_norm, e_w_out, o_w_in, o_lru_conv_w, o_lru_conv_b, o_lru_wa, o_lru_ba, o_lru_wx, o_lru_bx, o_lru_lambda, o_ml_conv_w, o_ml_conv_b, o_ml_wq, o_ml_wk, o_ml_bi, o_ml_bf, o_ml_norm, o_w_out, final_norm):
    bsz, seq, d = x.shape
    depth = ffn1_norm.shape[0]
    h = x
    for layer in range(depth):
        j = layer // 2
        h = _ffn(h.reshape(bsz * seq, d), ffn1_norm[layer], ffn1_wgu, ffn1_wd, layer,
                 final_norm, False).reshape(bsz, seq, d)
        if layer % 2 == 0:
            h = _even_mixer(h, mix_norm[layer], e_w_in[j], e_w_lr_up[j], e_b_lr[j], e_head_norm[j],
                            e_w_out[j], MIX_TS, EVEN_G)
        else:
            h = _odd_mixer(h, mix_norm[layer], o_w_in[j], o_lru_conv_w[j], o_lru_conv_b[j],
                           o_lru_wa[j], o_lru_ba[j], o_lru_wx[j], o_lru_bx[j], o_lru_lambda[j],
                           o_ml_conv_w[j], o_ml_conv_b[j], o_ml_wq[j], o_ml_wk[j], o_ml_bi[j], o_ml_bf[j],
                           o_ml_norm[j], o_w_out[j], MIX_TS, ODD_G)
        h = _ffn(h.reshape(bsz * seq, d), ffn2_norm[layer], ffn2_wgu, ffn2_wd, layer,
                 final_norm, layer == depth - 1).reshape(bsz, seq, d)
    return h
```

```python
import functools

import numpy as np
import jax
import jax.numpy as jnp
from jax import lax
from jax.experimental import pallas as pl
from jax.experimental.pallas import tpu as pltpu

F32 = jnp.float32
BF16 = jnp.bfloat16

D_MODEL = 1024
D_FF = 2816
GROUP_W = D_MODEL // 2
EPS = 1e-6

RET_HEADS = 4
RET_DK = 128
RET_CHUNK = 128
ROPE_BASE = 10000.0

GLA_HEADS = 4
GLA_DK = 64
GLA_DV = 128
GLA_RANK = 16
GLA_TAU = 16.0
GLA_CHUNK = 64
GLA_QK = GLA_HEADS * GLA_DK

LRU_W = GROUP_W
LRU_BLOCKS = 8
LRU_BS = LRU_W // LRU_BLOCKS
LRU_C = 8.0
CONV_W = 4

ML_HEADS = 4
ML_DH = 128
ML_CHUNK = 128

LANE = 128
SUBLANE = 8
VMEM_LIMIT = 56 * 1024 * 1024

E_RQ, E_RK, E_RV, E_RG = 0, 512, 1024, 1536
E_GQ, E_GK, E_GV, E_GG, E_LR = 2048, 2304, 2560, 3072, 3584
E_NP = 3712
O_LY, O_LX, O_MU, O_MV, O_MO, O_IF = 0, 512, 1024, 1536, 2048, 2560
O_NP = 2688

FFN_TM = 1024
FFN_SUB = 256
FFN_FC = 256
FFN_GU_ROWS = 32
FFN_D_ROWS = 128
STAGE_DEPTH = 8
MIX_TS = 256
EVEN_G = 4
ODD_G = 4
MIX_SLAB = 512
OUT_SLAB = 256
EVEN_LAG = 4
ODD_LAG = 5
PAIR = 2 * LANE


def _dot(a, b):
    return jnp.dot(a, b, preferred_element_type=F32)


def _dot_nt(a, b):
    return lax.dot_general(a, b, (((1,), (1,)), ((), ())), preferred_element_type=F32)


def _dot_tn(a, b):
    return lax.dot_general(a, b, (((0,), (0,)), ((), ())), preferred_element_type=F32)


def _rms(x):
    return x * lax.rsqrt(jnp.mean(x * x, axis=-1, keepdims=True) + EPS)


def _sigmoid(x):
    return 1.0 / (1.0 + jnp.exp(-x))


def _silu(x):
    return x * _sigmoid(x)


def _log_sigmoid(x):
    return jnp.minimum(x, 0.0) - jnp.log1p(jnp.exp(-jnp.abs(x)))


def _split_bf16(x):
    hi = x.astype(BF16)
    lo = (x - hi.astype(F32)).astype(BF16)
    return hi, lo


def _bdiag2(a, b):
    z = jnp.zeros_like(a)
    return jnp.concatenate([jnp.concatenate([a, z], axis=1), jnp.concatenate([z, b], axis=1)], axis=0)


def _round_robin(gens, lag=0):
    gens = list(gens)
    done = [False] * len(gens)
    rnd = 0
    while not all(done):
        for i, gen in enumerate(gens):
            if done[i] or rnd < lag * i:
                continue
            try:
                next(gen)
            except StopIteration:
                done[i] = True
        rnd += 1
        yield


def _run(gen):
    for _ in gen:
        pass


def _stage_weight(layer, w_hbm, w_bf, stage, sem, axis, width):
    n = w_bf.shape[axis] // width
    depth = stage.shape[0]

    def window(j):
        sl = pl.ds(j * width, width)
        return (sl, slice(None)) if axis == 0 else (slice(None), sl)

    def copy(j):
        return pltpu.make_async_copy(w_hbm.at[(layer,) + window(j)], stage.at[j % depth], sem.at[j % depth])

    for j in range(min(depth - 1, n)):
        copy(j).start()
    yield
    for j in range(n):
        if j + depth - 1 < n:
            copy(j + depth - 1).start()
        copy(j).wait()
        w_bf[window(j)] = stage[j % depth].astype(BF16)


def _stage_all(*streams):
    for s in streams:
        next(s)
    for s in streams:
        _run(s)


def _ffn_body(final, layer, x_ref, nw_ref, wgu_hbm, wd_hbm, fw_ref, o_ref,
              act_ref, wgu_ref, wd_ref, stage_gu, stage_d, sem_gu, sem_d):
    @pl.when(pl.program_id(0) == 0)
    def _():
        _stage_all(_stage_weight(layer, wgu_hbm, wgu_ref, stage_gu, sem_gu, 0, FFN_GU_ROWS),
                   _stage_weight(layer, wd_hbm, wd_ref, stage_d, sem_d, 0, FFN_D_ROWS))

    for r0 in range(0, FFN_TM, FFN_SUB):
        rows = pl.ds(r0, FFN_SUB)
        x = x_ref[rows, :]
        xn = (_rms(x) * nw_ref[...]).astype(BF16)
        for j in range(D_FF // FFN_FC):
            c0 = j * FFN_FC
            g = _dot(xn, wgu_ref[:, c0:c0 + FFN_FC])
            u = _dot(xn, wgu_ref[:, D_FF + c0:D_FF + c0 + FFN_FC])
            act_ref[rows, c0:c0 + FFN_FC] = (_silu(g) * u).astype(BF16)
        h = x + 0.5 * _dot(act_ref[rows, :], wd_ref[...])
        if final:
            h = _rms(h) * fw_ref[...]
        o_ref[rows, :] = h


def _ffn(h, norm_w, wgu_all, wd_all, layer, final_w, final):
    t = h.shape[0]
    const = lambda shape: pl.BlockSpec(shape, lambda i: (0, 0), pipeline_mode=pl.Buffered(1))
    return pl.pallas_call(
        functools.partial(_ffn_body, final, layer),
        grid=(t // FFN_TM,),
        in_specs=[
            pl.BlockSpec((FFN_TM, D_MODEL), lambda i: (i, 0)),
            const((1, D_MODEL)),
            pl.BlockSpec(memory_space=pl.ANY),
            pl.BlockSpec(memory_space=pl.ANY),
            const((1, D_MODEL)),
        ],
        out_specs=pl.BlockSpec((FFN_TM, D_MODEL), lambda i: (i, 0)),
        out_shape=jax.ShapeDtypeStruct((t, D_MODEL), F32),
        scratch_shapes=[
            pltpu.VMEM((FFN_TM, D_FF), BF16),
            pltpu.VMEM((D_MODEL, 2 * D_FF), BF16),
            pltpu.VMEM((D_FF, D_MODEL), BF16),
            pltpu.VMEM((STAGE_DEPTH, FFN_GU_ROWS, 2 * D_FF), F32),
            pltpu.VMEM((STAGE_DEPTH, FFN_D_ROWS, D_MODEL), F32),
            pltpu.SemaphoreType.DMA((STAGE_DEPTH,)),
            pltpu.SemaphoreType.DMA((STAGE_DEPTH,)),
        ],
        compiler_params=pltpu.CompilerParams(
            dimension_semantics=("arbitrary",), vmem_limit_bytes=VMEM_LIMIT),
        name="ffn_final" if final else "ffn",
    )(h, norm_w.reshape(1, -1), wgu_all, wd_all, final_w.reshape(1, -1))


def _even_body(x_ref, nw_ref, win_ref, wtail_ref, wlr_ref, blr_ref, cos_ref, sin_ref,
               dmat_ref, winb_ref, wendb_ref, dec_ref, tri_ref, ctile_ref,
               hnw_ref, wout_ref, o_ref,
               proj, labuf, obuf, ret_s, gla_st):
    groups = x_ref.shape[0]
    ts = x_ref.shape[1]

    @pl.when(pl.program_id(1) == 0)
    def _():
        ret_s[...] = jnp.zeros_like(ret_s)
        gla_st[...] = jnp.zeros_like(gla_st)

    def sequence(g):
        x = x_ref[g]
        pg = proj.at[g]
        hn = (_rms(x) * nw_ref[...]).astype(BF16)
        for c0 in range(0, E_LR, MIX_SLAB):
            pg[:, c0:c0 + MIX_SLAB] = _dot(hn, win_ref[:, c0:c0 + MIX_SLAB])
            yield
        pg[:, E_LR:E_NP] = _dot(hn, wtail_ref[...])
        yield
        glr = pg[:, E_LR:E_LR + LANE].astype(BF16)
        labuf[g] = _log_sigmoid(_dot(glr, wlr_ref[...]) + blr_ref[...]) * (1.0 / GLA_TAU)
        cos = cos_ref[...]
        sin = sin_ref[...]
        for c0 in range(0, 2 * RET_HEADS * RET_DK, RET_DK):
            xx = pg[:, c0:c0 + RET_DK]
            r = xx * cos + pltpu.roll(xx, RET_DK // 2, 1) * sin
            if c0 >= E_RK:
                r = r * (RET_DK ** -0.5)
            pg[:, c0:c0 + RET_DK] = r
        yield
        yield from _round_robin([
            _ret_chain(pg, obuf.at[g], ret_s.at[g], dmat_ref, winb_ref, wendb_ref, dec_ref, ts),
            _gla_chain(pg, labuf.at[g], obuf.at[g], gla_st.at[g], tri_ref, ctile_ref, ts)])
        o_ref[g] = x + _dot((obuf[g] * hnw_ref[...]).astype(BF16), wout_ref[...])
        yield

    _run(_round_robin([sequence(g) for g in range(groups)], lag=EVEN_LAG))


def _ret_chain(proj, obuf, ret_s, dmat_ref, winb_ref, wendb_ref, dec_ref, ts):
    for c in range(ts // RET_CHUNK):
        rows = pl.ds(c * RET_CHUNK, RET_CHUNK)
        stage = []
        for pr in range(RET_HEADS // 2):
            lo = pr * PAIR
            q2 = proj[rows, E_RQ + lo:E_RQ + lo + PAIR].astype(BF16)
            k2 = proj[rows, E_RK + lo:E_RK + lo + PAIR].astype(BF16)
            v2 = proj[rows, E_RV + lo:E_RV + lo + PAIR]
            s0 = ret_s[2 * pr]
            s1 = ret_s[2 * pr + 1]
            s_cat = _dot_nt(q2, _bdiag2(k2[:, :LANE], k2[:, LANE:]))
            inter = _dot(q2, _bdiag2(s0.astype(BF16), s1.astype(BF16)))
            full = _dot_tn(k2, (wendb_ref[pr] * v2).astype(BF16))
            ret_s[2 * pr] = dec_ref[2 * pr] * s0 + full[:LANE, :LANE]
            ret_s[2 * pr + 1] = dec_ref[2 * pr + 1] * s1 + full[LANE:, LANE:]
            stage.append((s_cat, inter, v2.astype(BF16)))
        yield
        outs = []
        for pr in range(RET_HEADS // 2):
            s_cat, inter, vb = stage[pr]
            p = (s_cat * dmat_ref[pr]).astype(BF16)
            outs.append(_dot(p, _bdiag2(vb[:, :LANE], vb[:, LANE:])) + inter * winb_ref[pr])
        yield
        for h in range(RET_HEADS):
            o = outs[h // 2][:, (h % 2) * LANE:(h % 2 + 1) * LANE]
            g = proj[rows, E_RG + h * RET_DK:E_RG + (h + 1) * RET_DK]
            obuf[rows, h * RET_DK:(h + 1) * RET_DK] = _rms(o) * _silu(g)
        yield


def _gla_chain(proj, labuf, obuf, gla_st, tri_ref, ctile_ref, ts):
    lane_head = lax.broadcasted_iota(jnp.int32, (1, GLA_QK), 1) // GLA_DK

    def stack_heads(z):
        return jnp.concatenate(
            [jnp.where(lane_head == h, z, 0.0) for h in range(GLA_HEADS)], axis=0).astype(BF16)

    zb = jnp.zeros((GLA_CHUNK, GLA_DV), BF16)
    for c in range(ts // GLA_CHUNK):
        rows = pl.ds(c * GLA_CHUNK, GLA_CHUNK)
        la_hi, la_lo = _split_bf16(labuf[rows, :])
        tri = tri_ref[...]
        b = _dot(tri, jnp.concatenate([la_hi, la_lo], axis=0))
        yield
        b_mid = b[GLA_CHUNK // 2 - 1:GLA_CHUNK // 2, :]
        b_end = b[GLA_CHUNK - 1:GLA_CHUNK, :]
        q = proj[rows, E_GQ:E_GQ + GLA_QK]
        k = proj[rows, E_GK:E_GK + GLA_QK] * (GLA_DK ** -0.5)
        vb = proj[rows, E_GV:E_GV + GLA_HEADS * GLA_DV].astype(BF16)
        vh = [vb[:, h * GLA_DV:(h + 1) * GLA_DV] for h in range(GLA_HEADS)]
        s_cat = _dot_nt((q * jnp.exp(b - b_mid)).astype(BF16), stack_heads(k * jnp.exp(b_mid - b)))
        st = gla_st[...]
        inter = _dot_nt(stack_heads(q * jnp.exp(b)), st.astype(BF16))
        upd = _dot_tn(jnp.concatenate(vh, axis=0), stack_heads(k * jnp.exp(b_end - b)))
        gla_st[...] = jnp.exp(b_end) * st + upd
        yield
        p = (s_cat * ctile_ref[...]).astype(BF16)
        vbd = jnp.concatenate(
            [jnp.concatenate([vh[h] if j == h else zb for j in range(GLA_HEADS)], axis=1)
             for h in range(GLA_HEADS)], axis=0)
        out_cat = _dot(p, vbd)
        yield
        for h in range(GLA_HEADS):
            g = proj[rows, E_GG + h * GLA_DV:E_GG + (h + 1) * GLA_DV]
            o = out_cat[:, h * GLA_DV:(h + 1) * GLA_DV] + inter[h * GLA_CHUNK:(h + 1) * GLA_CHUNK, :]
            obuf[rows, GROUP_W + h * GLA_DV:GROUP_W + (h + 1) * GLA_DV] = _rms(o) * _silu(g)
        yield


def _pair_lanes(t):
    return np.concatenate([t[0::2], t[1::2]], axis=-1)


def _even_mixer(h, norm_w, w_in, w_lr, b_lr, head_norm, w_out, ts, grp):
    bsz, seq, _ = h.shape
    f32 = np.float32
    hh = np.arange(RET_HEADS, dtype=f32)
    log_g = np.log1p(-np.exp2(f32(-5.0) - hh)).astype(f32)
    idx = np.arange(RET_CHUNK, dtype=f32)
    rel = idx[:, None] - idx[None, :]
    causal = rel >= 0
    dmat = np.where(causal, np.exp(log_g[:, None, None] * np.where(causal, rel, f32(0.0))), f32(0.0)).astype(f32)
    full = (RET_HEADS, RET_CHUNK, RET_DK)
    pair = (RET_HEADS // 2, RET_CHUNK, PAIR)
    w_end = np.broadcast_to(np.exp(log_g[:, None] * (f32(RET_CHUNK - 1.0) - idx)[None, :])[:, :, None], full)
    w_inb = np.broadcast_to(np.exp(log_g[:, None] * (idx + f32(1.0))[None, :])[:, :, None], full)
    dec = np.ascontiguousarray(np.broadcast_to(np.exp(log_g * f32(RET_CHUNK))[:, None, None], full), dtype=f32)
    w_end = np.ascontiguousarray(w_end, dtype=f32)
    w_inb = np.ascontiguousarray(w_inb, dtype=f32)

    half = RET_DK // 2
    inv = (f32(ROPE_BASE) ** (-np.arange(half, dtype=f32) / f32(half))).astype(f32)
    ang = np.arange(seq).astype(f32)[:, None] * inv[None, :]
    cos2 = np.concatenate([np.cos(ang), np.cos(ang)], axis=-1).astype(f32)
    sin2 = np.concatenate([-np.sin(ang), np.sin(ang)], axis=-1).astype(f32)

    tril = np.tril(np.ones((GLA_CHUNK, GLA_CHUNK), f32))
    ctile = np.tile(tril, (1, GLA_HEADS))

    w_main = w_in.astype(BF16)[:, :E_LR]
    w_tail = w_in[:, E_LR:]
    w_tail = jnp.pad(w_tail, ((0, 0), (0, E_NP - E_LR - w_tail.shape[1]))).astype(BF16)
    w_lr_p = jnp.pad(w_lr, ((0, LANE - GLA_RANK), (0, 0))).astype(BF16)

    const2 = lambda shape: pl.BlockSpec(shape, lambda b, s: (0, 0), pipeline_mode=pl.Buffered(1))
    const3 = lambda shape: pl.BlockSpec(shape, lambda b, s: (0, 0, 0), pipeline_mode=pl.Buffered(1))
    return pl.pallas_call(
        _even_body,
        grid=(bsz // grp, seq // ts),
        in_specs=[
            pl.BlockSpec((grp, ts, D_MODEL), lambda b, s: (b, s, 0)),
            const2((1, D_MODEL)),
            const2((D_MODEL, E_LR)),
            const2((D_MODEL, LANE)),
            const2((LANE, GLA_QK)),
            const2((1, GLA_QK)),
            pl.BlockSpec((ts, RET_DK), lambda b, s: (s, 0)),
            pl.BlockSpec((ts, RET_DK), lambda b, s: (s, 0)),
            const3(pair), const3(pair), const3(pair), const3(full),
            const2((GLA_CHUNK, 2 * GLA_CHUNK)),
            const2((GLA_CHUNK, GLA_QK)),
            const2((1, D_MODEL)),
            const2((D_MODEL, D_MODEL)),
        ],
        out_specs=pl.BlockSpec((grp, ts, D_MODEL), lambda b, s: (b, s, 0)),
        out_shape=jax.ShapeDtypeStruct(h.shape, F32),
        scratch_shapes=[
            pltpu.VMEM((grp, ts, E_NP), F32),
            pltpu.VMEM((grp, ts, GLA_QK), F32),
            pltpu.VMEM((grp, ts, D_MODEL), F32),
            pltpu.VMEM((grp,) + full, F32),
            pltpu.VMEM((grp, GLA_DV, GLA_QK), F32),
        ],
        compiler_params=pltpu.CompilerParams(
            dimension_semantics=("parallel", "arbitrary"), vmem_limit_bytes=VMEM_LIMIT),
        name="even_mixer",
    )(h, norm_w.reshape(1, -1), w_main, w_tail, w_lr_p, b_lr.reshape(1, -1), cos2, sin2,
      _pair_lanes(dmat), _pair_lanes(w_inb), _pair_lanes(w_end), dec, np.tile(tril, (1, 2)).astype(BF16), ctile,
      head_norm.reshape(1, -1), w_out.astype(BF16))


def _tiles(ref, t0, n, rows=slice(None)):
    return jnp.concatenate([ref[t0 + t, rows, :] for t in range(n)], axis=1)


def _set_tiles(ref, t0, val):
    for t in range(val.shape[1] // LANE):
        ref[t0 + t] = val[:, t * LANE:(t + 1) * LANE]


def _permute_rows(perm_ref, x_bf16):
    return _dot(perm_ref[...], x_bf16)


def _causal_conv(tail, xb, w_ref, b_ref):
    ts = xb.shape[0]
    keep = (CONV_W - 1) * SUBLANE
    last = xb[ts - keep:, :]
    prev = tail[...]
    first = lax.broadcasted_iota(jnp.int32, (SUBLANE, 1), 0) == 0
    fix = [jnp.where(first, pltpu.roll(prev[m * SUBLANE:(m + 1) * SUBLANE, :], 1, 0),
                     pltpu.roll(last[m * SUBLANE:(m + 1) * SUBLANE, :], 1, 0)) for m in range(CONV_W - 1)]
    tail[...] = last
    xpad = jnp.concatenate(fix + [xb], axis=0)
    acc = b_ref[...] + w_ref[CONV_W - 1:CONV_W, :] * xb
    for j in range(CONV_W - 1):
        acc = acc + w_ref[j:j + 1, :] * xpad[j * SUBLANE:j * SUBLANE + ts, :]
    return acc


def _odd_body(x_ref, nw_ref, win_ref, wtail_ref, lcw_ref, lcb_ref, mcw_ref, mcb_ref,
              wa_ref, ba_ref, wx_ref, bx_ref, lam_ref, wqk_ref, bif_ref, tri_ref,
              mnw_ref, wout_ref, perm_ref, unperm_ref, o_ref,
              proj, ltail, mtail, qk, ibuf, fbuf, obuf, lru_h, ml_c, ml_m):
    groups = x_ref.shape[0]
    ts = x_ref.shape[1]

    @pl.when(pl.program_id(1) == 0)
    def _():
        ltail[...] = jnp.zeros_like(ltail)
        mtail[...] = jnp.zeros_like(mtail)
        lru_h[...] = jnp.zeros_like(lru_h)
        ml_c[...] = jnp.zeros_like(ml_c)
        ml_m[...] = jnp.zeros_like(ml_m)

    def sequence(g):
        x = x_ref[g]
        hn = (_rms(x) * nw_ref[...]).astype(BF16)
        hn_blocked = _permute_rows(perm_ref, hn).astype(BF16)
        for c0 in range(0, O_IF, MIX_SLAB):
            lhs = hn_blocked if c0 + MIX_SLAB <= O_MV else hn
            _set_tiles(proj.at[g], c0 // LANE, _dot(lhs, win_ref[:, c0:c0 + MIX_SLAB]))
            yield
        proj[g, O_IF // LANE] = _dot(hn, wtail_ref[...])
        yield
        yield from _round_robin([
            _lru_chain(proj.at[g], ltail.at[g], obuf.at[g], lru_h.at[g],
                       lcw_ref, lcb_ref, wa_ref, ba_ref, wx_ref, bx_ref, lam_ref, unperm_ref, ts),
            _mlstm_chain(proj.at[g], mtail.at[g], qk.at[g], ibuf.at[g], fbuf.at[g], obuf.at[g],
                         ml_c.at[g], ml_m.at[g], mcw_ref, mcb_ref, wqk_ref, bif_ref, tri_ref, mnw_ref,
                         unperm_ref, ts)])
        ob = _tiles(obuf.at[g], 0, D_MODEL // LANE).astype(BF16)
        for c0 in range(0, D_MODEL, OUT_SLAB):
            o_ref[g, :, c0:c0 + OUT_SLAB] = x[:, c0:c0 + OUT_SLAB] + _dot(ob, wout_ref[:, c0:c0 + OUT_SLAB])
            yield

    _run(_round_robin([sequence(g) for g in range(groups)], lag=ODD_LAG))


def _lru_chain(proj, ltail, obuf, lru_h, lcw_ref, lcb_ref, wa_ref, ba_ref, wx_ref, bx_ref, lam_ref, unperm_ref, ts):
    lxc = _causal_conv(ltail, _tiles(proj, O_LX // LANE, LRU_W // LANE), lcw_ref, lcb_ref)
    xb = lxc.astype(BF16)
    def gate(w_ref, b_ref):
        pre = jnp.concatenate([_dot(xb[:, c0:c0 + PAIR], w_ref[c0:c0 + PAIR, c0:c0 + PAIR])
                               for c0 in range(0, LRU_W, PAIR)], axis=1)
        return _sigmoid(pre + b_ref[...])

    r = gate(wa_ref, ba_ref)
    i = gate(wx_ref, bx_ref)
    yield
    lam = lam_ref[...]
    softplus_neg_lam = jnp.maximum(-lam, 0.0) + jnp.log1p(jnp.exp(-jnp.abs(lam)))
    log_a = (-LRU_C) * r * softplus_neg_lam
    a = jnp.exp(log_a)
    u = jnp.sqrt(-jnp.tanh(log_a) * (a * a + 1.0)) * (i * lxc)
    n = ts // SUBLANE
    hs, ps = [u[0:SUBLANE, :]], [a[0:SUBLANE, :]]
    for j in range(1, n):
        aj = a[j * SUBLANE:(j + 1) * SUBLANE, :]
        hs.append(aj * hs[-1] + u[j * SUBLANE:(j + 1) * SUBLANE, :])
        ps.append(aj * ps[-1])
    sub = lax.broadcasted_iota(jnp.int32, (SUBLANE, 1), 0)
    pt, ht = ps[-1], hs[-1]
    d = 1
    while d < SUBLANE:
        keep = sub >= d
        p_s = jnp.where(keep, pltpu.roll(pt, d, 0), 1.0)
        h_s = jnp.where(keep, pltpu.roll(ht, d, 0), 0.0)
        ht = pt * h_s + ht
        pt = pt * p_s
        d *= 2
    hprev = lru_h[0:1, :]
    block_end = ht + pt * hprev
    carry = jnp.where(sub == 0, hprev, pltpu.roll(block_end, 1, 0))
    lru_h[0:1, :] = block_end[SUBLANE - 1:SUBLANE, :]
    hfull = jnp.concatenate([hs[j] + ps[j] * carry for j in range(n)], axis=0)
    ly = _tiles(proj, O_LY // LANE, LRU_W // LANE)
    gelu = ly * (0.5 * (1.0 + jnp.tanh(np.float32(np.sqrt(2.0 / np.pi)) * (ly + 0.044715 * (ly * ly * ly)))))
    _set_tiles(obuf, 0, _permute_rows(unperm_ref, (hfull * gelu).astype(BF16)))
    yield


def _mlstm_chain(proj, mtail, qk, ibuf, fbuf, obuf, ml_c, ml_m, mcw_ref, mcb_ref, wqk_ref, bif_ref, tri_ref,
                 mnw_ref, unperm_ref, ts):
    mc_blocked = _silu(_causal_conv(mtail, _tiles(proj, O_MU // LANE, GROUP_W // LANE), mcw_ref, mcb_ref))
    mc = _permute_rows(unperm_ref, mc_blocked.astype(BF16)).astype(BF16)
    for h in range(ML_HEADS):
        qkh = _dot(mc[:, h * ML_DH:(h + 1) * ML_DH], wqk_ref[h])
        qk[h] = qkh[:, :ML_DH]
        qk[ML_HEADS + h] = qkh[:, ML_DH:] * (ML_DH ** -0.5)
    gates = proj[O_IF // LANE] + bif_ref[...]
    ibuf[...] = gates
    fbuf[...] = pltpu.roll(_log_sigmoid(gates), LANE - ML_HEADS, 1)
    yield

    row_i = lax.broadcasted_iota(jnp.int32, (ML_CHUNK, ML_CHUNK), 0)
    col_i = lax.broadcasted_iota(jnp.int32, (ML_CHUNK, ML_CHUNK), 1)
    causal = row_i >= col_i
    ones_col = jnp.where(col_i == 0, 1.0, 0.0).astype(BF16)

    for c in range(ts // ML_CHUNK):
        rows = pl.ds(c * ML_CHUNK, ML_CHUNK)
        f_hi, f_lo = _split_bf16(fbuf[rows, :])
        tri = tri_ref[...]
        fc = _dot(tri, f_hi) + _dot(tri, f_lo)
        yield
        dm = ibuf[rows, :] - fc
        dm_t = dm.T
        b_end = fc[ML_CHUNK - 1:ML_CHUNK, :]
        a_col = b_end + dm
        a_max = jnp.max(a_col, axis=0, keepdims=True)
        w_end = jnp.exp(a_col - a_max)
        m_old = ml_m[0:1, :]
        m_new = jnp.maximum(b_end + m_old, a_max)
        g_old = jnp.exp(b_end + m_old - m_new)
        g_new = jnp.exp(a_max - m_new)
        ml_m[0:1, :] = m_new
        s_cat, qc, vh = [], [], []
        for pr in range(ML_HEADS // 2):
            q2 = _tiles(qk, 2 * pr, 2, rows).astype(BF16)
            k2f = _tiles(qk, ML_HEADS + 2 * pr, 2, rows)
            k2 = k2f.astype(BF16)
            s_cat.append(_dot_nt(q2, _bdiag2(k2[:, :LANE], k2[:, LANE:])))
            for j in range(2):
                h = 2 * pr + j
                v = proj[O_MV // LANE + h, rows, :].astype(BF16)
                cx = ml_c[h]
                qc.append(_dot(q2[:, j * LANE:(j + 1) * LANE], cx.astype(BF16)))
                wk = (w_end[:, h:h + 1] * k2f[:, j * LANE:(j + 1) * LANE]).astype(BF16)
                updx = _dot_tn(wk, jnp.concatenate([v, ones_col], axis=1))
                ml_c[h] = g_old[:, h:h + 1] * cx + g_new[:, h:h + 1] * updx
                vh.append(v)
        yield
        mxs, sms, dens, nums = [], [], [], []
        for h in range(ML_HEADS):
            d_row = dm_t[h:h + 1, :]
            m_h = m_old[:, h:h + 1]
            cm = jnp.max(jnp.where(causal, d_row, -jnp.inf), axis=1, keepdims=True)
            mx = jnp.maximum(cm, m_h)
            p = jnp.where(causal, jnp.exp(d_row - mx), 0.0)
            s = s_cat[h // 2][:, (h % 2) * LANE:(h % 2 + 1) * LANE] * p
            mxs.append(mx)
            sms.append(s.astype(BF16))
            dens.append(jnp.sum(s, axis=1, keepdims=True))
        for pr in range(ML_HEADS // 2):
            nums.append(_dot(jnp.concatenate([sms[2 * pr], sms[2 * pr + 1]], axis=1),
                             _bdiag2(vh[2 * pr], vh[2 * pr + 1])))
        yield
        for h in range(ML_HEADS):
            m_h = m_old[:, h:h + 1]
            w_inter = jnp.exp(m_h - mxs[h])
            num = nums[h // 2][:, (h % 2) * LANE:(h % 2 + 1) * LANE] + w_inter * qc[h][:, :ML_DH]
            den = dens[h] + w_inter * qc[h][:, ML_DH:ML_DH + 1]
            m_t = fc[:, h:h + 1] + mxs[h]
            hh = num / jnp.maximum(jnp.abs(den), jnp.exp(-m_t))
            og = _sigmoid(proj[O_MO // LANE + h, rows, :])
            obuf[GROUP_W // LANE + h, rows, :] = _rms(og * hh) * mnw_ref[:, h * ML_DH:(h + 1) * ML_DH]
        yield


def _block_diag(w):
    g, n, _ = w.shape
    eye = jnp.eye(g, dtype=w.dtype)
    return (eye[:, None, :, None] * w[:, :, None, :]).reshape(g * n, g * n)


def _odd_mixer(h, norm_w, w_in, lcw, lcb, wa, ba, wx, bx, lam, mcw, mcb, wq, wk, bi, bf, ml_norm, w_out, ts, grp):
    bsz, seq, _ = h.shape
    w_main = w_in.astype(BF16)[:, :O_IF]
    w_tail = w_in[:, O_IF:]
    w_tail = jnp.pad(w_tail, ((0, 0), (0, O_NP - O_IF - w_tail.shape[1]))).astype(BF16)
    wqk = jnp.concatenate([wq, wk], axis=-1).astype(BF16)
    bif = jnp.pad(jnp.concatenate([bi, bf]), (0, LANE - 2 * ML_HEADS)).reshape(1, LANE)
    tri = np.tril(np.ones((ML_CHUNK, ML_CHUNK), np.float32)).astype(BF16)
    r = np.arange(ts)
    time_of_row = (r % SUBLANE) * (ts // SUBLANE) + r // SUBLANE
    perm = (time_of_row[:, None] == r[None, :]).astype(BF16)
    row = lambda t: t.reshape(1, -1)

    const2 = lambda shape: pl.BlockSpec(shape, lambda b, s: (0, 0), pipeline_mode=pl.Buffered(1))
    const3 = lambda shape: pl.BlockSpec(shape, lambda b, s: (0, 0, 0), pipeline_mode=pl.Buffered(1))
    return pl.pallas_call(
        _odd_body,
        grid=(bsz // grp, seq // ts),
        in_specs=[
            pl.BlockSpec((grp, ts, D_MODEL), lambda b, s: (b, s, 0)),
            const2((1, D_MODEL)),
            const2((D_MODEL, O_IF)),
            const2((D_MODEL, LANE)),
            const2((CONV_W, LRU_W)), const2((1, LRU_W)),
            const2((CONV_W, GROUP_W)), const2((1, GROUP_W)),
            const2((LRU_W, LRU_W)), const2((1, LRU_W)),
            const2((LRU_W, LRU_W)), const2((1, LRU_W)),
            const2((1, LRU_W)),
            const3((ML_HEADS, ML_DH, 2 * ML_DH)),
            const2((1, LANE)),
            const2((ML_CHUNK, ML_CHUNK)),
            const2((1, GROUP_W)),
            const2((D_MODEL, D_MODEL)),
            const2((ts, ts)), const2((ts, ts)),
        ],
        out_specs=pl.BlockSpec((grp, ts, D_MODEL), lambda b, s: (b, s, 0)),
        out_shape=jax.ShapeDtypeStruct(h.shape, F32),
        scratch_shapes=[
            pltpu.VMEM((grp, O_NP // LANE, ts, LANE), F32),
            pltpu.VMEM((grp, (CONV_W - 1) * SUBLANE, LRU_W), F32),
            pltpu.VMEM((grp, (CONV_W - 1) * SUBLANE, GROUP_W), F32),
            pltpu.VMEM((grp, 2 * GROUP_W // LANE, ts, LANE), F32),
            pltpu.VMEM((grp, ts, LANE), F32),
            pltpu.VMEM((grp, ts, LANE), F32),
            pltpu.VMEM((grp, D_MODEL // LANE, ts, LANE), F32),
            pltpu.VMEM((grp, SUBLANE, LRU_W), F32),
            pltpu.VMEM((grp, ML_HEADS, ML_DH, 2 * ML_DH), F32),
            pltpu.VMEM((grp, SUBLANE, LANE), F32),
        ],
        compiler_params=pltpu.CompilerParams(
            dimension_semantics=("parallel", "arbitrary"), vmem_limit_bytes=VMEM_LIMIT),
        name="odd_mixer",
    )(h, row(norm_w), w_main, w_tail, lcw, row(lcb), mcw, row(mcb),
      _block_diag(wa).astype(BF16), row(ba), _block_diag(wx).astype(BF16), row(bx), row(lam),
      wqk, bif, tri, row(ml_norm), w_out.astype(BF16), perm, perm.T)


def kernel(x, ffn1_norm, ffn1_wgu, ffn1_wd, mix_norm, ffn2_norm, ffn2_wgu, ffn2_wd, e_w_in, e_w_lr_up, e_b_lr, e_head_norm, e_w_out, o_w_in, o_lru_conv_w, o_lru_conv_b, o_lru_wa, o_lru_ba, o_lru_wx, o_lru_bx, o_lru_lambda, o_ml_conv_w, o_ml_conv_b, o_ml_wq, o_ml_wk, o_ml_bi, o_ml_bf, o_ml_norm, o_w_out, final_norm):
    bsz, seq, d = x.shape
    depth = ffn1_norm.shape[0]
    h = x
    for layer in range(depth):
        j = layer // 2
        h = _ffn(h.reshape(bsz * seq, d), ffn1_norm[layer], ffn1_wgu, ffn1_wd, layer,
                 final_norm, False).reshape(bsz, seq, d)
        if layer % 2 == 0:
            h = _even_mixer(h, mix_norm[layer], e_w_in[j], e_w_lr_up[j], e_b_lr[j], e_head_norm[j],
                            e_w_out[j], MIX_TS, EVEN_G)
        else:
            h = _odd_mixer(h, mix_norm[layer], o_w_in[j], o_lru_conv_w[j], o_lru_conv_b[j],
                           o_lru_wa[j], o_lru_ba[j], o_lru_wx[j], o_lru_bx[j], o_lru_lambda[j],
                           o_ml_conv_w[j], o_ml_conv_b[j], o_ml_wq[j], o_ml_wk[j], o_ml_bi[j], o_ml_bf[j],
                           o_ml_norm[j], o_w_out[j], MIX_TS, ODD_G)
        h = _ffn(h.reshape(bsz * seq, d), ffn2_norm[layer], ffn2_wgu, ffn2_wd, layer,
                 final_norm, layer == depth - 1).reshape(bsz, seq, d)
    return h
```

```python
import functools

import numpy as np
import jax
import jax.numpy as jnp
from jax import lax
from jax.experimental import pallas as pl
from jax.experimental.pallas import tpu as pltpu

F32 = jnp.float32
BF16 = jnp.bfloat16

D_MODEL = 1024
D_FF = 2816
GROUP_W = D_MODEL // 2
EPS = 1e-6

RET_HEADS = 4
RET_DK = 128
RET_CHUNK = 128
ROPE_BASE = 10000.0

GLA_HEADS = 4
GLA_DK = 64
GLA_DV = 128
GLA_RANK = 16
GLA_TAU = 16.0
GLA_CHUNK = 64
GLA_QK = GLA_HEADS * GLA_DK

LRU_W = GROUP_W
LRU_BLOCKS = 8
LRU_BS = LRU_W // LRU_BLOCKS
LRU_C = 8.0
CONV_W = 4

ML_HEADS = 4
ML_DH = 128
ML_CHUNK = 128

LANE = 128
SUBLANE = 8
VMEM_LIMIT = 56 * 1024 * 1024

E_RQ, E_RK, E_RV, E_RG = 0, 512, 1024, 1536
E_GQ, E_GK, E_GV, E_GG, E_LR = 2048, 2304, 2560, 3072, 3584
E_NP = 3712
O_LY, O_LX, O_MU, O_MV, O_MO, O_IF = 0, 512, 1024, 1536, 2048, 2560
O_NP = 2688

FFN_TM = 1024
FFN_SUB = 256
FFN_FC = 256
FFN_GU_ROWS = 32
FFN_D_ROWS = 128
STAGE_DEPTH = 8
MIX_TS = 256
EVEN_G = 4
ODD_G = 4
MIX_SLAB = 512
OUT_SLAB = 256
EVEN_LAG = 4
ODD_LAG = 5
PAIR = 2 * LANE


def _dot(a, b):
    return jnp.dot(a, b, preferred_element_type=F32)


def _dot_nt(a, b):
    return lax.dot_general(a, b, (((1,), (1,)), ((), ())), preferred_element_type=F32)


def _dot_tn(a, b):
    return lax.dot_general(a, b, (((0,), (0,)), ((), ())), preferred_element_type=F32)


def _rms(x):
    return x * lax.rsqrt(jnp.mean(x * x, axis=-1, keepdims=True) + EPS)


def _sigmoid(x):
    return 1.0 / (1.0 + jnp.exp(-x))


def _silu(x):
    return x * _sigmoid(x)


def _log_sigmoid(x):
    return jnp.minimum(x, 0.0) - jnp.log1p(jnp.exp(-jnp.abs(x)))


def _split_bf16(x):
    hi = x.astype(BF16)
    lo = (x - hi.astype(F32)).astype(BF16)
    return hi, lo


def _bdiag2(a, b):
    z = jnp.zeros_like(a)
    return jnp.concatenate([jnp.concatenate([a, z], axis=1), jnp.concatenate([z, b], axis=1)], axis=0)


def _round_robin(gens, lag=0):
    gens = list(gens)
    done = [False] * len(gens)
    rnd = 0
    while not all(done):
        for i, gen in enumerate(gens):
            if done[i] or rnd < lag * i:
                continue
            try:
                next(gen)
            except StopIteration:
                done[i] = True
        rnd += 1
        yield


def _run(gen):
    for _ in gen:
        pass


def _stage_weight(layer, w_hbm, w_bf, stage, sem, axis, width):
    n = w_bf.shape[axis] // width
    depth = stage.shape[0]

    def window(j):
        sl = pl.ds(j * width, width)
        return (sl, slice(None)) if axis == 0 else (slice(None), sl)

    def copy(j):
        return pltpu.make_async_copy(w_hbm.at[(layer,) + window(j)], stage.at[j % depth], sem.at[j % depth])

    for j in range(min(depth - 1, n)):
        copy(j).start()
    yield
    for j in range(n):
        if j + depth - 1 < n:
            copy(j + depth - 1).start()
        copy(j).wait()
        w_bf[window(j)] = stage[j % depth].astype(BF16)


def _stage_all(*streams):
    for s in streams:
        next(s)
    for s in streams:
        _run(s)


def _ffn_body(final, layer, x_ref, nw_ref, wgu_hbm, wd_hbm, fw_ref, o_ref,
              act_ref, wgu_ref, wd_ref, stage_gu, stage_d, sem_gu, sem_d):
    @pl.when(pl.program_id(0) == 0)
    def _():
        _stage_all(_stage_weight(layer, wgu_hbm, wgu_ref, stage_gu, sem_gu, 0, FFN_GU_ROWS),
                   _stage_weight(layer, wd_hbm, wd_ref, stage_d, sem_d, 0, FFN_D_ROWS))

    for r0 in range(0, FFN_TM, FFN_SUB):
        rows = pl.ds(r0, FFN_SUB)
        x = x_ref[rows, :]
        xn = (_rms(x) * nw_ref[...]).astype(BF16)
        for j in range(D_FF // FFN_FC):
            c0 = j * FFN_FC
            g = _dot(xn, wgu_ref[:, c0:c0 + FFN_FC])
            u = _dot(xn, wgu_ref[:, D_FF + c0:D_FF + c0 + FFN_FC])
            act_ref[rows, c0:c0 + FFN_FC] = (_silu(g) * u).astype(BF16)
        h = x + 0.5 * _dot(act_ref[rows, :], wd_ref[...])
        if final:
            h = _rms(h) * fw_ref[...]
        o_ref[rows, :] = h


def _ffn(h, norm_w, wgu_all, wd_all, layer, final_w, final):
    t = h.shape[0]
    const = lambda shape: pl.BlockSpec(shape, lambda i: (0, 0), pipeline_mode=pl.Buffered(1))
    return pl.pallas_call(
        functools.partial(_ffn_body, final, layer),
        grid=(t // FFN_TM,),
        in_specs=[
            pl.BlockSpec((FFN_TM, D_MODEL), lambda i: (i, 0)),
            const((1, D_MODEL)),
            pl.BlockSpec(memory_space=pl.ANY),
            pl.BlockSpec(memory_space=pl.ANY),
            const((1, D_MODEL)),
        ],
        out_specs=pl.BlockSpec((FFN_TM, D_MODEL), lambda i: (i, 0)),
        out_shape=jax.ShapeDtypeStruct((t, D_MODEL), F32),
        scratch_shapes=[
            pltpu.VMEM((FFN_TM, D_FF), BF16),
            pltpu.VMEM((D_MODEL, 2 * D_FF), BF16),
            pltpu.VMEM((D_FF, D_MODEL), BF16),
            pltpu.VMEM((STAGE_DEPTH, FFN_GU_ROWS, 2 * D_FF), F32),
            pltpu.VMEM((STAGE_DEPTH, FFN_D_ROWS, D_MODEL), F32),
            pltpu.SemaphoreType.DMA((STAGE_DEPTH,)),
            pltpu.SemaphoreType.DMA((STAGE_DEPTH,)),
        ],
        compiler_params=pltpu.CompilerParams(
            dimension_semantics=("arbitrary",), vmem_limit_bytes=VMEM_LIMIT),
        name="ffn_final" if final else "ffn",
    )(h, norm_w.reshape(1, -1), wgu_all, wd_all, final_w.reshape(1, -1))


def _even_body(x_ref, nw_ref, win_ref, wtail_ref, wlr_ref, blr_ref, cos_ref, sin_ref,
               dmat_ref, winb_ref, wendb_ref, dec_ref, tri_ref, ctile_ref,
               hnw_ref, wout_ref, o_ref,
               proj, labuf, obuf, ret_s, gla_st):
    groups = x_ref.shape[0]
    ts = x_ref.shape[1]

    @pl.when(pl.program_id(1) == 0)
    def _():
        ret_s[...] = jnp.zeros_like(ret_s)
        gla_st[...] = jnp.zeros_like(gla_st)

    def sequence(g):
        x = x_ref[g]
        pg = proj.at[g]
        hn = (_rms(x) * nw_ref[...]).astype(BF16)
        for c0 in range(0, E_LR, MIX_SLAB):
            pg[:, c0:c0 + MIX_SLAB] = _dot(hn, win_ref[:, c0:c0 + MIX_SLAB])
            yield
        pg[:, E_LR:E_NP] = _dot(hn, wtail_ref[...])
        yield
        glr = pg[:, E_LR:E_LR + LANE].astype(BF16)
        labuf[g] = _log_sigmoid(_dot(glr, wlr_ref[...]) + blr_ref[...]) * (1.0 / GLA_TAU)
        cos = cos_ref[...]
        sin = sin_ref[...]
        for c0 in range(0, 2 * RET_HEADS * RET_DK, RET_DK):
            xx = pg[:, c0:c0 + RET_DK]
            r = xx * cos + pltpu.roll(xx, RET_DK // 2, 1) * sin
            if c0 >= E_RK:
                r = r * (RET_DK ** -0.5)
            pg[:, c0:c0 + RET_DK] = r
        yield
        yield from _round_robin([
            _ret_chain(pg, obuf.at[g], ret_s.at[g], dmat_ref, winb_ref, wendb_ref, dec_ref, ts),
            _gla_chain(pg, labuf.at[g], obuf.at[g], gla_st.at[g], tri_ref, ctile_ref, ts)])
        o_ref[g] = x + _dot((obuf[g] * hnw_ref[...]).astype(BF16), wout_ref[...])
        yield

    _run(_round_robin([sequence(g) for g in range(groups)], lag=EVEN_LAG))


def _ret_chain(proj, obuf, ret_s, dmat_ref, winb_ref, wendb_ref, dec_ref, ts):
    for c in range(ts // RET_CHUNK):
        rows = pl.ds(c * RET_CHUNK, RET_CHUNK)
        stage = []
        for pr in range(RET_HEADS // 2):
            lo = pr * PAIR
            q2 = proj[rows, E_RQ + lo:E_RQ + lo + PAIR].astype(BF16)
            k2 = proj[rows, E_RK + lo:E_RK + lo + PAIR].astype(BF16)
            v2 = proj[rows, E_RV + lo:E_RV + lo + PAIR]
            s0 = ret_s[2 * pr]
            s1 = ret_s[2 * pr + 1]
            s_cat = _dot_nt(q2, _bdiag2(k2[:, :LANE], k2[:, LANE:]))
            inter = _dot(q2, _bdiag2(s0.astype(BF16), s1.astype(BF16)))
            full = _dot_tn(k2, (wendb_ref[pr] * v2).astype(BF16))
            ret_s[2 * pr] = dec_ref[2 * pr] * s0 + full[:LANE, :LANE]
            ret_s[2 * pr + 1] = dec_ref[2 * pr + 1] * s1 + full[LANE:, LANE:]
            stage.append((s_cat, inter, v2.astype(BF16)))
        yield
        outs = []
        for pr in range(RET_HEADS // 2):
            s_cat, inter, vb = stage[pr]
            p = (s_cat * dmat_ref[pr]).astype(BF16)
            outs.append(_dot(p, _bdiag2(vb[:, :LANE], vb[:, LANE:])) + inter * winb_ref[pr])
        yield
        for h in range(RET_HEADS):
            o = outs[h // 2][:, (h % 2) * LANE:(h % 2 + 1) * LANE]
            g = proj[rows, E_RG + h * RET_DK:E_RG + (h + 1) * RET_DK]
            obuf[rows, h * RET_DK:(h + 1) * RET_DK] = _rms(o) * _silu(g)
        yield


def _gla_chain(proj, labuf, obuf, gla_st, tri_ref, ctile_ref, ts):
    lane_head = lax.broadcasted_iota(jnp.int32, (1, GLA_QK), 1) // GLA_DK

    def stack_heads(z):
        return jnp.concatenate(
            [jnp.where(lane_head == h, z, 0.0) for h in range(GLA_HEADS)], axis=0).astype(BF16)

    zb = jnp.zeros((GLA_CHUNK, GLA_DV), BF16)
    for c in range(ts // GLA_CHUNK):
        rows = pl.ds(c * GLA_CHUNK, GLA_CHUNK)
        la_hi, la_lo = _split_bf16(labuf[rows, :])
        tri = tri_ref[...]
        b = _dot(tri, jnp.concatenate([la_hi, la_lo], axis=0))
        yield
        b_mid = b[GLA_CHUNK // 2 - 1:GLA_CHUNK // 2, :]
        b_end = b[GLA_CHUNK - 1:GLA_CHUNK, :]
        q = proj[rows, E_GQ:E_GQ + GLA_QK]
        k = proj[rows, E_GK:E_GK + GLA_QK] * (GLA_DK ** -0.5)
        vb = proj[rows, E_GV:E_GV + GLA_HEADS * GLA_DV].astype(BF16)
        vh = [vb[:, h * GLA_DV:(h + 1) * GLA_DV] for h in range(GLA_HEADS)]
        s_cat = _dot_nt((q * jnp.exp(b - b_mid)).astype(BF16), stack_heads(k * jnp.exp(b_mid - b)))
        st = gla_st[...]
        inter = _dot_nt(stack_heads(q * jnp.exp(b)), st.astype(BF16))
        upd = _dot_tn(jnp.concatenate(vh, axis=0), stack_heads(k * jnp.exp(b_end - b)))
        gla_st[...] = jnp.exp(b_end) * st + upd
        yield
        p = (s_cat * ctile_ref[...]).astype(BF16)
        vbd = jnp.concatenate(
            [jnp.concatenate([vh[h] if j == h else zb for j in range(GLA_HEADS)], axis=1)
             for h in range(GLA_HEADS)], axis=0)
        out_cat = _dot(p, vbd)
        yield
        for h in range(GLA_HEADS):
            g = proj[rows, E_GG + h * GLA_DV:E_GG + (h + 1) * GLA_DV]
            o = out_cat[:, h * GLA_DV:(h + 1) * GLA_DV] + inter[h * GLA_CHUNK:(h + 1) * GLA_CHUNK, :]
            obuf[rows, GROUP_W + h * GLA_DV:GROUP_W + (h + 1) * GLA_DV] = _rms(o) * _silu(g)
        yield


def _pair_lanes(t):
    return np.concatenate([t[0::2], t[1::2]], axis=-1)


def _even_mixer(h, norm_w, w_in, w_lr, b_lr, head_norm, w_out, ts, grp):
    bsz, seq, _ = h.shape
    f32 = np.float32
    hh = np.arange(RET_HEADS, dtype=f32)
    log_g = np.log1p(-np.exp2(f32(-5.0) - hh)).astype(f32)
    idx = np.arange(RET_CHUNK, dtype=f32)
    rel = idx[:, None] - idx[None, :]
    causal = rel >= 0
    dmat = np.where(causal, np.exp(log_g[:, None, None] * np.where(causal, rel, f32(0.0))), f32(0.0)).astype(f32)
    full = (RET_HEADS, RET_CHUNK, RET_DK)
    pair = (RET_HEADS // 2, RET_CHUNK, PAIR)
    w_end = np.broadcast_to(np.exp(log_g[:, None] * (f32(RET_CHUNK - 1.0) - idx)[None, :])[:, :, None], full)
    w_inb = np.broadcast_to(np.exp(log_g[:, None] * (idx + f32(1.0))[None, :])[:, :, None], full)
    dec = np.ascontiguousarray(np.broadcast_to(np.exp(log_g * f32(RET_CHUNK))[:, None, None], full), dtype=f32)
    w_end = np.ascontiguousarray(w_end, dtype=f32)
    w_inb = np.ascontiguousarray(w_inb, dtype=f32)

    half = RET_DK // 2
    inv = (f32(ROPE_BASE) ** (-np.arange(half, dtype=f32) / f32(half))).astype(f32)
    ang = np.arange(seq).astype(f32)[:, None] * inv[None, :]
    cos2 = np.concatenate([np.cos(ang), np.cos(ang)], axis=-1).astype(f32)
    sin2 = np.concatenate([-np.sin(ang), np.sin(ang)], axis=-1).astype(f32)

    tril = np.tril(np.ones((GLA_CHUNK, GLA_CHUNK), f32))
    ctile = np.tile(tril, (1, GLA_HEADS))

    w_main = w_in.astype(BF16)[:, :E_LR]
    w_tail = w_in[:, E_LR:]
    w_tail = jnp.pad(w_tail, ((0, 0), (0, E_NP - E_LR - w_tail.shape[1]))).astype(BF16)
    w_lr_p = jnp.pad(w_lr, ((0, LANE - GLA_RANK), (0, 0))).astype(BF16)

    const2 = lambda shape: pl.BlockSpec(shape, lambda b, s: (0, 0), pipeline_mode=pl.Buffered(1))
    const3 = lambda shape: pl.BlockSpec(shape, lambda b, s: (0, 0, 0), pipeline_mode=pl.Buffered(1))
    return pl.pallas_call(
        _even_body,
        grid=(bsz // grp, seq // ts),
        in_specs=[
            pl.BlockSpec((grp, ts, D_MODEL), lambda b, s: (b, s, 0)),
            const2((1, D_MODEL)),
            const2((D_MODEL, E_LR)),
            const2((D_MODEL, LANE)),
            const2((LANE, GLA_QK)),
            const2((1, GLA_QK)),
            pl.BlockSpec((ts, RET_DK), lambda b, s: (s, 0)),
            pl.BlockSpec((ts, RET_DK), lambda b, s: (s, 0)),
            const3(pair), const3(pair), const3(pair), const3(full),
            const2((GLA_CHUNK, 2 * GLA_CHUNK)),
            const2((GLA_CHUNK, GLA_QK)),
            const2((1, D_MODEL)),
            const2((D_MODEL, D_MODEL)),
        ],
        out_specs=pl.BlockSpec((grp, ts, D_MODEL), lambda b, s: (b, s, 0)),
        out_shape=jax.ShapeDtypeStruct(h.shape, F32),
        scratch_shapes=[
            pltpu.VMEM((grp, ts, E_NP), F32),
            pltpu.VMEM((grp, ts, GLA_QK), F32),
            pltpu.VMEM((grp, ts, D_MODEL), F32),
            pltpu.VMEM((grp,) + full, F32),
            pltpu.VMEM((grp, GLA_DV, GLA_QK), F32),
        ],
        compiler_params=pltpu.CompilerParams(
            dimension_semantics=("parallel", "arbitrary"), vmem_limit_bytes=VMEM_LIMIT),
        name="even_mixer",
    )(h, norm_w.reshape(1, -1), w_main, w_tail, w_lr_p, b_lr.reshape(1, -1), cos2, sin2,
      _pair_lanes(dmat), _pair_lanes(w_inb), _pair_lanes(w_end), dec, np.tile(tril, (1, 2)).astype(BF16), ctile,
      head_norm.reshape(1, -1), w_out.astype(BF16))


def _tiles(ref, t0, n, rows=slice(None)):
    return jnp.concatenate([ref[t0 + t, rows, :] for t in range(n)], axis=1)


def _set_tiles(ref, t0, val):
    for t in range(val.shape[1] // LANE):
        ref[t0 + t] = val[:, t * LANE:(t + 1) * LANE]


def _permute_rows(perm_ref, x_bf16):
    return _dot(perm_ref[...], x_bf16)


def _causal_conv(tail, xb, w_ref, b_ref):
    ts = xb.shape[0]
    keep = (CONV_W - 1) * SUBLANE
    last = xb[ts - keep:, :]
    prev = tail[...]
    first = lax.broadcasted_iota(jnp.int32, (SUBLANE, 1), 0) == 0
    fix = [jnp.where(first, pltpu.roll(prev[m * SUBLANE:(m + 1) * SUBLANE, :], 1, 0),
                     pltpu.roll(last[m * SUBLANE:(m + 1) * SUBLANE, :], 1, 0)) for m in range(CONV_W - 1)]
    tail[...] = last
    xpad = jnp.concatenate(fix + [xb], axis=0)
    acc = b_ref[...] + w_ref[CONV_W - 1:CONV_W, :] * xb
    for j in range(CONV_W - 1):
        acc = acc + w_ref[j:j + 1, :] * xpad[j * SUBLANE:j * SUBLANE + ts, :]
    return acc


def _odd_body(x_ref, nw_ref, win_ref, wtail_ref, lcw_ref, lcb_ref, mcw_ref, mcb_ref,
              wa_ref, ba_ref, wx_ref, bx_ref, lam_ref, wqk_ref, bif_ref, tri_ref,
              mnw_ref, wout_ref, perm_ref, unperm_ref, o_ref,
              proj, ltail, mtail, qk, ibuf, fbuf, obuf, lru_h, ml_c, ml_m):
    groups = x_ref.shape[0]
    ts = x_ref.shape[1]

    @pl.when(pl.program_id(1) == 0)
    def _():
        ltail[...] = jnp.zeros_like(ltail)
        mtail[...] = jnp.zeros_like(mtail)
        lru_h[...] = jnp.zeros_like(lru_h)
        ml_c[...] = jnp.zeros_like(ml_c)
        ml_m[...] = jnp.zeros_like(ml_m)

    def sequence(g):
        x = x_ref[g]
        hn = (_rms(x) * nw_ref[...]).astype(BF16)
        hn_blocked = _permute_rows(perm_ref, hn).astype(BF16)
        for c0 in range(0, O_IF, MIX_SLAB):
            lhs = hn_blocked if c0 + MIX_SLAB <= O_MV else hn
            _set_tiles(proj.at[g], c0 // LANE, _dot(lhs, win_ref[:, c0:c0 + MIX_SLAB]))
            yield
        proj[g, O_IF // LANE] = _dot(hn, wtail_ref[...])
        yield
        yield from _round_robin([
            _lru_chain(proj.at[g], ltail.at[g], obuf.at[g], lru_h.at[g],
                       lcw_ref, lcb_ref, wa_ref, ba_ref, wx_ref, bx_ref, lam_ref, unperm_ref, ts),
            _mlstm_chain(proj.at[g], mtail.at[g], qk.at[g], ibuf.at[g], fbuf.at[g], obuf.at[g],
                         ml_c.at[g], ml_m.at[g], mcw_ref, mcb_ref, wqk_ref, bif_ref, tri_ref, mnw_ref,
                         unperm_ref, ts)])
        ob = _tiles(obuf.at[g], 0, D_MODEL // LANE).astype(BF16)
        for c0 in range(0, D_MODEL, OUT_SLAB):
            o_ref[g, :, c0:c0 + OUT_SLAB] = x[:, c0:c0 + OUT_SLAB] + _dot(ob, wout_ref[:, c0:c0 + OUT_SLAB])
            yield

    _run(_round_robin([sequence(g) for g in range(groups)], lag=ODD_LAG))


def _lru_chain(proj, ltail, obuf, lru_h, lcw_ref, lcb_ref, wa_ref, ba_ref, wx_ref, bx_ref, lam_ref, unperm_ref, ts):
    lxc = _causal_conv(ltail, _tiles(proj, O_LX // LANE, LRU_W // LANE), lcw_ref, lcb_ref)
    xb = lxc.astype(BF16)
    def gate(w_ref, b_ref):
        pre = jnp.concatenate([_dot(xb[:, c0:c0 + PAIR], w_ref[c0:c0 + PAIR, c0:c0 + PAIR])
                               for c0 in range(0, LRU_W, PAIR)], axis=1)
        return _sigmoid(pre + b_ref[...])

    r = gate(wa_ref, ba_ref)
    i = gate(wx_ref, bx_ref)
    yield
    lam = lam_ref[...]
    softplus_neg_lam = jnp.maximum(-lam, 0.0) + jnp.log1p(jnp.exp(-jnp.abs(lam)))
    n = ts // SUBLANE
    sub = lax.broadcasted_iota(jnp.int32, (SUBLANE, 1), 0)
    outs = []
    for t0 in range(0, LRU_W, LANE):
        cols = slice(t0, t0 + LANE)
        log_a = (-LRU_C) * r[:, cols] * softplus_neg_lam[:, cols]
        a = jnp.exp(log_a)
        u = jnp.sqrt(-jnp.tanh(log_a) * (a * a + 1.0)) * (i[:, cols] * lxc[:, cols])
        hs, ps = [u[0:SUBLANE, :]], [a[0:SUBLANE, :]]
        for j in range(1, n):
            aj = a[j * SUBLANE:(j + 1) * SUBLANE, :]
            hs.append(aj * hs[-1] + u[j * SUBLANE:(j + 1) * SUBLANE, :])
            ps.append(aj * ps[-1])
        pt, ht = ps[-1], hs[-1]
        d = 1
        while d < SUBLANE:
            keep = sub >= d
            p_s = jnp.where(keep, pltpu.roll(pt, d, 0), 1.0)
            h_s = jnp.where(keep, pltpu.roll(ht, d, 0), 0.0)
            ht = pt * h_s + ht
            pt = pt * p_s
            d *= 2
        hprev = lru_h[0:1, cols]
        block_end = ht + pt * hprev
        carry = jnp.where(sub == 0, hprev, pltpu.roll(block_end, 1, 0))
        lru_h[0:1, cols] = block_end[SUBLANE - 1:SUBLANE, :]
        hfull = jnp.concatenate([hs[j] + ps[j] * carry for j in range(n)], axis=0)
        ly = proj[(O_LY + t0) // LANE]
        gelu = ly * (0.5 * (1.0 + jnp.tanh(np.float32(np.sqrt(2.0 / np.pi)) * (ly + 0.044715 * (ly * ly * ly)))))
        outs.append((hfull * gelu).astype(BF16))
        yield
    _set_tiles(obuf, 0, _permute_rows(unperm_ref, jnp.concatenate(outs, axis=1)))
    yield


def _mlstm_chain(proj, mtail, qk, ibuf, fbuf, obuf, ml_c, ml_m, mcw_ref, mcb_ref, wqk_ref, bif_ref, tri_ref,
                 mnw_ref, unperm_ref, ts):
    mc_blocked = _silu(_causal_conv(mtail, _tiles(proj, O_MU // LANE, GROUP_W // LANE), mcw_ref, mcb_ref))
    mc = _permute_rows(unperm_ref, mc_blocked.astype(BF16)).astype(BF16)
    for h in range(ML_HEADS):
        qkh = _dot(mc[:, h * ML_DH:(h + 1) * ML_DH], wqk_ref[h])
        qk[h] = qkh[:, :ML_DH]
        qk[ML_HEADS + h] = qkh[:, ML_DH:] * (ML_DH ** -0.5)
    gates = proj[O_IF // LANE] + bif_ref[...]
    ibuf[...] = gates
    fbuf[...] = pltpu.roll(_log_sigmoid(gates), LANE - ML_HEADS, 1)
    yield

    row_i = lax.broadcasted_iota(jnp.int32, (ML_CHUNK, ML_CHUNK), 0)
    col_i = lax.broadcasted_iota(jnp.int32, (ML_CHUNK, ML_CHUNK), 1)
    causal = row_i >= col_i
    ones_col = jnp.where(col_i == 0, 1.0, 0.0).astype(BF16)

    for c in range(ts // ML_CHUNK):
        rows = pl.ds(c * ML_CHUNK, ML_CHUNK)
        f_hi, f_lo = _split_bf16(fbuf[rows, :])
        tri = tri_ref[...]
        fc = _dot(tri, f_hi) + _dot(tri, f_lo)
        yield
        dm = ibuf[rows, :] - fc
        dm_t = dm.T
        b_end = fc[ML_CHUNK - 1:ML_CHUNK, :]
        a_col = b_end + dm
        a_max = jnp.max(a_col, axis=0, keepdims=True)
        w_end = jnp.exp(a_col - a_max)
        m_old = ml_m[0:1, :]
        m_new = jnp.maximum(b_end + m_old, a_max)
        g_old = jnp.exp(b_end + m_old - m_new)
        g_new = jnp.exp(a_max - m_new)
        ml_m[0:1, :] = m_new
        s_cat, qc, vh = [], [], []
        for pr in range(ML_HEADS // 2):
            q2 = _tiles(qk, 2 * pr, 2, rows).astype(BF16)
            k2f = _tiles(qk, ML_HEADS + 2 * pr, 2, rows)
            k2 = k2f.astype(BF16)
            s_cat.append(_dot_nt(q2, _bdiag2(k2[:, :LANE], k2[:, LANE:])))
            for j in range(2):
                h = 2 * pr + j
                v = proj[O_MV // LANE + h, rows, :].astype(BF16)
                cx = ml_c[h]
                qc.append(_dot(q2[:, j * LANE:(j + 1) * LANE], cx.astype(BF16)))
                wk = (w_end[:, h:h + 1] * k2f[:, j * LANE:(j + 1) * LANE]).astype(BF16)
                updx = _dot_tn(wk, jnp.concatenate([v, ones_col], axis=1))
                ml_c[h] = g_old[:, h:h + 1] * cx + g_new[:, h:h + 1] * updx
                vh.append(v)
        yield
        mxs, sms, dens, nums = [], [], [], []
        for h in range(ML_HEADS):
            d_row = dm_t[h:h + 1, :]
            m_h = m_old[:, h:h + 1]
            cm = jnp.max(jnp.where(causal, d_row, -jnp.inf), axis=1, keepdims=True)
            mx = jnp.maximum(cm, m_h)
            p = jnp.where(causal, jnp.exp(d_row - mx), 0.0)
            s = s_cat[h // 2][:, (h % 2) * LANE:(h % 2 + 1) * LANE] * p
            mxs.append(mx)
            sms.append(s.astype(BF16))
            dens.append(jnp.sum(s, axis=1, keepdims=True))
        for pr in range(ML_HEADS // 2):
            nums.append(_dot(jnp.concatenate([sms[2 * pr], sms[2 * pr + 1]], axis=1),
                             _bdiag2(vh[2 * pr], vh[2 * pr + 1])))
        yield
        for h in range(ML_HEADS):
            m_h = m_old[:, h:h + 1]
            w_inter = jnp.exp(m_h - mxs[h])
            num = nums[h // 2][:, (h % 2) * LANE:(h % 2 + 1) * LANE] + w_inter * qc[h][:, :ML_DH]
            den = dens[h] + w_inter * qc[h][:, ML_DH:ML_DH + 1]
            m_t = fc[:, h:h + 1] + mxs[h]
            hh = num / jnp.maximum(jnp.abs(den), jnp.exp(-m_t))
            og = _sigmoid(proj[O_MO // LANE + h, rows, :])
            obuf[GROUP_W // LANE + h, rows, :] = _rms(og * hh) * mnw_ref[:, h * ML_DH:(h + 1) * ML_DH]
        yield


def _block_diag(w):
    g, n, _ = w.shape
    eye = jnp.eye(g, dtype=w.dtype)
    return (eye[:, None, :, None] * w[:, :, None, :]).reshape(g * n, g * n)


def _odd_mixer(h, norm_w, w_in, lcw, lcb, wa, ba, wx, bx, lam, mcw, mcb, wq, wk, bi, bf, ml_norm, w_out, ts, grp):
    bsz, seq, _ = h.shape
    w_main = w_in.astype(BF16)[:, :O_IF]
    w_tail = w_in[:, O_IF:]
    w_tail = jnp.pad(w_tail, ((0, 0), (0, O_NP - O_IF - w_tail.shape[1]))).astype(BF16)
    wqk = jnp.concatenate([wq, wk], axis=-1).astype(BF16)
    bif = jnp.pad(jnp.concatenate([bi, bf]), (0, LANE - 2 * ML_HEADS)).reshape(1, LANE)
    tri = np.tril(np.ones((ML_CHUNK, ML_CHUNK), np.float32)).astype(BF16)
    r = np.arange(ts)
    time_of_row = (r % SUBLANE) * (ts // SUBLANE) + r // SUBLANE
    perm = (time_of_row[:, None] == r[None, :]).astype(BF16)
    row = lambda t: t.reshape(1, -1)

    const2 = lambda shape: pl.BlockSpec(shape, lambda b, s: (0, 0), pipeline_mode=pl.Buffered(1))
    const3 = lambda shape: pl.BlockSpec(shape, lambda b, s: (0, 0, 0), pipeline_mode=pl.Buffered(1))
    return pl.pallas_call(
        _odd_body,
        grid=(bsz // grp, seq // ts),
        in_specs=[
            pl.BlockSpec((grp, ts, D_MODEL), lambda b, s: (b, s, 0)),
            const2((1, D_MODEL)),
            const2((D_MODEL, O_IF)),
            const2((D_MODEL, LANE)),
            const2((CONV_W, LRU_W)), const2((1, LRU_W)),
            const2((CONV_W, GROUP_W)), const2((1, GROUP_W)),
            const2((LRU_W, LRU_W)), const2((1, LRU_W)),
            const2((LRU_W, LRU_W)), const2((1, LRU_W)),
            const2((1, LRU_W)),
            const3((ML_HEADS, ML_DH, 2 * ML_DH)),
            const2((1, LANE)),
            const2((ML_CHUNK, ML_CHUNK)),
            const2((1, GROUP_W)),
            const2((D_MODEL, D_MODEL)),
            const2((ts, ts)), const2((ts, ts)),
        ],
        out_specs=pl.BlockSpec((grp, ts, D_MODEL), lambda b, s: (b, s, 0)),
        out_shape=jax.ShapeDtypeStruct(h.shape, F32),
        scratch_shapes=[
            pltpu.VMEM((grp, O_NP // LANE, ts, LANE), F32),
            pltpu.VMEM((grp, (CONV_W - 1) * SUBLANE, LRU_W), F32),
            pltpu.VMEM((grp, (CONV_W - 1) * SUBLANE, GROUP_W), F32),
            pltpu.VMEM((grp, 2 * GROUP_W // LANE, ts, LANE), F32),
            pltpu.VMEM((grp, ts, LANE), F32),
            pltpu.VMEM((grp, ts, LANE), F32),
            pltpu.VMEM((grp, D_MODEL // LANE, ts, LANE), F32),
            pltpu.VMEM((grp, SUBLANE, LRU_W), F32),
            pltpu.VMEM((grp, ML_HEADS, ML_DH, 2 * ML_DH), F32),
            pltpu.VMEM((grp, SUBLANE, LANE), F32),
        ],
        compiler_params=pltpu.CompilerParams(
            dimension_semantics=("parallel", "arbitrary"), vmem_limit_bytes=VMEM_LIMIT),
        name="odd_mixer",
    )(h, row(norm_w), w_main, w_tail, lcw, row(lcb), mcw, row(mcb),
      _block_diag(wa).astype(BF16), row(ba), _block_diag(wx).astype(BF16), row(bx), row(lam),
      wqk, bif, tri, row(ml_norm), w_out.astype(BF16), perm, perm.T)


def kernel(x, ffn1_norm, ffn1_wgu, ffn1_wd, mix_norm, ffn2_norm, ffn2_wgu, ffn2_wd, e_w_in, e_w_lr_up, e_b_lr, e_head_norm, e_w_out, o_w_in, o_lru_conv_w, o_lru_conv_b, o_lru_wa, o_lru_ba, o_lru_wx, o_lru_bx, o_lru_lambda, o_ml_conv_w, o_ml_conv_b, o_ml_wq, o_ml_wk, o_ml_bi, o_ml_bf, o_ml_norm, o_w_out, final_norm):
    bsz, seq, d = x.shape
    depth = ffn1_norm.shape[0]
    h = x
    for layer in range(depth):
        j = layer // 2
        h = _ffn(h.reshape(bsz * seq, d), ffn1_norm[layer], ffn1_wgu, ffn1_wd, layer,
                 final_norm, False).reshape(bsz, seq, d)
        if layer % 2 == 0:
            h = _even_mixer(h, mix_norm[layer], e_w_in[j], e_w_lr_up[j], e_b_lr[j], e_head_norm[j],
                            e_w_out[j], MIX_TS, EVEN_G)
        else:
            h = _odd_mixer(h, mix_norm[layer], o_w_in[j], o_lru_conv_w[j], o_lru_conv_b[j],
                           o_lru_wa[j], o_lru_ba[j], o_lru_wx[j], o_lru_bx[j], o_lru_lambda[j],
                           o_ml_conv_w[j], o_ml_conv_b[j], o_ml_wq[j], o_ml_wk[j], o_ml_bi[j], o_ml_bf[j],
                           o_ml_norm[j], o_w_out[j], MIX_TS, ODD_G)
        h = _ffn(h.reshape(bsz * seq, d), ffn2_norm[layer], ffn2_wgu, ffn2_wd, layer,
                 final_norm, layer == depth - 1).reshape(bsz, seq, d)
    return h
```
